```python
import jax, jax.numpy as jnp
from jax import lax
import numpy as np

D_MODEL = 1024
BATCH = 8
SEQ = 4096
DEPTH = 4

RET_HEADS = 4
RET_DK = 64
RET_DV = 64
RET_CHUNK = 128
DIL_HEADS = 6
DIL_DH = 64
DIL_PATTERNS = ((128, 1), (512, 4), (2048, 16))
MLA_HEADS = 6
MLA_Q_RANK = 384
MLA_KV_RANK = 256
MLA_NOPE = 64
MLA_ROPE = 32
MLA_V = 64
Q_BLOCK = 128
RET_IN = RET_HEADS * (2 * RET_DK + 2 * RET_DV)
DIL_IN = 3 * DIL_HEADS * DIL_DH
MLA_IN = MLA_Q_RANK + MLA_KV_RANK + MLA_ROPE
D_IN = RET_IN + DIL_IN + MLA_IN
MIX_OUT = RET_HEADS * RET_DV + DIL_HEADS * DIL_DH + MLA_HEADS * MLA_V
D_FF = 2816
N_EXPERTS = 8
TOP_K = 2
D_FF_EXPERT = 3584
ROPE_THETA = 10000.0
LN_EPS = 1e-5
RMS_EPS = 1e-6
DEEPNORM_ALPHA = (2.0 * DEPTH) ** 0.25
DEEPNORM_BETA = (8.0 * DEPTH) ** -0.25
N_DENSE = (DEPTH + 1) // 2
N_MOE = DEPTH // 2

kernel_name = 'hybrid_ret_dilated_mla_moe_deepnorm'


def layer_norm(x, g, b):
    xf = x.astype(jnp.float32)
    mu = jnp.mean(xf, -1, keepdims=True)
    var = jnp.mean(jnp.square(xf - mu), -1, keepdims=True)
    return ((xf - mu) * lax.rsqrt(var + LN_EPS) * g + b).astype(x.dtype)


def rms_norm(x, g):
    xf = x.astype(jnp.float32)
    return (xf * lax.rsqrt(jnp.mean(jnp.square(xf), -1, keepdims=True) + RMS_EPS) * g).astype(x.dtype)


def rope_tables(positions, dim):
    inv = ROPE_THETA ** (-jnp.arange(0, dim, 2, dtype=jnp.float32) / dim)
    ang = positions.astype(jnp.float32)[..., None] * inv
    return jnp.cos(ang), jnp.sin(ang)


def apply_rope(x, cos, sin):
    c = cos[:, :, None, :].astype(x.dtype)
    s = sin[:, :, None, :].astype(x.dtype)
    x1, x2 = jnp.split(x, 2, axis=-1)
    return jnp.concatenate([x1 * c - x2 * s, x2 * c + x1 * s], axis=-1)


def retention(q, k, v, g, gn_g):
    B, S, H, dk = q.shape
    dv = v.shape[-1]
    C = RET_CHUNK
    N = S // C
    dt = q.dtype
    log_g = jnp.log(1.0 - 2.0 ** (-5.0 - jnp.arange(H, dtype=jnp.float32)))
    idx = jnp.arange(C, dtype=jnp.float32)
    diff = idx[:, None] - idx[None, :]
    intra = jnp.where(diff >= 0, jnp.exp(jnp.maximum(diff, 0.0) * log_g[:, None, None]), 0.0).astype(dt)
    xi = jnp.exp((idx[:, None] + 1.0) * log_g[None, :]).astype(dt)
    zeta = jnp.exp((C - 1.0 - idx[:, None]) * log_g[None, :]).astype(dt)
    chunk_decay = jnp.exp(C * log_g).astype(dt)[None, :, None, None]
    qc = q.reshape(B, N, C, H, dk)
    kc = k.reshape(B, N, C, H, dk) * (dk ** -0.5)
    vc = v.reshape(B, N, C, H, dv)
    s = jnp.einsum('bnihd,bnjhd->bnhij', qc, kc) * intra
    inner = jnp.einsum('bnhij,bnjhe->bnihe', s, vc)
    kv = jnp.einsum('bnjhd,bnjhe->nbhde', kc * zeta[:, :, None], vc)

    def step(state, kv_n):
        return state * chunk_decay + kv_n, state

    _, prev = lax.scan(step, jnp.zeros((B, H, dk, dv), dt), kv)
    cross = jnp.einsum('bnihd,nbhde->bnihe', qc, prev) * xi[:, :, None]
    o = (inner + cross).reshape(B, S, H, dv).astype(jnp.float32)
    mu = jnp.mean(o, -1, keepdims=True)
    var = jnp.mean(jnp.square(o - mu), -1, keepdims=True)
    o = ((o - mu) * lax.rsqrt(var + LN_EPS)).reshape(B, S, H * dv) * gn_g
    return jax.nn.silu(g) * o.astype(dt)


def dilated_branch(q, k, v, window, dilation):
    B, S, H, dh = q.shape
    L = S // dilation
    W = window // dilation
    nb = -(-L // W)
    Lp = nb * W
    R = B * dilation

    def to_sub(x):
        x = x.reshape(B, L, dilation, H, dh).transpose(0, 2, 1, 3, 4).reshape(R, L, H, dh)
        return jnp.pad(x, ((0, 0), (0, Lp - L), (0, 0), (0, 0)))

    def banded(x):
        cur = x.reshape(R, nb, W, H, dh)
        prev = jnp.pad(cur, ((0, 0), (1, 0), (0, 0), (0, 0), (0, 0)))[:, :nb]
        return jnp.concatenate([prev, cur], axis=2)

    qb = to_sub(q).reshape(R, nb, W, H, dh)
    kb = banded(to_sub(k))
    vb = banded(to_sub(v))
    s = jnp.einsum('rnqhd,rnkhd->rnhqk', qb, kb).astype(jnp.float32) * (dh ** -0.5)
    qi = jnp.arange(W)[:, None]
    kj = jnp.arange(2 * W)[None, :]
    dist = W + qi - kj
    band = (dist >= 0) & (dist <= W)
    first = jnp.arange(nb)[:, None, None] > 0
    valid = band[None] & (first | (kj >= W)[None])
    s = jnp.where(valid[None, :, None], s, -jnp.inf)
    m = jnp.max(s, -1, keepdims=True)
    p = jnp.exp(s - m)
    l = jnp.sum(p, -1, keepdims=True)
    o = jnp.einsum('rnhqk,rnkhd->rnqhd', (p / l).astype(v.dtype), vb)
    lse = (m + jnp.log(l))[..., 0]
    o = o.reshape(R, Lp, H, dh)[:, :L]
    o = o.reshape(B, dilation, L, H, dh).transpose(0, 2, 1, 3, 4).reshape(B, S, H, dh)
    lse = jnp.moveaxis(lse, 2, 3).reshape(R, Lp, H)[:, :L]
    lse = lse.reshape(B, dilation, L, H).transpose(0, 2, 1, 3).reshape(B, S, H)
    return o, lse


def dilated_attention(q, k, v):
    outs, lses = [], []
    for window, dilation in DIL_PATTERNS:
        o, lse = dilated_branch(q, k, v, window, dilation)
        outs.append(o)
        lses.append(lse)
    wts = jax.nn.softmax(jnp.stack(lses, 0), axis=0).astype(q.dtype)
    return jnp.einsum('pbsh,pbshd->bshd', wts, jnp.stack(outs, 0))


def mla_attention(qn, qr, kn, kr, v):
    B, S, H, _ = qn.shape
    nb = S // Q_BLOCK
    scale = (MLA_NOPE + MLA_ROPE) ** -0.5
    kpos = jnp.arange(S)

    def blocks(x):
        return jnp.moveaxis(x.reshape(B, nb, Q_BLOCK, *x.shape[2:]), 1, 0)

    def one(args):
        qn_b, qr_b, start = args
        s = jnp.einsum('bqhd,bkhd->bhqk', qn_b, kn) + jnp.einsum('bqhd,bkd->bhqk', qr_b, kr)
        s = s.astype(jnp.float32) * scale
        qpos = start + jnp.arange(Q_BLOCK)
        s = jnp.where(kpos[None, :] <= qpos[:, None], s, -jnp.inf)
        p = jax.nn.softmax(s, axis=-1).astype(v.dtype)
        return jnp.einsum('bhqk,bkhd->bqhd', p, v)

    starts = jnp.arange(nb, dtype=jnp.int32) * Q_BLOCK
    out = lax.map(one, (blocks(qn), blocks(qr), starts))
    return jnp.moveaxis(out, 0, 1).reshape(B, S, H, v.shape[-1])


def hybrid_mixer(h, w_in, ret_gn_g, mla_qn_g, mla_kvn_g, w_uq, w_ukv, w_out, rope_ret, rope_dil, rope_mla):
    B, S, _ = h.shape
    z = h @ w_in
    za, zb, zc = jnp.split(z, [RET_IN, RET_IN + DIL_IN], axis=-1)
    nq = RET_HEADS * RET_DK
    nv = RET_HEADS * RET_DV
    rq, rk, rv, rg = jnp.split(za, [nq, 2 * nq, 2 * nq + nv], axis=-1)
    rq = apply_rope(rq.reshape(B, S, RET_HEADS, RET_DK), *rope_ret)
    rk = apply_rope(rk.reshape(B, S, RET_HEADS, RET_DK), *rope_ret)
    ya = retention(rq, rk, rv.reshape(B, S, RET_HEADS, RET_DV), rg, ret_gn_g)
    dq, dk, dv = [t.reshape(B, S, DIL_HEADS, DIL_DH) for t in jnp.split(zb, 3, axis=-1)]
    dq = apply_rope(dq, *rope_dil)
    dk = apply_rope(dk, *rope_dil)
    yb = dilated_attention(dq, dk, dv).reshape(B, S, DIL_HEADS * DIL_DH)
    cq, ckv, kr = jnp.split(zc, [MLA_Q_RANK, MLA_Q_RANK + MLA_KV_RANK], axis=-1)
    q = (rms_norm(cq, mla_qn_g) @ w_uq).reshape(B, S, MLA_HEADS, MLA_NOPE + MLA_ROPE)
    qn, qr = jnp.split(q, [MLA_NOPE], axis=-1)
    qr = apply_rope(qr, *rope_mla)
    kv = (rms_norm(ckv, mla_kvn_g) @ w_ukv).reshape(B, S, MLA_HEADS, MLA_NOPE + MLA_V)
    kn, mv = jnp.split(kv, [MLA_NOPE], axis=-1)
    kr = apply_rope(kr[:, :, None, :], *rope_mla)[:, :, 0]
    yc = mla_attention(qn, qr, kn, kr, mv).reshape(B, S, MLA_HEADS * MLA_V)
    return jnp.concatenate([ya, yb, yc], axis=-1) @ w_out


def swiglu(h, w1, w3, w2):
    return (jax.nn.silu(h @ w1) * (h @ w3)) @ w2


def moe_swiglu(h, w_router, w1, w3, w2):
    B, S, D = h.shape
    t = h.reshape(B * S, D)
    logits = (t @ w_router).astype(jnp.float32)
    top_v, top_i = lax.top_k(logits, TOP_K)
    gates = jax.nn.softmax(top_v, axis=-1)
    combine = jnp.sum(jax.nn.one_hot(top_i, N_EXPERTS, dtype=jnp.float32) * gates[..., None], axis=1).astype(t.dtype)
    out = jnp.zeros_like(t)
    for e in range(N_EXPERTS):
        out = out + combine[:, e:e + 1] * swiglu(t, w1[e], w3[e], w2[e])
    return out.reshape(B, S, D)


def setup_inputs(seed: int = 0) -> dict:
    key = jax.random.key(seed)
    ks = jax.random.split(key, 23)
    f32 = jnp.float32

    def nrm(k, shape, scale):
        return jax.random.normal(k, shape, f32) * scale

    def gain(k, shape):
        return 1.0 + 0.02 * jax.random.normal(k, shape, f32)

    L = DEPTH
    D = D_MODEL
    return {
        'x': nrm(ks[0], (BATCH, SEQ, D), 1.0),
        'c': nrm(ks[1], (BATCH, D), 1.0),
        'positions': jnp.tile(jnp.arange(SEQ, dtype=jnp.int32)[None, :], (BATCH, 1)),
        'w_in': nrm(ks[2], (L, D, D_IN), D ** -0.5),
        'ret_gn_g': gain(ks[3], (L, RET_HEADS * RET_DV)),
        'mla_qn_g': gain(ks[4], (L, MLA_Q_RANK)),
        'mla_kvn_g': gain(ks[5], (L, MLA_KV_RANK)),
        'w_uq': nrm(ks[6], (L, MLA_Q_RANK, MLA_HEADS * (MLA_NOPE + MLA_ROPE)), MLA_Q_RANK ** -0.5),
        'w_ukv': nrm(ks[7], (L, MLA_KV_RANK, MLA_HEADS * (MLA_NOPE + MLA_V)), MLA_KV_RANK ** -0.5),
        'w_out': nrm(ks[8], (L, MIX_OUT, D), DEEPNORM_BETA * MIX_OUT ** -0.5),
        'w_ada': nrm(ks[9], (L, D, 6 * D), 0.1 * D ** -0.5),
        'b_ada': nrm(ks[10], (L, 6 * D), 0.01),
        'ln1_g': gain(ks[11], (L, D)),
        'ln1_b': nrm(ks[12], (L, D), 0.01),
        'ln2_g': gain(ks[13], (L, D)),
        'ln2_b': nrm(ks[14], (L, D), 0.01),
        'w1_dense': nrm(ks[15], (N_DENSE, D, D_FF), D ** -0.5),
        'w3_dense': nrm(ks[16], (N_DENSE, D, D_FF), D ** -0.5),
        'w2_dense': nrm(ks[17], (N_DENSE, D_FF, D), DEEPNORM_BETA * D_FF ** -0.5),
        'w_router': nrm(ks[18], (N_MOE, D, N_EXPERTS), D ** -0.5),
        'w1_moe': nrm(ks[19], (N_MOE, N_EXPERTS, D, D_FF_EXPERT), D ** -0.5),
        'w3_moe': nrm(ks[20], (N_MOE, N_EXPERTS, D, D_FF_EXPERT), D ** -0.5),
        'w2_moe': nrm(ks[21], (N_MOE, N_EXPERTS, D_FF_EXPERT, D), DEEPNORM_BETA * D_FF_EXPERT ** -0.5),
    }


def reference(x, c, positions, w_in, ret_gn_g, mla_qn_g, mla_kvn_g, w_uq, w_ukv, w_out,
              w_ada, b_ada, ln1_g, ln1_b, ln2_g, ln2_b, w1_dense, w3_dense, w2_dense,
              w_router, w1_moe, w3_moe, w2_moe):
    rope_ret = rope_tables(positions, RET_DK)
    rope_dil = rope_tables(positions, DIL_DH)
    rope_mla = rope_tables(positions, MLA_ROPE)
    cond = jax.nn.silu(c)
    for l in range(DEPTH):
        mod = (cond @ w_ada[l] + b_ada[l])[:, None, :]
        sh1, sc1, g1, sh2, sc2, g2 = jnp.split(mod, 6, axis=-1)
        h = x * (1.0 + sc1) + sh1
        y = hybrid_mixer(h, w_in[l], ret_gn_g[l], mla_qn_g[l], mla_kvn_g[l], w_uq[l], w_ukv[l], w_out[l],
                         rope_ret, rope_dil, rope_mla)
        x = layer_norm(DEEPNORM_ALPHA * x + (1.0 + g1) * y, ln1_g[l], ln1_b[l])
        h = x * (1.0 + sc2) + sh2
        if l % 2 == 0:
            y = swiglu(h, w1_dense[l // 2], w3_dense[l // 2], w2_dense[l // 2])
        else:
            y = moe_swiglu(h, w_router[l // 2], w1_moe[l // 2], w3_moe[l // 2], w2_moe[l // 2])
        x = layer_norm(DEEPNORM_ALPHA * x + (1.0 + g2) * y, ln2_g[l], ln2_b[l])
    return x
```

```python
import functools
import math

import jax
import jax.numpy as jnp
from jax import lax
from jax.experimental import pallas as pl
from jax.experimental.pallas import tpu as pltpu

D_MODEL = 1024
DEPTH = 4
RET_HEADS = 4
RET_D = 64
RET_CHUNK = 128
DIL_HEADS = 6
DIL_D = 64
DIL_DILATIONS = (1, 4, 16)
DIL_W = 128
MLA_HEADS = 6
MLA_Q_RANK = 384
MLA_KV_RANK = 256
MLA_NOPE = 64
MLA_ROPE = 32
MLA_V = 64
N_EXPERTS = 8
ROPE_THETA = 10000.0
LN_EPS = 1e-5
RMS_EPS = 1e-6
ALPHA = (2.0 * DEPTH) ** 0.25

LANES = 128
RET_W = RET_HEADS * RET_D
DIL_WD = DIL_HEADS * DIL_D
MLA_PAD = MLA_HEADS * LANES
MLA_VW = MLA_HEADS * MLA_V
RET_IN = 4 * RET_W
DIL_IN = 3 * DIL_WD
MLA_OFF = RET_IN + DIL_IN
D_IN_PAD = MLA_OFF + MLA_Q_RANK + MLA_KV_RANK + LANES
KR_LANE = MLA_NOPE

VMEM_LIMIT = 48 * 1024 * 1024
BF16 = jnp.bfloat16
F32 = jnp.float32
NEG_INF = float("-inf")


def _cparams(sem):
    return pltpu.CompilerParams(dimension_semantics=sem, vmem_limit_bytes=VMEM_LIMIT)


def _dot(a, b):
    return jnp.dot(a, b, preferred_element_type=F32)


def _dot_nt(a, b):
    return lax.dot_general(a, b, (((1,), (1,)), ((), ())), preferred_element_type=F32)


def _dot_tn(a, b):
    return lax.dot_general(a, b, (((0,), (0,)), ((), ())), preferred_element_type=F32)


def _silu(x):
    return x * (1.0 / (1.0 + jnp.exp(-x)))


def _layer_norm(v, g, b):
    mu = jnp.mean(v, axis=-1, keepdims=True)
    d = v - mu
    var = jnp.mean(d * d, axis=-1, keepdims=True)
    return d * lax.rsqrt(var + LN_EPS) * g + b


def _lane_ids(shape):
    return lax.broadcasted_iota(jnp.int32, shape, len(shape) - 1)


def _rope_group(x, cos, sin_signed, first_half, half):
    fwd = pltpu.roll(x, LANES - half, 1)
    bwd = pltpu.roll(x, half, 1)
    return x * cos + jnp.where(first_half, fwd, bwd) * sin_signed


def _ada_kernel(c_ref, w_ref, b_ref, o_ref):
    cond = _silu(c_ref[...])
    o_ref[0] = jnp.dot(cond, w_ref[0], preferred_element_type=F32,
                       precision=lax.Precision.HIGHEST) + b_ref[0]


def _ada_call(c, w_ada, b_ada):
    nl, d, n = w_ada.shape
    b = c.shape[0]
    tn = 1536
    return pl.pallas_call(
        _ada_kernel,
        out_shape=jax.ShapeDtypeStruct((nl, b, n), F32),
        grid=(nl, n // tn),
        in_specs=[pl.BlockSpec((b, d), lambda l, j: (0, 0)),
                  pl.BlockSpec((1, d, tn), lambda l, j: (l, 0, j)),
                  pl.BlockSpec((1, 1, tn), lambda l, j: (l, 0, j))],
        out_specs=pl.BlockSpec((1, b, tn), lambda l, j: (l, 0, j)),
        compiler_params=_cparams(("arbitrary", "arbitrary")),
        name="ada_mod",
    )(c, w_ada, b_ada.reshape(nl, 1, n))


def _inproj_kernel(x_ref, sc_ref, sh_ref, w_ref, c64_ref, s64_ref, cm_ref, sm_ref,
                   qg_ref, kvg_ref, wuq_ref, wkn_ref, wv_ref,
                   rq_ref, rk_ref, rv_ref, rg_ref, dq_ref, dk_ref, dv_ref,
                   mq_ref, mk_ref, mv_ref):
    h = (x_ref[0] * (1.0 + sc_ref[0]) + sh_ref[0]).astype(BF16)
    c64 = c64_ref[0]
    s64 = s64_ref[0]
    cm = cm_ref[0]
    sm = sm_ref[0]
    lane = _lane_ids((1, LANES))
    first64 = (lane % 64) < 32
    firstm = lane < (KR_LANE + MLA_ROPE // 2)

    def rope64(zc):
        return _rope_group(zc, c64, s64, first64, 32)

    def ropem(zc):
        return _rope_group(zc, cm, sm, firstm, MLA_ROPE // 2)

    za = _dot(h, w_ref[:, 0:RET_IN])
    for j in range(RET_W // LANES):
        sl = slice(j * LANES, (j + 1) * LANES)
        rq_ref[0, :, sl] = rope64(za[:, j * LANES:(j + 1) * LANES]).astype(BF16)
        kc = za[:, RET_W + j * LANES:RET_W + (j + 1) * LANES]
        rk_ref[0, :, sl] = (rope64(kc) * (RET_D ** -0.5)).astype(BF16)
    rv_ref[0] = za[:, 2 * RET_W:3 * RET_W].astype(BF16)
    rg_ref[0] = za[:, 3 * RET_W:4 * RET_W]

    zb = _dot(h, w_ref[:, RET_IN:MLA_OFF])
    for j in range(DIL_WD // LANES):
        sl = slice(j * LANES, (j + 1) * LANES)
        dq_ref[0, :, sl] = (rope64(zb[:, j * LANES:(j + 1) * LANES]) * (DIL_D ** -0.5)).astype(BF16)
        dk_ref[0, :, sl] = rope64(zb[:, DIL_WD + j * LANES:DIL_WD + (j + 1) * LANES]).astype(BF16)
    dv_ref[0] = zb[:, 2 * DIL_WD:3 * DIL_WD].astype(BF16)

    zc = _dot(h, w_ref[:, MLA_OFF:D_IN_PAD])
    cq = zc[:, 0:MLA_Q_RANK]
    ckv = zc[:, MLA_Q_RANK:MLA_Q_RANK + MLA_KV_RANK]
    kr = ropem(zc[:, MLA_Q_RANK + MLA_KV_RANK:])
    cqn = (cq * lax.rsqrt(jnp.mean(cq * cq, axis=-1, keepdims=True) + RMS_EPS) * qg_ref[...]).astype(BF16)
    ckvn = (ckv * lax.rsqrt(jnp.mean(ckv * ckv, axis=-1, keepdims=True) + RMS_EPS) * kvg_ref[...]).astype(BF16)
    q = _dot(cqn, wuq_ref[...])
    kn = _dot(ckvn, wkn_ref[...])
    for hh in range(MLA_HEADS):
        sl = slice(hh * LANES, (hh + 1) * LANES)
        mq_ref[0, :, sl] = ropem(q[:, hh * LANES:(hh + 1) * LANES]).astype(BF16)
        mk_ref[0, :, sl] = (kn[:, hh * LANES:(hh + 1) * LANES] + kr).astype(BF16)
    mv_ref[0] = _dot(ckvn, wv_ref[...]).astype(BF16)


def _inproj_call(x, sc, sh, w_in_p, tabs, qg, kvg, wuq_p, wkn_p, wv, tm=512):
    b, s, d = x.shape
    c64, s64, cm, sm = tabs
    tok = lambda w: pl.BlockSpec((1, tm, w), lambda i, j: (i, j, 0))
    per_b = pl.BlockSpec((1, 1, d), lambda i, j: (i, 0, 0))
    full = lambda a: pl.BlockSpec(a.shape, lambda i, j: (0,) * a.ndim)
    outs = [(RET_W, BF16), (RET_W, BF16), (RET_W, BF16), (RET_W, F32),
            (DIL_WD, BF16), (DIL_WD, BF16), (DIL_WD, BF16),
            (MLA_PAD, BF16), (MLA_PAD, BF16), (MLA_VW, BF16)]
    return pl.pallas_call(
        _inproj_kernel,
        out_shape=[jax.ShapeDtypeStruct((b, s, w), dt) for w, dt in outs],
        grid=(b, s // tm),
        in_specs=[tok(d), per_b, per_b, full(w_in_p), tok(LANES), tok(LANES), tok(LANES), tok(LANES),
                  full(qg), full(kvg), full(wuq_p), full(wkn_p), full(wv)],
        out_specs=[tok(w) for w, _ in outs],
        compiler_params=_cparams(("parallel", "parallel")),
        name="in_proj",
    )(x, sc, sh, w_in_p, c64, s64, cm, sm, qg, kvg, wuq_p, wkn_p, wv)


def _ret_kernel(q_ref, k_ref, v_ref, g_ref, gn_ref, intra_ref, xi_ref, zeta_ref, dmat_ref, bmask_ref,
                o_ref, state_ref, *, chunks):
    @pl.when(pl.program_id(1) == 0)
    def _():
        state_ref[...] = jnp.zeros_like(state_ref)

    lane = _lane_ids((1, LANES))
    head_a = lane < RET_D
    c = RET_CHUNK
    for ci in range(chunks):
        rows = slice(ci * c, (ci + 1) * c)
        for p in range(RET_W // LANES):
            cols = slice(p * LANES, (p + 1) * LANES)
            q = q_ref[0, rows, cols]
            k = k_ref[0, rows, cols]
            v = v_ref[0, rows, cols]
            zero = jnp.zeros_like(q)
            s_a = _dot_nt(jnp.where(head_a, q, zero), k) * intra_ref[2 * p]
            s_b = _dot_nt(jnp.where(head_a, zero, q), k) * intra_ref[2 * p + 1]
            s_cat = jnp.concatenate([s_a, s_b], axis=1).astype(BF16)
            v_stack = jnp.concatenate([jnp.where(head_a, v, zero), jnp.where(head_a, zero, v)], axis=0)
            inner = _dot(s_cat, v_stack)
            state = state_ref[p]
            cross = _dot(q, state.astype(BF16)) * xi_ref[:, cols]
            kz = (k.astype(F32) * zeta_ref[:, cols]).astype(BF16)
            state_ref[p] = state * dmat_ref[p] + _dot_tn(kz, v) * bmask_ref[...]
            o = inner + cross
            inv_n = 1.0 / RET_D
            mu = jnp.where(head_a,
                           jnp.sum(jnp.where(head_a, o, 0.0), axis=-1, keepdims=True),
                           jnp.sum(jnp.where(head_a, 0.0, o), axis=-1, keepdims=True)) * inv_n
            d = o - mu
            dd = d * d
            var = jnp.where(head_a,
                            jnp.sum(jnp.where(head_a, dd, 0.0), axis=-1, keepdims=True),
                            jnp.sum(jnp.where(head_a, 0.0, dd), axis=-1, keepdims=True)) * inv_n
            on = d * lax.rsqrt(var + LN_EPS) * gn_ref[:, cols]
            o_ref[0, rows, cols] = (_silu(g_ref[0, rows, cols]) * on).astype(BF16)


def _ret_tables():
    c = RET_CHUNK
    log_g = jnp.log(1.0 - 2.0 ** (-5.0 - jnp.arange(RET_HEADS, dtype=F32)))
    idx = jnp.arange(c, dtype=F32)
    diff = idx[:, None] - idx[None, :]
    intra = jnp.where(diff >= 0, jnp.exp(jnp.maximum(diff, 0.0) * log_g[:, None, None]), 0.0)
    xi = jnp.exp((idx[:, None] + 1.0) * log_g[None, :])
    zeta = jnp.exp((c - 1.0 - idx[:, None]) * log_g[None, :])
    decay = jnp.exp(c * log_g)
    xi_l = jnp.repeat(xi, RET_D, axis=1)
    zeta_l = jnp.repeat(zeta, RET_D, axis=1)
    r = jnp.arange(LANES)
    same = (r[:, None] // RET_D) == (r[None, :] // RET_D)
    bmask = same.astype(F32)
    dec_l = jnp.repeat(decay, RET_D).reshape(RET_W // LANES, 1, LANES)
    dmat = bmask[None] * dec_l
    return intra, xi_l, zeta_l, dmat, bmask


def _ret_call(rq, rk, rv, rg, gn, tables, chunks=4):
    b, s, w = rq.shape
    tm = chunks * RET_CHUNK
    intra, xi_l, zeta_l, dmat, bmask = tables
    tok = pl.BlockSpec((1, tm, w), lambda i, j: (i, j, 0))
    full = lambda a: pl.BlockSpec(a.shape, lambda i, j: (0,) * a.ndim)
    return pl.pallas_call(
        functools.partial(_ret_kernel, chunks=chunks),
        out_shape=jax.ShapeDtypeStruct((b, s, w), BF16),
        grid=(b, s // tm),
        in_specs=[tok, tok, tok, tok, full(gn), full(intra), full(xi_l), full(zeta_l), full(dmat), full(bmask)],
        out_specs=tok,
        scratch_shapes=[pltpu.VMEM((w // LANES, LANES, LANES), F32)],
        compiler_params=_cparams(("parallel", "arbitrary")),
        name="retention",
    )(rq, rk, rv, rg, gn, intra, xi_l, zeta_l, dmat, bmask)


STAT_L_OFF = 8


def _dil_pattern(q_ref, kp_ref, kc_ref, vp_ref, vc_ref, first_block):
    w = DIL_W
    lane = _lane_ids((1, LANES))
    head_a = lane < DIL_D
    qi = lax.broadcasted_iota(jnp.int32, (w, 2 * w), 0)
    kj = lax.broadcasted_iota(jnp.int32, (w, 2 * w), 1)
    dist = w + qi - kj
    valid = (dist >= 0) & (dist <= w) & (jnp.logical_not(first_block) | (kj >= w))
    res = []
    for p in range(DIL_WD // LANES):
        cols = slice(p * LANES, (p + 1) * LANES)
        q = q_ref[0, :, cols]
        kcat = jnp.concatenate([kp_ref[0, :, cols], kc_ref[0, :, cols]], axis=0)
        vcat = jnp.concatenate([vp_ref[0, :, cols], vc_ref[0, :, cols]], axis=0)
        zero = jnp.zeros_like(q)
        zv = jnp.zeros_like(vcat)
        ms, ls, ps = [], [], []
        for qh in (jnp.where(head_a, q, zero), jnp.where(head_a, zero, q)):
            sc = jnp.where(valid, _dot_nt(qh, kcat), NEG_INF)
            m = jnp.max(sc, axis=-1, keepdims=True)
            pe = jnp.exp(sc - m)
            ms.append(m)
            ls.append(jnp.sum(pe, axis=-1, keepdims=True))
            ps.append(pe.astype(BF16))
        v_stack = jnp.concatenate([jnp.where(head_a, vcat, zv), jnp.where(head_a, zv, vcat)], axis=0)
        acc = _dot(jnp.concatenate(ps, axis=1), v_stack)
        res.append((acc, ms, ls))
    return res


def _dil_partial_kernel(q_ref, kp_ref, kc_ref, vp_ref, vc_ref, acc_ref, st_ref):
    first_block = pl.program_id(2) == 0
    res = _dil_pattern(q_ref, kp_ref, kc_ref, vp_ref, vc_ref, first_block)
    lane = _lane_ids((1, LANES))
    st = jnp.zeros((DIL_W, LANES), F32)
    for p, (acc, ms, ls) in enumerate(res):
        acc_ref[0, :, p * LANES:(p + 1) * LANES] = acc
        for j in range(2):
            hh = 2 * p + j
            st = jnp.where(lane == hh, ms[j], st)
            st = jnp.where(lane == STAT_L_OFF + hh, ls[j], st)
    st_ref[0] = st


def _dil_final_kernel(q_ref, kp_ref, kc_ref, vp_ref, vc_ref, acc1_ref, st1_ref, acc2_ref, st2_ref, o_ref):
    first_block = pl.program_id(2) == 0
    res = _dil_pattern(q_ref, kp_ref, kc_ref, vp_ref, vc_ref, first_block)
    lane = _lane_ids((1, LANES))
    head_a = lane < DIL_D
    st1 = st1_ref[0]
    st2 = st2_ref[0]

    def col(st, idx):
        return jnp.sum(jnp.where(lane == idx, st, 0.0), axis=-1, keepdims=True)

    for p, (acc3, ms, ls) in enumerate(res):
        cols = slice(p * LANES, (p + 1) * LANES)
        scale = []
        den = []
        for j in range(2):
            hh = 2 * p + j
            m1, l1 = col(st1, hh), col(st1, STAT_L_OFF + hh)
            m2, l2 = col(st2, hh), col(st2, STAT_L_OFF + hh)
            m3, l3 = ms[j], ls[j]
            mm = jnp.maximum(jnp.maximum(m1, m2), m3)
            a1, a2, a3 = jnp.exp(m1 - mm), jnp.exp(m2 - mm), jnp.exp(m3 - mm)
            scale.append((a1, a2, a3))
            den.append(a1 * l1 + a2 * l2 + a3 * l3)
        pick = lambda i: jnp.where(head_a, scale[0][i], scale[1][i])
        num = acc1_ref[0, :, cols] * pick(0) + acc2_ref[0, :, cols] * pick(1) + acc3 * pick(2)
        o_ref[0, :, cols] = (num / jnp.where(head_a, den[0], den[1])).astype(BF16)


def _dil_specs(r, width):
    cur = pl.BlockSpec((1, DIL_W, width), lambda b, rho, n: (b, n, rho))
    prev = pl.BlockSpec((1, DIL_W, width), lambda b, rho, n: (b, jnp.maximum(n - 1, 0), rho))
    return cur, prev


def _dil_call(dq, dk, dv):
    b, s, w = dq.shape
    partial = []
    for r in DIL_DILATIONS[:-1]:
        l = s // r
        view = lambda a: a.reshape(b, l, r * a.shape[-1])
        cur, prev = _dil_specs(r, w)
        st_spec, _ = _dil_specs(r, LANES)
        acc, st = pl.pallas_call(
            _dil_partial_kernel,
            out_shape=[jax.ShapeDtypeStruct((b, l, r * w), F32),
                       jax.ShapeDtypeStruct((b, l, r * LANES), F32)],
            grid=(b, r, l // DIL_W),
            in_specs=[cur, prev, cur, prev, cur],
            out_specs=[cur, st_spec],
            compiler_params=_cparams(("parallel", "parallel", "arbitrary")),
            name=f"dilated_r{r}",
        )(view(dq), view(dk), view(dk), view(dv), view(dv))
        partial += [acc.reshape(b, s, w), st.reshape(b, s, LANES)]
    r = DIL_DILATIONS[-1]
    l = s // r
    view = lambda a: a.reshape(b, l, r * a.shape[-1])
    cur, prev = _dil_specs(r, w)
    st_spec, _ = _dil_specs(r, LANES)
    out = pl.pallas_call(
        _dil_final_kernel,
        out_shape=jax.ShapeDtypeStruct((b, l, r * w), BF16),
        grid=(b, r, l // DIL_W),
        in_specs=[cur, prev, cur, prev, cur, cur, st_spec, cur, st_spec],
        out_specs=cur,
        compiler_params=_cparams(("parallel", "parallel", "arbitrary")),
        name=f"dilated_r{r}_merge",
    )(view(dq), view(dk), view(dk), view(dv), view(dv), *[view(a) for a in partial])
    return out.reshape(b, s, w)


def _mla_kernel(q_ref, k_ref, v_ref, o_ref, m_ref, l_ref, acc_ref, *, tq, tk, scale):
    qi = pl.program_id(2)
    ki = pl.program_id(3)
    lane = _lane_ids((1, LANES))
    head_a = lane < MLA_V

    @pl.when(ki == 0)
    def _():
        m_ref[...] = jnp.full_like(m_ref, NEG_INF)
        l_ref[...] = jnp.zeros_like(l_ref)
        acc_ref[...] = jnp.zeros_like(acc_ref)

    def step(masked):
        v = v_ref[0]
        zv = jnp.zeros_like(v)
        v_stack = jnp.concatenate([jnp.where(head_a, v, zv), jnp.where(head_a, zv, v)], axis=0)
        ps, alphas = [], []
        for j in range(2):
            q = q_ref[0, :, j * LANES:(j + 1) * LANES]
            k = k_ref[0, :, j * LANES:(j + 1) * LANES]
            s = _dot_nt(q, k) * scale
            if masked:
                row = lax.broadcasted_iota(jnp.int32, (tq, tk), 0)
                colk = lax.broadcasted_iota(jnp.int32, (tq, tk), 1)
                s = jnp.where(colk <= row, s, NEG_INF)
            m_old = m_ref[j]
            m_new = jnp.maximum(m_old, jnp.max(s, axis=-1, keepdims=True))
            alpha = jnp.exp(m_old - m_new)
            pe = jnp.exp(s - m_new)
            l_ref[j] = alpha * l_ref[j] + jnp.sum(pe, axis=-1, keepdims=True)
            m_ref[j] = m_new
            ps.append(pe.astype(BF16))
            alphas.append(alpha)
        acc_ref[...] = acc_ref[...] * jnp.where(head_a, alphas[0], alphas[1]) + _dot(
            jnp.concatenate(ps, axis=1), v_stack)

    @pl.when(ki < qi)
    def _():
        step(False)

    @pl.when(ki == qi)
    def _():
        step(True)
        o_ref[0] = (acc_ref[...] / jnp.where(head_a, l_ref[0], l_ref[1])).astype(BF16)


def _mla_call(mq, mk, mv, t=512):
    b, s, _ = mq.shape
    n = s // t
    scale = (MLA_NOPE + MLA_ROPE) ** -0.5
    return pl.pallas_call(
        functools.partial(_mla_kernel, tq=t, tk=t, scale=scale),
        out_shape=jax.ShapeDtypeStruct((b, s, MLA_VW), BF16),
        grid=(b, MLA_HEADS // 2, n, n),
        in_specs=[pl.BlockSpec((1, t, 2 * LANES), lambda bb, p, i, j: (bb, i, p)),
                  pl.BlockSpec((1, t, 2 * LANES), lambda bb, p, i, j: (bb, jnp.minimum(j, i), p)),
                  pl.BlockSpec((1, t, LANES), lambda bb, p, i, j: (bb, jnp.minimum(j, i), p))],
        out_specs=pl.BlockSpec((1, t, LANES), lambda bb, p, i, j: (bb, i, p)),
        scratch_shapes=[pltpu.VMEM((2, t, 1), F32), pltpu.VMEM((2, t, 1), F32), pltpu.VMEM((t, LANES), F32)],
        compiler_params=_cparams(("parallel", "parallel", "parallel", "arbitrary")),
        name="mla_attention",
    )(mq, mk, mv)


def _outproj_kernel(ya_ref, yb_ref, yc_ref, wa_ref, wb_ref, wc_ref, x_ref, g_ref, lg_ref, lb_ref, o_ref):
    y = _dot(ya_ref[0], wa_ref[...]) + _dot(yb_ref[0], wb_ref[...]) + _dot(yc_ref[0], wc_ref[...])
    v = ALPHA * x_ref[0] + (1.0 + g_ref[0]) * y
    o_ref[0] = _layer_norm(v, lg_ref[...], lb_ref[...])


def _outproj_call(ya, yb, yc, wa, wb, wc, x, g1, lg, lb, tm=512):
    b, s, d = x.shape
    tok = lambda w: pl.BlockSpec((1, tm, w), lambda i, j: (i, j, 0))
    per_b = pl.BlockSpec((1, 1, d), lambda i, j: (i, 0, 0))
    full = lambda a: pl.BlockSpec(a.shape, lambda i, j: (0,) * a.ndim)
    return pl.pallas_call(
        _outproj_kernel,
        out_shape=jax.ShapeDtypeStruct((b, s, d), F32),
        grid=(b, s // tm),
        in_specs=[tok(ya.shape[-1]), tok(yb.shape[-1]), tok(yc.shape[-1]), full(wa), full(wb), full(wc),
                  tok(d), per_b, full(lg), full(lb)],
        out_specs=tok(d),
        compiler_params=_cparams(("parallel", "parallel")),
        name="out_proj_ln",
    )(ya, yb, yc, wa, wb, wc, x, g1, lg, lb)


def _ffn_kernel(x_ref, sc_ref, sh_ref, g_ref, w1_ref, w3_ref, w2_ref, lg_ref, lb_ref, o_ref, h_ref, acc_ref):
    f = pl.program_id(1)

    @pl.when(f == 0)
    def _():
        h_ref[...] = (x_ref[...] * (1.0 + sc_ref[0]) + sh_ref[0]).astype(BF16)
        acc_ref[...] = jnp.zeros_like(acc_ref)

    h = h_ref[...]
    mid = (_silu(_dot(h, w1_ref[...])) * _dot(h, w3_ref[...])).astype(BF16)
    acc_ref[...] += _dot(mid, w2_ref[...])

    @pl.when(f == pl.num_programs(1) - 1)
    def _():
        v = ALPHA * x_ref[...] + (1.0 + g_ref[0]) * acc_ref[...]
        o_ref[...] = _layer_norm(v, lg_ref[...], lb_ref[...])


def _ffn_call(x2d, sc, sh, g2, w1, w3, w2, lg, lb, tiles_per_batch, tm, tf):
    t, d = x2d.shape
    ff = w1.shape[1]
    per_b = pl.BlockSpec((1, 1, d), lambda i, f: (i // tiles_per_batch, 0, 0))
    vec = pl.BlockSpec((1, d), lambda i, f: (0, 0))
    return pl.pallas_call(
        _ffn_kernel,
        out_shape=jax.ShapeDtypeStruct((t, d), F32),
        grid=(t // tm, ff // tf),
        in_specs=[pl.BlockSpec((tm, d), lambda i, f: (i, 0)), per_b, per_b, per_b,
                  pl.BlockSpec((d, tf), lambda i, f: (0, f)),
                  pl.BlockSpec((d, tf), lambda i, f: (0, f)),
                  pl.BlockSpec((tf, d), lambda i, f: (f, 0)), vec, vec],
        out_specs=pl.BlockSpec((tm, d), lambda i, f: (i, 0)),
        scratch_shapes=[pltpu.VMEM((tm, d), BF16), pltpu.VMEM((tm, d), F32)],
        compiler_params=_cparams(("parallel", "arbitrary")),
        name="ffn_dense",
    )(x2d, sc, sh, g2, w1, w3, w2, lg, lb)


ROUTE_G_OFF = 2


def _router_kernel(x_ref, sc_ref, sh_ref, wr_ref, h_ref, r_ref):
    h = x_ref[...] * (1.0 + sc_ref[0]) + sh_ref[0]
    h_ref[...] = h.astype(BF16)
    logits = jnp.dot(h, wr_ref[...], preferred_element_type=F32, precision=lax.Precision.HIGHEST)
    lane = _lane_ids(logits.shape)
    lg = jnp.where(lane < N_EXPERTS, logits, NEG_INF)
    m1 = jnp.max(lg, axis=-1, keepdims=True)
    i1 = jnp.min(jnp.where(lg == m1, lane, LANES), axis=-1, keepdims=True)
    lg2 = jnp.where(lane == i1, NEG_INF, lg)
    m2 = jnp.max(lg2, axis=-1, keepdims=True)
    i2 = jnp.min(jnp.where(lg2 == m2, lane, LANES), axis=-1, keepdims=True)
    e2 = jnp.exp(m2 - m1)
    den = 1.0 + e2
    out = jnp.where(lane == 0, i1.astype(F32), 0.0)
    out = jnp.where(lane == 1, i2.astype(F32), out)
    out = jnp.where(lane == ROUTE_G_OFF, 1.0 / den, out)
    out = jnp.where(lane == ROUTE_G_OFF + 1, e2 / den, out)
    r_ref[...] = out


def _router_call(x2d, sc, sh, wr_p, tiles_per_batch, tm):
    t, d = x2d.shape
    per_b = pl.BlockSpec((1, 1, d), lambda i: (i // tiles_per_batch, 0, 0))
    return pl.pallas_call(
        _router_kernel,
        out_shape=[jax.ShapeDtypeStruct((t, d), BF16), jax.ShapeDtypeStruct((t, LANES), F32)],
        grid=(t // tm,),
        in_specs=[pl.BlockSpec((tm, d), lambda i: (i, 0)), per_b, per_b,
                  pl.BlockSpec(wr_p.shape, lambda i: (0, 0))],
        out_specs=[pl.BlockSpec((tm, d), lambda i: (i, 0)), pl.BlockSpec((tm, LANES), lambda i: (i, 0))],
        compiler_params=_cparams(("parallel",)),
        name="moe_router",
    )(x2d, sc, sh, wr_p)


def _expert_kernel(te_ref, nu_ref, x_ref, w1_ref, w3_ref, w2_ref, o_ref, acc_ref):
    i = pl.program_id(0)
    f = pl.program_id(1)
    used = i < nu_ref[0]

    @pl.when(f == 0)
    def _():
        acc_ref[...] = jnp.zeros_like(acc_ref)

    @pl.when(used)
    def _():
        h = x_ref[...]
        mid = (_silu(_dot(h, w1_ref[0])) * _dot(h, w3_ref[0])).astype(BF16)
        acc_ref[...] += _dot(mid, w2_ref[0])

    @pl.when(f == pl.num_programs(1) - 1)
    def _():
        o_ref[...] = acc_ref[...]


def _expert_call(tile_expert, n_used, xs, w1, w3, w2, tm, tf):
    p, d = xs.shape
    ff = w1.shape[2]
    nf = ff // tf

    def fidx(i, f, nu):
        return jnp.where(i < nu[0], f, nf - 1)

    return pl.pallas_call(
        _expert_kernel,
        out_shape=jax.ShapeDtypeStruct((p, d), F32),
        grid_spec=pltpu.PrefetchScalarGridSpec(
            num_scalar_prefetch=2,
            grid=(p // tm, nf),
            in_specs=[pl.BlockSpec((tm, d), lambda i, f, te, nu: (i, 0)),
                      pl.BlockSpec((1, d, tf), lambda i, f, te, nu: (te[i], 0, fidx(i, f, nu))),
                      pl.BlockSpec((1, d, tf), lambda i, f, te, nu: (te[i], 0, fidx(i, f, nu))),
                      pl.BlockSpec((1, tf, d), lambda i, f, te, nu: (te[i], fidx(i, f, nu), 0))],
            out_specs=pl.BlockSpec((tm, d), lambda i, f, te, nu: (i, 0)),
            scratch_shapes=[pltpu.VMEM((tm, d), F32)]),
        compiler_params=_cparams(("parallel", "arbitrary")),
        name="moe_experts",
    )(tile_expert, n_used, xs, w1, w3, w2)


def _combine_kernel(x_ref, ya_ref, yb_ref, r_ref, g_ref, lg_ref, lb_ref, o_ref):
    r = r_ref[...]
    lane = _lane_ids(r.shape)
    ga = jnp.sum(jnp.where(lane == ROUTE_G_OFF, r, 0.0), axis=-1, keepdims=True)
    gb = jnp.sum(jnp.where(lane == ROUTE_G_OFF + 1, r, 0.0), axis=-1, keepdims=True)
    y = ga * ya_ref[...] + gb * yb_ref[...]
    v = ALPHA * x_ref[...] + (1.0 + g_ref[0]) * y
    o_ref[...] = _layer_norm(v, lg_ref[...], lb_ref[...])


def _combine_call(x2d, ya, yb, route, g2, lg, lb, tiles_per_batch, tm):
    t, d = x2d.shape
    tok = pl.BlockSpec((tm, d), lambda i: (i, 0))
    per_b = pl.BlockSpec((1, 1, d), lambda i: (i // tiles_per_batch, 0, 0))
    vec = pl.BlockSpec((1, d), lambda i: (0, 0))
    return pl.pallas_call(
        _combine_kernel,
        out_shape=jax.ShapeDtypeStruct((t, d), F32),
        grid=(t // tm,),
        in_specs=[tok, tok, tok, pl.BlockSpec((tm, LANES), lambda i: (i, 0)), per_b, vec, vec],
        out_specs=tok,
        compiler_params=_cparams(("parallel",)),
        name="moe_combine_ln",
    )(x2d, ya, yb, route, g2, lg, lb)


def _dispatch_plan(route, tm):
    t = route.shape[0]
    e = route[:, 0:2].astype(jnp.int32).reshape(-1)
    onehot = (e[:, None] == jnp.arange(N_EXPERTS, dtype=jnp.int32)[None, :]).astype(jnp.int32)
    csum = jnp.cumsum(onehot, axis=0)
    counts = csum[-1]
    rank = jnp.take_along_axis(csum, e[:, None], axis=1)[:, 0] - 1
    padded = ((counts + tm - 1) // tm) * tm
    ends = jnp.cumsum(padded)
    starts = ends - padded
    dest = starts[e] + rank
    n_rows = 2 * t + N_EXPERTS * tm
    row_token = jnp.zeros((n_rows,), jnp.int32).at[dest].set(jnp.arange(2 * t, dtype=jnp.int32) // 2)
    tile_start = jnp.arange(n_rows // tm, dtype=jnp.int32) * tm
    tile_expert = jnp.minimum(jnp.searchsorted(ends, tile_start, side="right"), N_EXPERTS - 1).astype(jnp.int32)
    n_used = (ends[-1] // tm).astype(jnp.int32).reshape(1)
    return row_token, dest, tile_expert, n_used


def _moe_layer(x2d, sc, sh, g2, wr_p, w1, w3, w2, lg, lb, tiles_per_batch, tm, tme, tf):
    h, route = _router_call(x2d, sc, sh, wr_p, tiles_per_batch, tm)
    row_token, dest, tile_expert, n_used = _dispatch_plan(route, tme)
    xs = jnp.take(h, row_token, axis=0)
    ys = _expert_call(tile_expert, n_used, xs, w1, w3, w2, tme, tf)
    ysel = jnp.take(ys, dest, axis=0).reshape(x2d.shape[0], 2, x2d.shape[1])
    return _combine_call(x2d, ysel[:, 0], ysel[:, 1], route, g2, lg, lb, tiles_per_batch, tm)


def _rope_tables(positions):
    pos = positions.astype(F32)[..., None]
    lane = jnp.arange(LANES)
    inv64 = ROPE_THETA ** (-jnp.arange(0, RET_D, 2, dtype=F32) / RET_D)
    ang64 = pos * inv64[lane % 32]
    sign64 = jnp.where((lane % 64) < 32, -1.0, 1.0)
    c64, s64 = jnp.cos(ang64), jnp.sin(ang64) * sign64
    invm = ROPE_THETA ** (-jnp.arange(0, MLA_ROPE, 2, dtype=F32) / MLA_ROPE)
    in_rope = (lane >= KR_LANE) & (lane < KR_LANE + MLA_ROPE)
    angm = pos * invm[(lane - KR_LANE) % (MLA_ROPE // 2)]
    signm = jnp.where(lane < KR_LANE + MLA_ROPE // 2, -1.0, 1.0)
    cm = jnp.where(in_rope, jnp.cos(angm), 1.0)
    sm = jnp.where(in_rope, jnp.sin(angm) * signm, 0.0)
    return c64, s64, cm, sm


def _prep_mixer_weights(w_in, w_uq, w_ukv):
    d = w_in.shape[0]
    kr_cols = jnp.zeros((d, LANES), w_in.dtype).at[:, KR_LANE:KR_LANE + MLA_ROPE].set(
        w_in[:, MLA_OFF + MLA_Q_RANK + MLA_KV_RANK:])
    w_in_p = jnp.concatenate([w_in[:, :MLA_OFF + MLA_Q_RANK + MLA_KV_RANK], kr_cols], axis=1).astype(BF16)
    uq = w_uq.reshape(MLA_Q_RANK, MLA_HEADS, MLA_NOPE + MLA_ROPE)
    wuq_p = jnp.pad(uq, ((0, 0), (0, 0), (0, LANES - MLA_NOPE - MLA_ROPE))).reshape(MLA_Q_RANK, MLA_PAD).astype(BF16)
    ukv = w_ukv.reshape(MLA_KV_RANK, MLA_HEADS, MLA_NOPE + MLA_V)
    wkn_p = jnp.pad(ukv[:, :, :MLA_NOPE], ((0, 0), (0, 0), (0, LANES - MLA_NOPE))).reshape(MLA_KV_RANK, MLA_PAD).astype(BF16)
    wv = ukv[:, :, MLA_NOPE:].reshape(MLA_KV_RANK, MLA_VW).astype(BF16)
    return w_in_p, wuq_p, wkn_p, wv


def kernel(x, c, positions, w_in, ret_gn_g, mla_qn_g, mla_kvn_g, w_uq, w_ukv, w_out, w_ada, b_ada, ln1_g, ln1_b, ln2_g, ln2_b, w1_dense, w3_dense, w2_dense, w_router, w1_moe, w3_moe, w2_moe):
    b, s, d = x.shape
    tabs = _rope_tables(positions)
    ret_tabs = _ret_tables()
    mod = _ada_call(c, w_ada, b_ada)
    tm_tok = 512
    tiles_per_batch = s // tm_tok
    for l in range(w_in.shape[0]):
        sh1, sc1, g1, sh2, sc2, g2 = [mod[l, :, j * d:(j + 1) * d].reshape(b, 1, d) for j in range(6)]
        w_in_p, wuq_p, wkn_p, wv = _prep_mixer_weights(w_in[l], w_uq[l], w_ukv[l])
        rq, rk, rv, rg, dq, dk, dv, mq, mk, mv = _inproj_call(
            x, sc1, sh1, w_in_p, tabs, mla_qn_g[l].reshape(1, -1), mla_kvn_g[l].reshape(1, -1), wuq_p, wkn_p, wv)
        ya = _ret_call(rq, rk, rv, rg, ret_gn_g[l].reshape(1, -1), ret_tabs)
        yb = _dil_call(dq, dk, dv)
        yc = _mla_call(mq, mk, mv)
        wo = w_out[l].astype(BF16)
        x = _outproj_call(ya, yb, yc, wo[:RET_W], wo[RET_W:RET_W + DIL_WD], wo[RET_W + DIL_WD:], x, g1,
                          ln1_g[l].reshape(1, d), ln1_b[l].reshape(1, d))
        x2d = x.reshape(b * s, d)
        lg, lb = ln2_g[l].reshape(1, d), ln2_b[l].reshape(1, d)
        if l % 2 == 0:
            j = l // 2
            x2d = _ffn_call(x2d, sc2, sh2, g2, w1_dense[j].astype(BF16), w3_dense[j].astype(BF16),
                            w2_dense[j].astype(BF16), lg, lb, tiles_per_batch, tm_tok, 256)
        else:
            j = l // 2
            wr_p = jnp.pad(w_router[j], ((0, 0), (0, LANES - N_EXPERTS)))
            x2d = _moe_layer(x2d, sc2, sh2, g2, wr_p, w1_moe[j].astype(BF16), w3_moe[j].astype(BF16),
                             w2_moe[j].astype(BF16), lg, lb, tiles_per_batch, tm_tok, 512, 512)
        x = x2d.reshape(b, s, d)
    return x
```

```python
import functools
import math

import jax
import jax.numpy as jnp
from jax import lax
from jax.experimental import pallas as pl
from jax.experimental.pallas import tpu as pltpu

D_MODEL = 1024
DEPTH = 4
RET_HEADS = 4
RET_D = 64
RET_CHUNK = 128
DIL_HEADS = 6
DIL_D = 64
DIL_DILATIONS = (1, 4, 16)
DIL_W = 128
MLA_HEADS = 6
MLA_Q_RANK = 384
MLA_KV_RANK = 256
MLA_NOPE = 64
MLA_ROPE = 32
MLA_V = 64
N_EXPERTS = 8
ROPE_THETA = 10000.0
LN_EPS = 1e-5
RMS_EPS = 1e-6
ALPHA = (2.0 * DEPTH) ** 0.25

LANES = 128
RET_W = RET_HEADS * RET_D
DIL_WD = DIL_HEADS * DIL_D
MLA_PAD = MLA_HEADS * LANES
MLA_VW = MLA_HEADS * MLA_V
RET_IN = 4 * RET_W
DIL_IN = 3 * DIL_WD
MLA_OFF = RET_IN + DIL_IN
D_IN_PAD = MLA_OFF + MLA_Q_RANK + MLA_KV_RANK + LANES
KR_LANE = MLA_NOPE

VMEM_LIMIT = 48 * 1024 * 1024
FFN_VMEM_LIMIT = 56 * 1024 * 1024
FF_CHUNK = 512
BF16 = jnp.bfloat16
F32 = jnp.float32
NEG_INF = float("-inf")
LOG2E = math.log2(math.e)
MLA_Q_SCALE = (MLA_NOPE + MLA_ROPE) ** -0.5 * LOG2E


def _cparams(sem, vmem_limit=VMEM_LIMIT):
    return pltpu.CompilerParams(dimension_semantics=sem, vmem_limit_bytes=vmem_limit)


def _dot(a, b):
    return jnp.dot(a, b, preferred_element_type=F32)


def _dot_nt(a, b):
    return lax.dot_general(a, b, (((1,), (1,)), ((), ())), preferred_element_type=F32)


def _dot_tn(a, b):
    return lax.dot_general(a, b, (((0,), (0,)), ((), ())), preferred_element_type=F32)


def _silu(x):
    return x * (1.0 / (1.0 + jnp.exp(-x)))


def _layer_norm(v, g, b):
    mu = jnp.mean(v, axis=-1, keepdims=True)
    d = v - mu
    var = jnp.mean(d * d, axis=-1, keepdims=True)
    return d * lax.rsqrt(var + LN_EPS) * g + b


def _lane_ids(shape):
    return lax.broadcasted_iota(jnp.int32, shape, len(shape) - 1)


def _rope_group(x, cos, sin_signed, first_half, half):
    fwd = pltpu.roll(x, LANES - half, 1)
    bwd = pltpu.roll(x, half, 1)
    return x * cos + jnp.where(first_half, fwd, bwd) * sin_signed


def _ada_kernel(c_ref, w_ref, b_ref, o_ref):
    cond = _silu(c_ref[...])
    o_ref[0] = jnp.dot(cond, w_ref[0], preferred_element_type=F32,
                       precision=lax.Precision.HIGHEST) + b_ref[0]


def _ada_call(c, w_ada, b_ada):
    nl, d, n = w_ada.shape
    b = c.shape[0]
    tn = 1536
    return pl.pallas_call(
        _ada_kernel,
        out_shape=jax.ShapeDtypeStruct((nl, b, n), F32),
        grid=(nl, n // tn),
        in_specs=[pl.BlockSpec((b, d), lambda l, j: (0, 0)),
                  pl.BlockSpec((1, d, tn), lambda l, j: (l, 0, j)),
                  pl.BlockSpec((1, 1, tn), lambda l, j: (l, 0, j))],
        out_specs=pl.BlockSpec((1, b, tn), lambda l, j: (l, 0, j)),
        compiler_params=_cparams(("arbitrary", "arbitrary")),
        name="ada_mod",
    )(c, w_ada, b_ada.reshape(nl, 1, n))


def _inproj_kernel(x_ref, sc_ref, sh_ref, w_ref, c64_ref, s64_ref, cm_ref, sm_ref,
                   qg_ref, kvg_ref, wuq_ref, wkn_ref, wv_ref,
                   rq_ref, rk_ref, rv_ref, rg_ref,
                   dq1_ref, dk1_ref, dv1_ref, dq4_ref, dk4_ref, dv4_ref, dq16_ref, dk16_ref, dv16_ref,
                   mq_ref, mk_ref, mv_ref, scr_ref, *, tm):
    h = (x_ref[0] * (1.0 + sc_ref[0]) + sh_ref[0]).astype(BF16)
    c64 = c64_ref[0]
    s64 = s64_ref[0]
    cm = cm_ref[0]
    sm = sm_ref[0]
    lane = _lane_ids((1, LANES))
    first64 = (lane % 64) < 32
    firstm = lane < (KR_LANE + MLA_ROPE // 2)

    def rope64(zc):
        return _rope_group(zc, c64, s64, first64, 32)

    def ropem(zc):
        return _rope_group(zc, cm, sm, firstm, MLA_ROPE // 2)

    za = _dot(h, w_ref[:, 0:RET_IN])
    for j in range(RET_W // LANES):
        sl = slice(j * LANES, (j + 1) * LANES)
        rq_ref[0, :, sl] = rope64(za[:, j * LANES:(j + 1) * LANES]).astype(BF16)
        kc = za[:, RET_W + j * LANES:RET_W + (j + 1) * LANES]
        rk_ref[0, :, sl] = (rope64(kc) * (RET_D ** -0.5)).astype(BF16)
    rv_ref[0] = za[:, 2 * RET_W:3 * RET_W].astype(BF16)
    rg_ref[0] = za[:, 3 * RET_W:4 * RET_W]

    zb = _dot(h, w_ref[:, RET_IN:MLA_OFF])
    q_scale = (DIL_D ** -0.5) * LOG2E
    views = ((dq1_ref, dq4_ref, dq16_ref), (dk1_ref, dk4_ref, dk16_ref), (dv1_ref, dv4_ref, dv16_ref))
    for a, (n1, n4, n16) in enumerate(views):
        for j in range(DIL_WD // LANES):
            zc = zb[:, a * DIL_WD + j * LANES:a * DIL_WD + (j + 1) * LANES]
            if a == 0:
                zc = rope64(zc) * q_scale
            elif a == 1:
                zc = rope64(zc)
            scr_ref[j] = zc
            n1[0, :, j * LANES:(j + 1) * LANES] = zc.astype(BF16)
        for r, ref in ((4, n4), (16, n16)):
            for rho in range(r):
                for j in range(DIL_WD // LANES):
                    ref[0, :, rho * DIL_WD + j * LANES:rho * DIL_WD + (j + 1) * LANES] = (
                        scr_ref[j, pl.ds(rho, tm // r, stride=r), :].astype(BF16))

    zc = _dot(h, w_ref[:, MLA_OFF:D_IN_PAD])
    cq = zc[:, 0:MLA_Q_RANK]
    ckv = zc[:, MLA_Q_RANK:MLA_Q_RANK + MLA_KV_RANK]
    kr = ropem(zc[:, MLA_Q_RANK + MLA_KV_RANK:])
    cqn = (cq * lax.rsqrt(jnp.mean(cq * cq, axis=-1, keepdims=True) + RMS_EPS) * qg_ref[...]).astype(BF16)
    ckvn = (ckv * lax.rsqrt(jnp.mean(ckv * ckv, axis=-1, keepdims=True) + RMS_EPS) * kvg_ref[...]).astype(BF16)
    q = _dot(cqn, wuq_ref[...])
    kn = _dot(ckvn, wkn_ref[...])
    for hh in range(MLA_HEADS):
        sl = slice(hh * LANES, (hh + 1) * LANES)
        mq_ref[0, :, sl] = (ropem(q[:, hh * LANES:(hh + 1) * LANES]) * MLA_Q_SCALE).astype(BF16)
        mk_ref[0, :, sl] = (kn[:, hh * LANES:(hh + 1) * LANES] + kr).astype(BF16)
    mv_ref[0] = _dot(ckvn, wv_ref[...]).astype(BF16)


def _inproj_call(x, sc, sh, w_in_p, tabs, qg, kvg, wuq_p, wkn_p, wv, tm=512):
    b, s, d = x.shape
    c64, s64, cm, sm = tabs
    tok = lambda w: pl.BlockSpec((1, tm, w), lambda i, j: (i, j, 0))
    per_b = pl.BlockSpec((1, 1, d), lambda i, j: (i, 0, 0))
    full = lambda a: pl.BlockSpec(a.shape, lambda i, j: (0,) * a.ndim)
    outs = [(1, RET_W, BF16), (1, RET_W, BF16), (1, RET_W, BF16), (1, RET_W, F32)]
    outs += [(r, DIL_WD, BF16) for r in DIL_DILATIONS for _ in range(3)]
    outs += [(1, MLA_PAD, BF16), (1, MLA_PAD, BF16), (1, MLA_VW, BF16)]
    res = pl.pallas_call(
        functools.partial(_inproj_kernel, tm=tm),
        out_shape=[jax.ShapeDtypeStruct((b, s // r, r * w), dt) for r, w, dt in outs],
        grid=(b, s // tm),
        in_specs=[tok(d), per_b, per_b, full(w_in_p), tok(LANES), tok(LANES), tok(LANES), tok(LANES),
                  full(qg), full(kvg), full(wuq_p), full(wkn_p), full(wv)],
        out_specs=[pl.BlockSpec((1, tm // r, r * w), lambda i, j: (i, j, 0)) for r, w, _ in outs],
        scratch_shapes=[pltpu.VMEM((DIL_WD // LANES, tm, LANES), F32)],
        compiler_params=_cparams(("parallel", "parallel")),
        name="in_proj",
    )(x, sc, sh, w_in_p, c64, s64, cm, sm, qg, kvg, wuq_p, wkn_p, wv)
    ret = res[0:4]
    dil = [res[4 + 3 * i:7 + 3 * i] for i in range(len(DIL_DILATIONS))]
    mla = res[4 + 3 * len(DIL_DILATIONS):]
    return ret, dil, mla


def _ret_kernel(q_ref, k_ref, v_ref, g_ref, gn_ref, intra_ref, xi_ref, zeta_ref, dmat_ref, bmask_ref,
                o_ref, state_ref, *, chunks):
    @pl.when(pl.program_id(1) == 0)
    def _():
        state_ref[...] = jnp.zeros_like(state_ref)

    lane = _lane_ids((1, LANES))
    head_a = lane < RET_D
    c = RET_CHUNK
    for ci in range(chunks):
        rows = slice(ci * c, (ci + 1) * c)
        for p in range(RET_W // LANES):
            cols = slice(p * LANES, (p + 1) * LANES)
            q = q_ref[0, rows, cols]
            k = k_ref[0, rows, cols]
            v = v_ref[0, rows, cols]
            zero = jnp.zeros_like(q)
            s_a = _dot_nt(jnp.where(head_a, q, zero), k) * intra_ref[2 * p]
            s_b = _dot_nt(jnp.where(head_a, zero, q), k) * intra_ref[2 * p + 1]
            s_cat = jnp.concatenate([s_a, s_b], axis=1).astype(BF16)
            v_stack = jnp.concatenate([jnp.where(head_a, v, zero), jnp.where(head_a, zero, v)], axis=0)
            inner = _dot(s_cat, v_stack)
            state = state_ref[p]
            cross = _dot(q, state.astype(BF16)) * xi_ref[:, cols]
            kz = (k.astype(F32) * zeta_ref[:, cols]).astype(BF16)
            state_ref[p] = state * dmat_ref[p] + _dot_tn(kz, v) * bmask_ref[...]
            o = inner + cross
            inv_n = 1.0 / RET_D
            mu = jnp.where(head_a,
                           jnp.sum(jnp.where(head_a, o, 0.0), axis=-1, keepdims=True),
                           jnp.sum(jnp.where(head_a, 0.0, o), axis=-1, keepdims=True)) * inv_n
            d = o - mu
            dd = d * d
            var = jnp.where(head_a,
                            jnp.sum(jnp.where(head_a, dd, 0.0), axis=-1, keepdims=True),
                            jnp.sum(jnp.where(head_a, 0.0, dd), axis=-1, keepdims=True)) * inv_n
            on = d * lax.rsqrt(var + LN_EPS) * gn_ref[:, cols]
            o_ref[0, rows, cols] = (_silu(g_ref[0, rows, cols]) * on).astype(BF16)


def _ret_tables():
    c = RET_CHUNK
    log_g = jnp.log(1.0 - 2.0 ** (-5.0 - jnp.arange(RET_HEADS, dtype=F32)))
    idx = jnp.arange(c, dtype=F32)
    diff = idx[:, None] - idx[None, :]
    intra = jnp.where(diff >= 0, jnp.exp(jnp.maximum(diff, 0.0) * log_g[:, None, None]), 0.0)
    xi = jnp.exp((idx[:, None] + 1.0) * log_g[None, :])
    zeta = jnp.exp((c - 1.0 - idx[:, None]) * log_g[None, :])
    decay = jnp.exp(c * log_g)
    xi_l = jnp.repeat(xi, RET_D, axis=1)
    zeta_l = jnp.repeat(zeta, RET_D, axis=1)
    r = jnp.arange(LANES)
    same = (r[:, None] // RET_D) == (r[None, :] // RET_D)
    bmask = same.astype(F32)
    dec_l = jnp.repeat(decay, RET_D).reshape(RET_W // LANES, 1, LANES)
    dmat = bmask[None] * dec_l
    return intra, xi_l, zeta_l, dmat, bmask


def _ret_call(rq, rk, rv, rg, gn, tables, chunks=4):
    b, s, w = rq.shape
    tm = chunks * RET_CHUNK
    intra, xi_l, zeta_l, dmat, bmask = tables
    tok = pl.BlockSpec((1, tm, w), lambda i, j: (i, j, 0))
    full = lambda a: pl.BlockSpec(a.shape, lambda i, j: (0,) * a.ndim)
    return pl.pallas_call(
        functools.partial(_ret_kernel, chunks=chunks),
        out_shape=jax.ShapeDtypeStruct((b, s, w), BF16),
        grid=(b, s // tm),
        in_specs=[tok, tok, tok, tok, full(gn), full(intra), full(xi_l), full(zeta_l), full(dmat), full(bmask)],
        out_specs=tok,
        scratch_shapes=[pltpu.VMEM((w // LANES, LANES, LANES), F32)],
        compiler_params=_cparams(("parallel", "arbitrary")),
        name="retention",
    )(rq, rk, rv, rg, gn, intra, xi_l, zeta_l, dmat, bmask)


STAT_L_OFF = 8


def _dil_band(first_block):
    w = DIL_W
    qi = lax.broadcasted_iota(jnp.int32, (w, 2 * w), 0)
    kj = lax.broadcasted_iota(jnp.int32, (w, 2 * w), 1)
    dist = w + qi - kj
    band = (dist >= 0) & (dist <= w)
    if first_block is False:
        return band
    return band & (jnp.logical_not(first_block) | (kj >= w))


def _dil_block(q, kcat, vcat, valid):
    head_a = _lane_ids((1, LANES)) < DIL_D
    zero = jnp.zeros_like(q)
    zv = jnp.zeros_like(vcat)
    ms, ls, ps = [], [], []
    for qh in (jnp.where(head_a, q, zero), jnp.where(head_a, zero, q)):
        sc = jnp.where(valid, _dot_nt(qh, kcat), NEG_INF)
        m = jnp.max(sc, axis=-1, keepdims=True)
        pe = jnp.exp2(sc - m)
        ms.append(m)
        ls.append(jnp.sum(pe, axis=-1, keepdims=True))
        ps.append(pe.astype(BF16))
    v_stack = jnp.concatenate([jnp.where(head_a, vcat, zv), jnp.where(head_a, zv, vcat)], axis=0)
    return _dot(jnp.concatenate(ps, axis=1), v_stack), ms, ls


def _dil_partial_kernel(q_ref, kp_ref, kc_ref, vp_ref, vc_ref, acc_ref, st_ref, *, r):
    valid = _dil_band(pl.program_id(1) == 0)
    lane = _lane_ids((1, LANES))
    for rho in range(r):
        st = jnp.zeros((DIL_W, LANES), F32)
        rows = pl.ds(rho, DIL_W, stride=r)
        for p in range(DIL_WD // LANES):
            cols = slice(rho * DIL_WD + p * LANES, rho * DIL_WD + (p + 1) * LANES)
            kcat = jnp.concatenate([kp_ref[0, :, cols], kc_ref[0, :, cols]], axis=0)
            vcat = jnp.concatenate([vp_ref[0, :, cols], vc_ref[0, :, cols]], axis=0)
            acc, ms, ls = _dil_block(q_ref[0, :, cols], kcat, vcat, valid)
            acc_ref[0, p, rows, :] = acc
            for j in range(2):
                st = jnp.where(lane == 2 * p + j, ms[j], st)
                st = jnp.where(lane == STAT_L_OFF + 2 * p + j, ls[j], st)
        st_ref[0, rows, :] = st


def _dil_final_kernel(q_ref, k_ref, kh_ref, v_ref, vh_ref, acc4_ref, st4_ref, acc16_ref, st16_ref, o_ref, *, nblk):
    lane = _lane_ids((1, LANES))
    head_a = lane < DIL_D
    w = DIL_W

    def col(st, idx):
        return jnp.sum(jnp.where(lane == idx, st, 0.0), axis=-1, keepdims=True)

    for i in range(nblk):
        rows = slice(i * w, (i + 1) * w)
        valid = _dil_band(pl.program_id(1) == 0 if i == 0 else False)
        st4 = st4_ref[0, rows]
        st16 = st16_ref[0, rows]
        for p in range(DIL_WD // LANES):
            cols = slice(p * LANES, (p + 1) * LANES)
            if i == 0:
                kcat = jnp.concatenate([kh_ref[0, :, cols], k_ref[0, rows, cols]], axis=0)
                vcat = jnp.concatenate([vh_ref[0, :, cols], v_ref[0, rows, cols]], axis=0)
            else:
                kcat = k_ref[0, (i - 1) * w:(i + 1) * w, cols]
                vcat = v_ref[0, (i - 1) * w:(i + 1) * w, cols]
            acc1, ms, ls = _dil_block(q_ref[0, rows, cols], kcat, vcat, valid)
            scale, den = [], []
            for j in range(2):
                hh = 2 * p + j
                m4, l4 = col(st4, hh), col(st4, STAT_L_OFF + hh)
                m16, l16 = col(st16, hh), col(st16, STAT_L_OFF + hh)
                mm = jnp.maximum(jnp.maximum(ms[j], m4), m16)
                a1, a4, a16 = jnp.exp2(ms[j] - mm), jnp.exp2(m4 - mm), jnp.exp2(m16 - mm)
                scale.append((a1, a4, a16))
                den.append(a1 * ls[j] + a4 * l4 + a16 * l16)
            pick = lambda t: jnp.where(head_a, scale[0][t], scale[1][t])
            num = acc1 * pick(0) + acc4_ref[0, p, rows, :] * pick(1) + acc16_ref[0, p, rows, :] * pick(2)
            o_ref[0, rows, cols] = (num / jnp.where(head_a, den[0], den[1])).astype(BF16)


def _dil_call(views, nblk=4):
    (q1, k1, v1) = views[0]
    b, s, w = q1.shape
    npair = w // LANES
    partial = []
    for r, (q, k, v) in zip(DIL_DILATIONS[1:], views[1:]):
        cur = pl.BlockSpec((1, DIL_W, r * w), lambda bb, n: (bb, n, 0))
        prev = pl.BlockSpec((1, DIL_W, r * w), lambda bb, n: (bb, jnp.maximum(n - 1, 0), 0))
        partial += pl.pallas_call(
            functools.partial(_dil_partial_kernel, r=r),
            out_shape=[jax.ShapeDtypeStruct((b, npair, s, LANES), F32), jax.ShapeDtypeStruct((b, s, LANES), F32)],
            grid=(b, s // (r * DIL_W)),
            in_specs=[cur, prev, cur, prev, cur],
            out_specs=[pl.BlockSpec((1, npair, r * DIL_W, LANES), lambda bb, n: (bb, 0, n, 0)),
                       pl.BlockSpec((1, r * DIL_W, LANES), lambda bb, n: (bb, n, 0))],
            compiler_params=_cparams(("parallel", "arbitrary")),
            name=f"dilated_r{r}",
        )(q, k, k, v, v)
    tm = nblk * DIL_W
    tok = lambda width: pl.BlockSpec((1, tm, width), lambda bb, n: (bb, n, 0))
    halo = pl.BlockSpec((1, DIL_W, w), lambda bb, n: (bb, jnp.maximum(n * nblk - 1, 0), 0))
    accs = pl.BlockSpec((1, npair, tm, LANES), lambda bb, n: (bb, 0, n, 0))
    return pl.pallas_call(
        functools.partial(_dil_final_kernel, nblk=nblk),
        out_shape=jax.ShapeDtypeStruct((b, s, w), BF16),
        grid=(b, s // tm),
        in_specs=[tok(w), tok(w), halo, tok(w), halo, accs, tok(LANES), accs, tok(LANES)],
        out_specs=tok(w),
        compiler_params=_cparams(("parallel", "arbitrary")),
        name="dilated_r1_merge",
    )(q1, k1, k1, v1, v1, *partial)


def _mla_kernel(q_ref, k_ref, v_ref, o_ref, m_ref, l_ref, acc_ref, *, t, sub):
    qi = pl.program_id(2)
    ki = pl.program_id(3)
    head_a = _lane_ids((1, LANES)) < MLA_V
    nsub = t // sub

    @pl.when(ki == 0)
    def _():
        m_ref[...] = jnp.full_like(m_ref, NEG_INF)
        l_ref[...] = jnp.zeros_like(l_ref)
        acc_ref[...] = jnp.zeros_like(acc_ref)

    def v_stack(c):
        v = v_ref[0, c * sub:(c + 1) * sub]
        zv = jnp.zeros_like(v)
        return jnp.concatenate([jnp.where(head_a, v, zv), jnp.where(head_a, zv, v)], axis=0)

    def tile(r, c, vs, masked):
        rows = slice(r * sub, (r + 1) * sub)
        ps, alphas = [], []
        for j in range(2):
            q = q_ref[0, rows, j * LANES:(j + 1) * LANES]
            k = k_ref[0, c * sub:(c + 1) * sub, j * LANES:(j + 1) * LANES]
            s = _dot_nt(q, k)
            if masked:
                row = lax.broadcasted_iota(jnp.int32, (sub, sub), 0)
                colk = lax.broadcasted_iota(jnp.int32, (sub, sub), 1)
                s = jnp.where(colk <= row, s, NEG_INF)
            m_old = m_ref[j, rows]
            m_new = jnp.maximum(m_old, jnp.max(s, axis=-1, keepdims=True))
            alpha = jnp.exp2(m_old - m_new)
            pe = jnp.exp2(s - pltpu.repeat(m_new, sub // LANES, axis=1))
            l_ref[j, rows] = alpha * l_ref[j, rows] + jnp.sum(pe, axis=-1, keepdims=True)
            m_ref[j, rows] = m_new
            ps.append(pe.astype(BF16))
            alphas.append(alpha)
        acc_ref[rows] = acc_ref[rows] * jnp.where(head_a, alphas[0], alphas[1]) + _dot(
            jnp.concatenate(ps, axis=1), vs)

    @pl.when(ki < qi)
    def _():
        for c in range(nsub):
            vs = v_stack(c)
            for r in range(nsub):
                tile(r, c, vs, False)

    @pl.when(ki == qi)
    def _():
        for c in range(nsub):
            vs = v_stack(c)
            for r in range(c, nsub):
                tile(r, c, vs, r == c)
        o_ref[0] = (acc_ref[...] / jnp.where(head_a, l_ref[0], l_ref[1])).astype(BF16)


def _mla_call(mq, mk, mv, t=1024, sub=512):
    b, s, _ = mq.shape
    n = s // t
    return pl.pallas_call(
        functools.partial(_mla_kernel, t=t, sub=sub),
        out_shape=jax.ShapeDtypeStruct((b, s, MLA_VW), BF16),
        grid=(b, MLA_HEADS // 2, n, n),
        in_specs=[pl.BlockSpec((1, t, 2 * LANES), lambda bb, p, i, j: (bb, i, p)),
                  pl.BlockSpec((1, t, 2 * LANES), lambda bb, p, i, j: (bb, jnp.minimum(j, i), p)),
                  pl.BlockSpec((1, t, LANES), lambda bb, p, i, j: (bb, jnp.minimum(j, i), p))],
        out_specs=pl.BlockSpec((1, t, LANES), lambda bb, p, i, j: (bb, i, p)),
        scratch_shapes=[pltpu.VMEM((2, t, LANES), F32), pltpu.VMEM((2, t, LANES), F32),
                        pltpu.VMEM((t, LANES), F32)],
        compiler_params=_cparams(("parallel", "parallel", "parallel", "arbitrary")),
        name="mla_attention",
    )(mq, mk, mv)


def _outproj_kernel(ya_ref, yb_ref, yc_ref, wa_ref, wb_ref, wc_ref, x_ref, g_ref, lg_ref, lb_ref, o_ref):
    y = _dot(ya_ref[0], wa_ref[...]) + _dot(yb_ref[0], wb_ref[...]) + _dot(yc_ref[0], wc_ref[...])
    v = ALPHA * x_ref[0] + (1.0 + g_ref[0]) * y
    o_ref[0] = _layer_norm(v, lg_ref[...], lb_ref[...])


def _outproj_call(ya, yb, yc, wa, wb, wc, x, g1, lg, lb, tm=512):
    b, s, d = x.shape
    tok = lambda w: pl.BlockSpec((1, tm, w), lambda i, j: (i, j, 0))
    per_b = pl.BlockSpec((1, 1, d), lambda i, j: (i, 0, 0))
    full = lambda a: pl.BlockSpec(a.shape, lambda i, j: (0,) * a.ndim)
    return pl.pallas_call(
        _outproj_kernel,
        out_shape=jax.ShapeDtypeStruct((b, s, d), F32),
        grid=(b, s // tm),
        in_specs=[tok(ya.shape[-1]), tok(yb.shape[-1]), tok(yc.shape[-1]), full(wa), full(wb), full(wc),
                  tok(d), per_b, full(lg), full(lb)],
        out_specs=tok(d),
        compiler_params=_cparams(("parallel", "parallel")),
        name="out_proj_ln",
    )(ya, yb, yc, wa, wb, wc, x, g1, lg, lb)


def _swiglu(h, w1_ref, w3_ref, w2_ref, chunk):
    ff = w1_ref.shape[1]
    y = None
    for a in range(0, ff, chunk):
        b = min(a + chunk, ff)
        mid = (_silu(_dot(h, w1_ref[:, a:b])) * _dot(h, w3_ref[:, a:b])).astype(BF16)
        part = _dot(mid, w2_ref[a:b, :])
        y = part if y is None else y + part
    return y


def _ffn_kernel(x_ref, sc_ref, sh_ref, g_ref, w1_ref, w3_ref, w2_ref, lg_ref, lb_ref, o_ref, *, chunk):
    x = x_ref[...]
    h = (x * (1.0 + sc_ref[0]) + sh_ref[0]).astype(BF16)
    y = _swiglu(h, w1_ref, w3_ref, w2_ref, chunk)
    o_ref[...] = _layer_norm(ALPHA * x + (1.0 + g_ref[0]) * y, lg_ref[...], lb_ref[...])


def _ffn_call(x2d, sc, sh, g2, w1, w3, w2, lg, lb, tiles_per_batch, tm, chunk):
    t, d = x2d.shape
    per_b = pl.BlockSpec((1, 1, d), lambda i: (i // tiles_per_batch, 0, 0))
    vec = pl.BlockSpec((1, d), lambda i: (0, 0))
    resident = lambda a: pl.BlockSpec(a.shape, lambda i: (0, 0), pipeline_mode=pl.Buffered(1))
    return pl.pallas_call(
        functools.partial(_ffn_kernel, chunk=chunk),
        out_shape=jax.ShapeDtypeStruct((t, d), F32),
        grid=(t // tm,),
        in_specs=[pl.BlockSpec((tm, d), lambda i: (i, 0)), per_b, per_b, per_b,
                  resident(w1), resident(w3), resident(w2), vec, vec],
        out_specs=pl.BlockSpec((tm, d), lambda i: (i, 0)),
        compiler_params=_cparams(("parallel",), FFN_VMEM_LIMIT),
        name="ffn_dense",
    )(x2d, sc, sh, g2, w1, w3, w2, lg, lb)


ROUTE_G_OFF = 2


def _router_kernel(x_ref, sc_ref, sh_ref, wr_ref, h_ref, r_ref):
    h = x_ref[...] * (1.0 + sc_ref[0]) + sh_ref[0]
    h_ref[...] = h.astype(BF16)
    logits = jnp.dot(h, wr_ref[...], preferred_element_type=F32, precision=lax.Precision.HIGHEST)
    lane = _lane_ids(logits.shape)
    lg = jnp.where(lane < N_EXPERTS, logits, NEG_INF)
    m1 = jnp.max(lg, axis=-1, keepdims=True)
    i1 = jnp.min(jnp.where(lg == m1, lane, LANES), axis=-1, keepdims=True)
    lg2 = jnp.where(lane == i1, NEG_INF, lg)
    m2 = jnp.max(lg2, axis=-1, keepdims=True)
    i2 = jnp.min(jnp.where(lg2 == m2, lane, LANES), axis=-1, keepdims=True)
    e2 = jnp.exp(m2 - m1)
    den = 1.0 + e2
    out = jnp.where(lane == 0, i1.astype(F32), 0.0)
    out = jnp.where(lane == 1, i2.astype(F32), out)
    out = jnp.where(lane == ROUTE_G_OFF, 1.0 / den, out)
    out = jnp.where(lane == ROUTE_G_OFF + 1, e2 / den, out)
    r_ref[...] = out


def _router_call(x2d, sc, sh, wr_p, tiles_per_batch, tm):
    t, d = x2d.shape
    per_b = pl.BlockSpec((1, 1, d), lambda i: (i // tiles_per_batch, 0, 0))
    return pl.pallas_call(
        _router_kernel,
        out_shape=[jax.ShapeDtypeStruct((t, d), BF16), jax.ShapeDtypeStruct((t, LANES), F32)],
        grid=(t // tm,),
        in_specs=[pl.BlockSpec((tm, d), lambda i: (i, 0)), per_b, per_b,
                  pl.BlockSpec(wr_p.shape, lambda i: (0, 0))],
        out_specs=[pl.BlockSpec((tm, d), lambda i: (i, 0)), pl.BlockSpec((tm, LANES), lambda i: (i, 0))],
        compiler_params=_cparams(("parallel",)),
        name="moe_router",
    )(x2d, sc, sh, wr_p)


def _expert_kernel(te_ref, nu_ref, x_ref, w1_ref, w3_ref, w2_ref, o_ref, *, chunk):
    i = pl.program_id(0)

    @pl.when(i < nu_ref[0])
    def _():
        o_ref[...] = _swiglu(x_ref[...], w1_ref.at[0], w3_ref.at[0], w2_ref.at[0], chunk)

    @pl.when(i >= nu_ref[0])
    def _():
        o_ref[...] = jnp.zeros_like(o_ref)


def _expert_call(tile_expert, n_used, xs, w1, w3, w2, tm, chunk):
    p, d = xs.shape
    ff = w1.shape[2]
    wspec = lambda shape: pl.BlockSpec(shape, lambda i, te, nu: (te[i], 0, 0), pipeline_mode=pl.Buffered(1))
    return pl.pallas_call(
        functools.partial(_expert_kernel, chunk=chunk),
        out_shape=jax.ShapeDtypeStruct((p, d), F32),
        grid_spec=pltpu.PrefetchScalarGridSpec(
            num_scalar_prefetch=2,
            grid=(p // tm,),
            in_specs=[pl.BlockSpec((tm, d), lambda i, te, nu: (i, 0)),
                      wspec((1, d, ff)), wspec((1, d, ff)), wspec((1, ff, d))],
            out_specs=pl.BlockSpec((tm, d), lambda i, te, nu: (i, 0))),
        compiler_params=_cparams(("arbitrary",), FFN_VMEM_LIMIT),
        name="moe_experts",
    )(tile_expert, n_used, xs, w1, w3, w2)


def _combine_kernel(x_ref, ya_ref, yb_ref, r_ref, g_ref, lg_ref, lb_ref, o_ref):
    r = r_ref[...]
    lane = _lane_ids(r.shape)
    ga = jnp.sum(jnp.where(lane == ROUTE_G_OFF, r, 0.0), axis=-1, keepdims=True)
    gb = jnp.sum(jnp.where(lane == ROUTE_G_OFF + 1, r, 0.0), axis=-1, keepdims=True)
    y = ga * ya_ref[...] + gb * yb_ref[...]
    v = ALPHA * x_ref[...] + (1.0 + g_ref[0]) * y
    o_ref[...] = _layer_norm(v, lg_ref[...], lb_ref[...])


def _combine_call(x2d, ya, yb, route, g2, lg, lb, tiles_per_batch, tm):
    t, d = x2d.shape
    tok = pl.BlockSpec((tm, d), lambda i: (i, 0))
    per_b = pl.BlockSpec((1, 1, d), lambda i: (i // tiles_per_batch, 0, 0))
    vec = pl.BlockSpec((1, d), lambda i: (0, 0))
    return pl.pallas_call(
        _combine_kernel,
        out_shape=jax.ShapeDtypeStruct((t, d), F32),
        grid=(t // tm,),
        in_specs=[tok, tok, tok, pl.BlockSpec((tm, LANES), lambda i: (i, 0)), per_b, vec, vec],
        out_specs=tok,
        compiler_params=_cparams(("parallel",)),
        name="moe_combine_ln",
    )(x2d, ya, yb, route, g2, lg, lb)


def _dispatch_plan(route, tm):
    t = route.shape[0]
    e = route[:, 0:2].astype(jnp.int32).reshape(-1)
    onehot = (e[:, None] == jnp.arange(N_EXPERTS, dtype=jnp.int32)[None, :]).astype(jnp.int32)
    csum = jnp.cumsum(onehot, axis=0)
    counts = csum[-1]
    rank = jnp.take_along_axis(csum, e[:, None], axis=1)[:, 0] - 1
    padded = ((counts + tm - 1) // tm) * tm
    ends = jnp.cumsum(padded)
    starts = ends - padded
    dest = starts[e] + rank
    n_rows = 2 * t + N_EXPERTS * tm
    row_token = jnp.zeros((n_rows,), jnp.int32).at[dest].set(jnp.arange(2 * t, dtype=jnp.int32) // 2)
    tile_start = jnp.arange(n_rows // tm, dtype=jnp.int32) * tm
    tile_expert = jnp.minimum(jnp.sum((tile_start[:, None] >= ends[None, :]).astype(jnp.int32), axis=1),
                              N_EXPERTS - 1)
    n_used = (ends[-1] // tm).astype(jnp.int32).reshape(1)
    return row_token, dest, tile_expert, n_used


def _moe_layer(x2d, sc, sh, g2, wr_p, w1, w3, w2, lg, lb, tiles_per_batch, tm, tme, tf):
    h, route = _router_call(x2d, sc, sh, wr_p, tiles_per_batch, tm)
    row_token, dest, tile_expert, n_used = _dispatch_plan(route, tme)
    xs = jnp.take(h, row_token, axis=0)
    ys = _expert_call(tile_expert, n_used, xs, w1, w3, w2, tme, tf)
    ysel = jnp.take(ys, dest, axis=0).reshape(x2d.shape[0], 2, x2d.shape[1])
    return _combine_call(x2d, ysel[:, 0], ysel[:, 1], route, g2, lg, lb, tiles_per_batch, tm)


def _rope_tables(positions):
    pos = positions.astype(F32)[..., None]
    lane = jnp.arange(LANES)
    inv64 = ROPE_THETA ** (-jnp.arange(0, RET_D, 2, dtype=F32) / RET_D)
    invm = ROPE_THETA ** (-jnp.arange(0, MLA_ROPE, 2, dtype=F32) / MLA_ROPE)
    ang64 = pos * inv64
    angm = pos * invm
    cos64, sin64, cosm, sinm = jnp.cos(ang64), jnp.sin(ang64), jnp.cos(angm), jnp.sin(angm)
    sign64 = jnp.where((lane % 64) < 32, -1.0, 1.0)
    c64 = jnp.tile(cos64, (1, 1, LANES // 32))
    s64 = jnp.tile(sin64, (1, 1, LANES // 32)) * sign64
    in_rope = (lane >= KR_LANE) & (lane < KR_LANE + MLA_ROPE)
    signm = jnp.where(lane < KR_LANE + MLA_ROPE // 2, -1.0, 1.0)
    cm = jnp.where(in_rope, jnp.tile(cosm, (1, 1, LANES // 16)), 1.0)
    sm = jnp.where(in_rope, jnp.tile(sinm, (1, 1, LANES // 16)) * signm, 0.0)
    return c64, s64, cm, sm


def _prep_mixer_weights(w_in, w_uq, w_ukv):
    d = w_in.shape[0]
    kr_cols = jnp.zeros((d, LANES), w_in.dtype).at[:, KR_LANE:KR_LANE + MLA_ROPE].set(
        w_in[:, MLA_OFF + MLA_Q_RANK + MLA_KV_RANK:])
    w_in_p = jnp.concatenate([w_in[:, :MLA_OFF + MLA_Q_RANK + MLA_KV_RANK], kr_cols], axis=1).astype(BF16)
    uq = w_uq.reshape(MLA_Q_RANK, MLA_HEADS, MLA_NOPE + MLA_ROPE)
    wuq_p = jnp.pad(uq, ((0, 0), (0, 0), (0, LANES - MLA_NOPE - MLA_ROPE))).reshape(MLA_Q_RANK, MLA_PAD).astype(BF16)
    ukv = w_ukv.reshape(MLA_KV_RANK, MLA_HEADS, MLA_NOPE + MLA_V)
    wkn_p = jnp.pad(ukv[:, :, :MLA_NOPE], ((0, 0), (0, 0), (0, LANES - MLA_NOPE))).reshape(MLA_KV_RANK, MLA_PAD).astype(BF16)
    wv = ukv[:, :, MLA_NOPE:].reshape(MLA_KV_RANK, MLA_VW).astype(BF16)
    return w_in_p, wuq_p, wkn_p, wv


def kernel(x, c, positions, w_in, ret_gn_g, mla_qn_g, mla_kvn_g, w_uq, w_ukv, w_out, w_ada, b_ada, ln1_g, ln1_b, ln2_g, ln2_b, w1_dense, w3_dense, w2_dense, w_router, w1_moe, w3_moe, w2_moe):
    b, s, d = x.shape
    tabs = _rope_tables(positions)
    ret_tabs = _ret_tables()
    mod = _ada_call(c, w_ada, b_ada)
    tm_tok = 512
    tiles_per_batch = s // tm_tok
    for l in range(w_in.shape[0]):
        sh1, sc1, g1, sh2, sc2, g2 = [mod[l, :, j * d:(j + 1) * d].reshape(b, 1, d) for j in range(6)]
        w_in_p, wuq_p, wkn_p, wv = _prep_mixer_weights(w_in[l], w_uq[l], w_ukv[l])
        (rq, rk, rv, rg), dil_views, (mq, mk, mv) = _inproj_call(
            x, sc1, sh1, w_in_p, tabs, mla_qn_g[l].reshape(1, -1), mla_kvn_g[l].reshape(1, -1), wuq_p, wkn_p, wv)
        ya = _ret_call(rq, rk, rv, rg, ret_gn_g[l].reshape(1, -1), ret_tabs)
        yb = _dil_call(dil_views)
        yc = _mla_call(mq, mk, mv)
        wo = w_out[l].astype(BF16)
        x = _outproj_call(ya, yb, yc, wo[:RET_W], wo[RET_W:RET_W + DIL_WD], wo[RET_W + DIL_WD:], x, g1,
                          ln1_g[l].reshape(1, d), ln1_b[l].reshape(1, d))
        x2d = x.reshape(b * s, d)
        lg, lb = ln2_g[l].reshape(1, d), ln2_b[l].reshape(1, d)
        if l % 2 == 0:
            j = l // 2
            x2d = _ffn_call(x2d, sc2, sh2, g2, w1_dense[j].astype(BF16), w3_dense[j].astype(BF16),
                            w2_dense[j].astype(BF16), lg, lb, tiles_per_batch, tm_tok, FF_CHUNK)
        else:
            j = l // 2
            wr_p = jnp.pad(w_router[j], ((0, 0), (0, LANES - N_EXPERTS)))
            x2d = _moe_layer(x2d, sc2, sh2, g2, wr_p, w1_moe[j].astype(BF16), w3_moe[j].astype(BF16),
                             w2_moe[j].astype(BF16), lg, lb, tiles_per_batch, tm_tok, 512, FF_CHUNK)
        x = x2d.reshape(b, s, d)
    return x
```

```python
import functools
import math

import jax
import jax.numpy as jnp
from jax import lax
from jax.experimental import pallas as pl
from jax.experimental.pallas import tpu as pltpu

D_MODEL = 1024
DEPTH = 4
RET_HEADS = 4
RET_D = 64
RET_CHUNK = 128
DIL_HEADS = 6
DIL_D = 64
DIL_DILATIONS = (1, 4, 16)
DIL_W = 128
MLA_HEADS = 6
MLA_Q_RANK = 384
MLA_KV_RANK = 256
MLA_NOPE = 64
MLA_ROPE = 32
MLA_V = 64
N_EXPERTS = 8
ROPE_THETA = 10000.0
LN_EPS = 1e-5
RMS_EPS = 1e-6
ALPHA = (2.0 * DEPTH) ** 0.25

LANES = 128
RET_W = RET_HEADS * RET_D
DIL_WD = DIL_HEADS * DIL_D
MLA_PAD = MLA_HEADS * LANES
MLA_VW = MLA_HEADS * MLA_V
RET_IN = 4 * RET_W
DIL_IN = 3 * DIL_WD
MLA_OFF = RET_IN + DIL_IN
D_IN_PAD = MLA_OFF + MLA_Q_RANK + MLA_KV_RANK + LANES
KR_LANE = MLA_NOPE

VMEM_LIMIT = 48 * 1024 * 1024
FFN_VMEM_LIMIT = 56 * 1024 * 1024
FF_CHUNK = 512
BF16 = jnp.bfloat16
F32 = jnp.float32
NEG_INF = float("-inf")
LOG2E = math.log2(math.e)
MLA_Q_SCALE = (MLA_NOPE + MLA_ROPE) ** -0.5 * LOG2E


def _cparams(sem, vmem_limit=VMEM_LIMIT):
    return pltpu.CompilerParams(dimension_semantics=sem, vmem_limit_bytes=vmem_limit)


def _dot(a, b):
    return jnp.dot(a, b, preferred_element_type=F32)


def _dot_nt(a, b):
    return lax.dot_general(a, b, (((1,), (1,)), ((), ())), preferred_element_type=F32)


def _dot_tn(a, b):
    return lax.dot_general(a, b, (((0,), (0,)), ((), ())), preferred_element_type=F32)


def _silu(x):
    return x * (1.0 / (1.0 + jnp.exp(-x)))


def _layer_norm(v, g, b):
    mu = jnp.mean(v, axis=-1, keepdims=True)
    d = v - mu
    var = jnp.mean(d * d, axis=-1, keepdims=True)
    return d * lax.rsqrt(var + LN_EPS) * g + b


def _lane_ids(shape):
    return lax.broadcasted_iota(jnp.int32, shape, len(shape) - 1)


def _rope_group(x, cos, sin_signed, first_half, half):
    fwd = pltpu.roll(x, LANES - half, 1)
    bwd = pltpu.roll(x, half, 1)
    return x * cos + jnp.where(first_half, fwd, bwd) * sin_signed


def _ada_kernel(c_ref, w_ref, b_ref, o_ref):
    cond = _silu(c_ref[...])
    o_ref[0] = jnp.dot(cond, w_ref[0], preferred_element_type=F32,
                       precision=lax.Precision.HIGHEST) + b_ref[0]


def _ada_call(c, w_ada, b_ada):
    nl, d, n = w_ada.shape
    b = c.shape[0]
    tn = 1536
    return pl.pallas_call(
        _ada_kernel,
        out_shape=jax.ShapeDtypeStruct((nl, b, n), F32),
        grid=(nl, n // tn),
        in_specs=[pl.BlockSpec((b, d), lambda l, j: (0, 0)),
                  pl.BlockSpec((1, d, tn), lambda l, j: (l, 0, j)),
                  pl.BlockSpec((1, 1, tn), lambda l, j: (l, 0, j))],
        out_specs=pl.BlockSpec((1, b, tn), lambda l, j: (l, 0, j)),
        compiler_params=_cparams(("arbitrary", "arbitrary")),
        name="ada_mod",
    )(c, w_ada, b_ada.reshape(nl, 1, n))


def _inproj_kernel(x_ref, sc_ref, sh_ref, w_ref, c64_ref, s64_ref, cm_ref, sm_ref,
                   qg_ref, kvg_ref, wuq_ref, wkn_ref, wv_ref,
                   rq_ref, rk_ref, rv_ref, rg_ref,
                   dq1_ref, dk1_ref, dv1_ref, dq4_ref, dk4_ref, dv4_ref, dq16_ref, dk16_ref, dv16_ref,
                   mq_ref, mk_ref, mv_ref, scr_ref, *, tm):
    h = (x_ref[0] * (1.0 + sc_ref[0]) + sh_ref[0]).astype(BF16)
    c64 = c64_ref[0]
    s64 = s64_ref[0]
    cm = cm_ref[0]
    sm = sm_ref[0]
    lane = _lane_ids((1, LANES))
    first64 = (lane % 64) < 32
    firstm = lane < (KR_LANE + MLA_ROPE // 2)

    def rope64(zc):
        return _rope_group(zc, c64, s64, first64, 32)

    def ropem(zc):
        return _rope_group(zc, cm, sm, firstm, MLA_ROPE // 2)

    za = _dot(h, w_ref[:, 0:RET_IN])
    for j in range(RET_W // LANES):
        sl = slice(j * LANES, (j + 1) * LANES)
        rq_ref[0, :, sl] = rope64(za[:, j * LANES:(j + 1) * LANES]).astype(BF16)
        kc = za[:, RET_W + j * LANES:RET_W + (j + 1) * LANES]
        rk_ref[0, :, sl] = (rope64(kc) * (RET_D ** -0.5)).astype(BF16)
    rv_ref[0] = za[:, 2 * RET_W:3 * RET_W].astype(BF16)
    rg_ref[0] = za[:, 3 * RET_W:4 * RET_W]

    zb = _dot(h, w_ref[:, RET_IN:MLA_OFF])
    q_scale = (DIL_D ** -0.5) * LOG2E
    views = ((dq1_ref, dq4_ref, dq16_ref), (dk1_ref, dk4_ref, dk16_ref), (dv1_ref, dv4_ref, dv16_ref))
    for a, (n1, n4, n16) in enumerate(views):
        for j in range(DIL_WD // LANES):
            zc = zb[:, a * DIL_WD + j * LANES:a * DIL_WD + (j + 1) * LANES]
            if a == 0:
                zc = rope64(zc) * q_scale
            elif a == 1:
                zc = rope64(zc)
            scr_ref[j] = zc
            n1[0, :, j * LANES:(j + 1) * LANES] = zc.astype(BF16)
        for r, ref in ((4, n4), (16, n16)):
            for rho in range(r):
                for j in range(DIL_WD // LANES):
                    ref[0, :, rho * DIL_WD + j * LANES:rho * DIL_WD + (j + 1) * LANES] = (
                        scr_ref[j, pl.ds(rho, tm // r, stride=r), :].astype(BF16))

    zc = _dot(h, w_ref[:, MLA_OFF:D_IN_PAD])
    cq = zc[:, 0:MLA_Q_RANK]
    ckv = zc[:, MLA_Q_RANK:MLA_Q_RANK + MLA_KV_RANK]
    kr = ropem(zc[:, MLA_Q_RANK + MLA_KV_RANK:])
    cqn = (cq * lax.rsqrt(jnp.mean(cq * cq, axis=-1, keepdims=True) + RMS_EPS) * qg_ref[...]).astype(BF16)
    ckvn = (ckv * lax.rsqrt(jnp.mean(ckv * ckv, axis=-1, keepdims=True) + RMS_EPS) * kvg_ref[...]).astype(BF16)
    q = _dot(cqn, wuq_ref[...])
    kn = _dot(ckvn, wkn_ref[...])
    for hh in range(MLA_HEADS):
        sl = slice(hh * LANES, (hh + 1) * LANES)
        mq_ref[0, :, sl] = (ropem(q[:, hh * LANES:(hh + 1) * LANES]) * MLA_Q_SCALE).astype(BF16)
        mk_ref[0, :, sl] = (kn[:, hh * LANES:(hh + 1) * LANES] + kr).astype(BF16)
    mv_ref[0] = _dot(ckvn, wv_ref[...]).astype(BF16)


def _inproj_call(x, sc, sh, w_in_p, tabs, qg, kvg, wuq_p, wkn_p, wv, tm=512):
    b, s, d = x.shape
    c64, s64, cm, sm = tabs
    tok = lambda w: pl.BlockSpec((1, tm, w), lambda i, j: (i, j, 0))
    per_b = pl.BlockSpec((1, 1, d), lambda i, j: (i, 0, 0))
    full = lambda a: pl.BlockSpec(a.shape, lambda i, j: (0,) * a.ndim)
    outs = [(1, RET_W, BF16), (1, RET_W, BF16), (1, RET_W, BF16), (1, RET_W, F32)]
    outs += [(r, DIL_WD, BF16) for r in DIL_DILATIONS for _ in range(3)]
    outs += [(1, MLA_PAD, BF16), (1, MLA_PAD, BF16), (1, MLA_VW, BF16)]
    res = pl.pallas_call(
        functools.partial(_inproj_kernel, tm=tm),
        out_shape=[jax.ShapeDtypeStruct((b, s // r, r * w), dt) for r, w, dt in outs],
        grid=(b, s // tm),
        in_specs=[tok(d), per_b, per_b, full(w_in_p), tok(LANES), tok(LANES), tok(LANES), tok(LANES),
                  full(qg), full(kvg), full(wuq_p), full(wkn_p), full(wv)],
        out_specs=[pl.BlockSpec((1, tm // r, r * w), lambda i, j: (i, j, 0)) for r, w, _ in outs],
        scratch_shapes=[pltpu.VMEM((DIL_WD // LANES, tm, LANES), F32)],
        compiler_params=_cparams(("parallel", "parallel")),
        name="in_proj",
    )(x, sc, sh, w_in_p, c64, s64, cm, sm, qg, kvg, wuq_p, wkn_p, wv)
    ret = res[0:4]
    dil = [res[4 + 3 * i:7 + 3 * i] for i in range(len(DIL_DILATIONS))]
    mla = res[4 + 3 * len(DIL_DILATIONS):]
    return ret, dil, mla


def _ret_kernel(q_ref, k_ref, v_ref, g_ref, gn_ref, intra_ref, xi_ref, zeta_ref, dmat_ref, bmask_ref,
                o_ref, state_ref, *, chunks):
    @pl.when(pl.program_id(1) == 0)
    def _():
        state_ref[...] = jnp.zeros_like(state_ref)

    lane = _lane_ids((1, LANES))
    head_a = lane < RET_D
    c = RET_CHUNK
    for ci in range(chunks):
        rows = slice(ci * c, (ci + 1) * c)
        for p in range(RET_W // LANES):
            cols = slice(p * LANES, (p + 1) * LANES)
            q = q_ref[0, rows, cols]
            k = k_ref[0, rows, cols]
            v = v_ref[0, rows, cols]
            zero = jnp.zeros_like(q)
            s_a = _dot_nt(jnp.where(head_a, q, zero), k) * intra_ref[2 * p]
            s_b = _dot_nt(jnp.where(head_a, zero, q), k) * intra_ref[2 * p + 1]
            s_cat = jnp.concatenate([s_a, s_b], axis=1).astype(BF16)
            v_stack = jnp.concatenate([jnp.where(head_a, v, zero), jnp.where(head_a, zero, v)], axis=0)
            inner = _dot(s_cat, v_stack)
            state = state_ref[p]
            cross = _dot(q, state.astype(BF16)) * xi_ref[:, cols]
            kz = (k.astype(F32) * zeta_ref[:, cols]).astype(BF16)
            state_ref[p] = state * dmat_ref[p] + _dot_tn(kz, v) * bmask_ref[...]
            o = inner + cross
            inv_n = 1.0 / RET_D
            mu = jnp.where(head_a,
                           jnp.sum(jnp.where(head_a, o, 0.0), axis=-1, keepdims=True),
                           jnp.sum(jnp.where(head_a, 0.0, o), axis=-1, keepdims=True)) * inv_n
            d = o - mu
            dd = d * d
            var = jnp.where(head_a,
                            jnp.sum(jnp.where(head_a, dd, 0.0), axis=-1, keepdims=True),
                            jnp.sum(jnp.where(head_a, 0.0, dd), axis=-1, keepdims=True)) * inv_n
            on = d * lax.rsqrt(var + LN_EPS) * gn_ref[:, cols]
            o_ref[0, rows, cols] = (_silu(g_ref[0, rows, cols]) * on).astype(BF16)


def _ret_tables():
    c = RET_CHUNK
    log_g = jnp.log(1.0 - 2.0 ** (-5.0 - jnp.arange(RET_HEADS, dtype=F32)))
    idx = jnp.arange(c, dtype=F32)
    diff = idx[:, None] - idx[None, :]
    intra = jnp.where(diff >= 0, jnp.exp(jnp.maximum(diff, 0.0) * log_g[:, None, None]), 0.0)
    xi = jnp.exp((idx[:, None] + 1.0) * log_g[None, :])
    zeta = jnp.exp((c - 1.0 - idx[:, None]) * log_g[None, :])
    decay = jnp.exp(c * log_g)
    xi_l = jnp.repeat(xi, RET_D, axis=1)
    zeta_l = jnp.repeat(zeta, RET_D, axis=1)
    r = jnp.arange(LANES)
    same = (r[:, None] // RET_D) == (r[None, :] // RET_D)
    bmask = same.astype(F32)
    dec_l = jnp.repeat(decay, RET_D).reshape(RET_W // LANES, 1, LANES)
    dmat = bmask[None] * dec_l
    return intra, xi_l, zeta_l, dmat, bmask


def _ret_call(rq, rk, rv, rg, gn, tables, chunks=4):
    b, s, w = rq.shape
    tm = chunks * RET_CHUNK
    intra, xi_l, zeta_l, dmat, bmask = tables
    tok = pl.BlockSpec((1, tm, w), lambda i, j: (i, j, 0))
    full = lambda a: pl.BlockSpec(a.shape, lambda i, j: (0,) * a.ndim)
    return pl.pallas_call(
        functools.partial(_ret_kernel, chunks=chunks),
        out_shape=jax.ShapeDtypeStruct((b, s, w), BF16),
        grid=(b, s // tm),
        in_specs=[tok, tok, tok, tok, full(gn), full(intra), full(xi_l), full(zeta_l), full(dmat), full(bmask)],
        out_specs=tok,
        scratch_shapes=[pltpu.VMEM((w // LANES, LANES, LANES), F32)],
        compiler_params=_cparams(("parallel", "arbitrary")),
        name="retention",
    )(rq, rk, rv, rg, gn, intra, xi_l, zeta_l, dmat, bmask)


def _dil_band(first_block):
    w = DIL_W
    qi = lax.broadcasted_iota(jnp.int32, (w, 2 * w), 0)
    kj = lax.broadcasted_iota(jnp.int32, (w, 2 * w), 1)
    dist = w + qi - kj
    band = (dist >= 0) & (dist <= w)
    if first_block is False:
        return band
    return band & (jnp.logical_not(first_block) | (kj >= w))


DIL_GROUP = 2


def _dil_attend(items, valid_of):
    head_a = _lane_ids((1, LANES)) < DIL_D
    scores = []
    for idx, (q, kcat, _) in enumerate(items):
        zero = jnp.zeros_like(q)
        valid = valid_of(idx)
        scores.append([jnp.where(valid, _dot_nt(qh, kcat), NEG_INF)
                       for qh in (jnp.where(head_a, q, zero), jnp.where(head_a, zero, q))])
    soft = []
    for scs in scores:
        ms, ls, ps = [], [], []
        for sc in scs:
            m = jnp.max(sc, axis=-1, keepdims=True)
            pe = jnp.exp2(sc - m)
            ms.append(m)
            ls.append(jnp.sum(pe, axis=-1, keepdims=True))
            ps.append(pe.astype(BF16))
        soft.append((ms, ls, jnp.concatenate(ps, axis=1)))
    out = []
    for (_, _, vcat), (ms, ls, pcat) in zip(items, soft):
        zv = jnp.zeros_like(vcat)
        v_stack = jnp.concatenate([jnp.where(head_a, vcat, zv), jnp.where(head_a, zv, vcat)], axis=0)
        out.append((_dot(pcat, v_stack), ms, ls))
    return out


def _dil_normalise(acc, ms, ls):
    head_a = _lane_ids((1, LANES)) < DIL_D
    o = acc * jnp.where(head_a, 1.0 / ls[0], 1.0 / ls[1])
    lse = jnp.where(head_a, ms[0] + jnp.log2(ls[0]), ms[1] + jnp.log2(ls[1]))
    return o, lse


def _dil_partial_kernel(q_ref, kp_ref, kc_ref, vp_ref, vc_ref, o_ref, lse_ref, *, r):
    valid = _dil_band(pl.program_id(1) == 0)
    npair = DIL_WD // LANES
    for g0 in range(0, r, DIL_GROUP):
        rhos = range(g0, min(g0 + DIL_GROUP, r))
        items = []
        for rho in rhos:
            for p in range(npair):
                cols = slice(rho * DIL_WD + p * LANES, rho * DIL_WD + (p + 1) * LANES)
                items.append((q_ref[0, :, cols],
                              jnp.concatenate([kp_ref[0, :, cols], kc_ref[0, :, cols]], axis=0),
                              jnp.concatenate([vp_ref[0, :, cols], vc_ref[0, :, cols]], axis=0)))
        res = _dil_attend(items, lambda idx: valid)
        for gi, rho in enumerate(rhos):
            rows = pl.ds(rho, DIL_W, stride=r)
            for p in range(npair):
                o, lse = _dil_normalise(*res[gi * npair + p])
                o_ref[0, p, rows, :] = o
                lse_ref[0, p, rows, :] = lse


def _dil_final_kernel(q_ref, k_ref, kh_ref, v_ref, vh_ref, o4_ref, lse4_ref, o16_ref, lse16_ref, o_ref, *, nblk):
    w = DIL_W
    npair = DIL_WD // LANES
    band_first = _dil_band(pl.program_id(1) == 0)
    band = _dil_band(False)

    for g0 in range(0, nblk, DIL_GROUP):
        blocks = range(g0, min(g0 + DIL_GROUP, nblk))
        items = []
        for i in blocks:
            rows = slice(i * w, (i + 1) * w)
            for p in range(npair):
                cols = slice(p * LANES, (p + 1) * LANES)
                if i == 0:
                    kcat = jnp.concatenate([kh_ref[0, :, cols], k_ref[0, rows, cols]], axis=0)
                    vcat = jnp.concatenate([vh_ref[0, :, cols], v_ref[0, rows, cols]], axis=0)
                else:
                    kcat = k_ref[0, (i - 1) * w:(i + 1) * w, cols]
                    vcat = v_ref[0, (i - 1) * w:(i + 1) * w, cols]
                items.append((q_ref[0, rows, cols], kcat, vcat))
        res = _dil_attend(items, lambda idx: band_first if blocks[idx // npair] == 0 else band)
        for gi, i in enumerate(blocks):
            rows = slice(i * w, (i + 1) * w)
            for p in range(npair):
                o1, lse1 = _dil_normalise(*res[gi * npair + p])
                lse4 = lse4_ref[0, p, rows, :]
                lse16 = lse16_ref[0, p, rows, :]
                top = jnp.maximum(jnp.maximum(lse1, lse4), lse16)
                w1, w4, w16 = jnp.exp2(lse1 - top), jnp.exp2(lse4 - top), jnp.exp2(lse16 - top)
                num = w1 * o1 + w4 * o4_ref[0, p, rows, :] + w16 * o16_ref[0, p, rows, :]
                o_ref[0, rows, p * LANES:(p + 1) * LANES] = (num / (w1 + w4 + w16)).astype(BF16)


def _dil_call(views, nblk=4):
    (q1, k1, v1) = views[0]
    b, s, w = q1.shape
    npair = w // LANES
    partial = []
    for r, (q, k, v) in zip(DIL_DILATIONS[1:], views[1:]):
        cur = pl.BlockSpec((1, DIL_W, r * w), lambda bb, n: (bb, n, 0))
        prev = pl.BlockSpec((1, DIL_W, r * w), lambda bb, n: (bb, jnp.maximum(n - 1, 0), 0))
        partial += pl.pallas_call(
            functools.partial(_dil_partial_kernel, r=r),
            out_shape=[jax.ShapeDtypeStruct((b, npair, s, LANES), F32)] * 2,
            grid=(b, s // (r * DIL_W)),
            in_specs=[cur, prev, cur, prev, cur],
            out_specs=[pl.BlockSpec((1, npair, r * DIL_W, LANES), lambda bb, n: (bb, 0, n, 0))] * 2,
            compiler_params=_cparams(("parallel", "arbitrary")),
            name=f"dilated_r{r}",
        )(q, k, k, v, v)
    tm = nblk * DIL_W
    tok = lambda width: pl.BlockSpec((1, tm, width), lambda bb, n: (bb, n, 0))
    halo = pl.BlockSpec((1, DIL_W, w), lambda bb, n: (bb, jnp.maximum(n * nblk - 1, 0), 0))
    accs = pl.BlockSpec((1, npair, tm, LANES), lambda bb, n: (bb, 0, n, 0))
    return pl.pallas_call(
        functools.partial(_dil_final_kernel, nblk=nblk),
        out_shape=jax.ShapeDtypeStruct((b, s, w), BF16),
        grid=(b, s // tm),
        in_specs=[tok(w), tok(w), halo, tok(w), halo, accs, accs, accs, accs],
        out_specs=tok(w),
        compiler_params=_cparams(("parallel", "arbitrary")),
        name="dilated_r1_merge",
    )(q1, k1, k1, v1, v1, *partial)


def _mla_kernel(q_ref, k_ref, v_ref, o_ref, m_ref, l_ref, acc_ref, *, t, sub):
    qi = pl.program_id(2)
    ki = pl.program_id(3)
    head_a = _lane_ids((1, LANES)) < MLA_V
    nsub = t // sub

    @pl.when(ki == 0)
    def _():
        m_ref[...] = jnp.full_like(m_ref, NEG_INF)
        l_ref[...] = jnp.zeros_like(l_ref)
        acc_ref[...] = jnp.zeros_like(acc_ref)

    def v_stack(c):
        v = v_ref[0, c * sub:(c + 1) * sub]
        zv = jnp.zeros_like(v)
        return jnp.concatenate([jnp.where(head_a, v, zv), jnp.where(head_a, zv, v)], axis=0)

    def column(c, rs, diag_r):
        vs = v_stack(c)
        scores = {}
        for r in rs:
            rows = slice(r * sub, (r + 1) * sub)
            for j in range(2):
                q = q_ref[0, rows, j * LANES:(j + 1) * LANES]
                k = k_ref[0, c * sub:(c + 1) * sub, j * LANES:(j + 1) * LANES]
                s = _dot_nt(q, k)
                if r == diag_r:
                    row = lax.broadcasted_iota(jnp.int32, (sub, sub), 0)
                    colk = lax.broadcasted_iota(jnp.int32, (sub, sub), 1)
                    s = jnp.where(colk <= row, s, NEG_INF)
                scores[(r, j)] = s
        probs = {}
        for r in rs:
            rows = slice(r * sub, (r + 1) * sub)
            ps, alphas = [], []
            for j in range(2):
                s = scores[(r, j)]
                m_old = m_ref[j, rows]
                m_new = jnp.maximum(m_old, jnp.max(s, axis=-1, keepdims=True))
                alpha = jnp.exp2(m_old - m_new)
                pe = jnp.exp2(s - jnp.concatenate([m_new] * (sub // LANES), axis=1))
                l_ref[j, rows] = alpha * l_ref[j, rows] + jnp.sum(pe, axis=-1, keepdims=True)
                m_ref[j, rows] = m_new
                ps.append(pe.astype(BF16))
                alphas.append(alpha)
            probs[r] = (jnp.concatenate(ps, axis=1), jnp.where(head_a, alphas[0], alphas[1]))
        for r in rs:
            rows = slice(r * sub, (r + 1) * sub)
            pcat, alpha = probs[r]
            acc_ref[rows] = acc_ref[rows] * alpha + _dot(pcat, vs)

    @pl.when(ki < qi)
    def _():
        for c in range(nsub):
            column(c, range(nsub), None)

    @pl.when(ki == qi)
    def _():
        for c in range(nsub):
            column(c, range(c, nsub), c)
        o_ref[0] = (acc_ref[...] / jnp.where(head_a, l_ref[0], l_ref[1])).astype(BF16)


def _mla_call(mq, mk, mv, t=1024, sub=512):
    b, s, _ = mq.shape
    n = s // t
    return pl.pallas_call(
        functools.partial(_mla_kernel, t=t, sub=sub),
        out_shape=jax.ShapeDtypeStruct((b, s, MLA_VW), BF16),
        grid=(b, MLA_HEADS // 2, n, n),
        in_specs=[pl.BlockSpec((1, t, 2 * LANES), lambda bb, p, i, j: (bb, i, p)),
                  pl.BlockSpec((1, t, 2 * LANES), lambda bb, p, i, j: (bb, jnp.minimum(j, i), p)),
                  pl.BlockSpec((1, t, LANES), lambda bb, p, i, j: (bb, jnp.minimum(j, i), p))],
        out_specs=pl.BlockSpec((1, t, LANES), lambda bb, p, i, j: (bb, i, p)),
        scratch_shapes=[pltpu.VMEM((2, t, LANES), F32), pltpu.VMEM((2, t, LANES), F32),
                        pltpu.VMEM((t, LANES), F32)],
        compiler_params=_cparams(("parallel", "parallel", "parallel", "arbitrary")),
        name="mla_attention",
    )(mq, mk, mv)


def _outproj_kernel(ya_ref, yb_ref, yc_ref, wa_ref, wb_ref, wc_ref, x_ref, g_ref, lg_ref, lb_ref, o_ref):
    y = _dot(ya_ref[0], wa_ref[...]) + _dot(yb_ref[0], wb_ref[...]) + _dot(yc_ref[0], wc_ref[...])
    v = ALPHA * x_ref[0] + (1.0 + g_ref[0]) * y
    o_ref[0] = _layer_norm(v, lg_ref[...], lb_ref[...])


def _outproj_call(ya, yb, yc, wa, wb, wc, x, g1, lg, lb, tm=512):
    b, s, d = x.shape
    tok = lambda w: pl.BlockSpec((1, tm, w), lambda i, j: (i, j, 0))
    per_b = pl.BlockSpec((1, 1, d), lambda i, j: (i, 0, 0))
    full = lambda a: pl.BlockSpec(a.shape, lambda i, j: (0,) * a.ndim)
    return pl.pallas_call(
        _outproj_kernel,
        out_shape=jax.ShapeDtypeStruct((b, s, d), F32),
        grid=(b, s // tm),
        in_specs=[tok(ya.shape[-1]), tok(yb.shape[-1]), tok(yc.shape[-1]), full(wa), full(wb), full(wc),
                  tok(d), per_b, full(lg), full(lb)],
        out_specs=tok(d),
        compiler_params=_cparams(("parallel", "parallel")),
        name="out_proj_ln",
    )(ya, yb, yc, wa, wb, wc, x, g1, lg, lb)


def _swiglu(h, w1_ref, w3_ref, w2_ref, chunk):
    ff = w1_ref.shape[1]
    y = None
    for a in range(0, ff, chunk):
        b = min(a + chunk, ff)
        mid = (_silu(_dot(h, w1_ref[:, a:b])) * _dot(h, w3_ref[:, a:b])).astype(BF16)
        part = _dot(mid, w2_ref[a:b, :])
        y = part if y is None else y + part
    return y


def _ffn_kernel(x_ref, sc_ref, sh_ref, g_ref, w1_ref, w3_ref, w2_ref, lg_ref, lb_ref, o_ref, *, chunk):
    x = x_ref[...]
    h = (x * (1.0 + sc_ref[0]) + sh_ref[0]).astype(BF16)
    y = _swiglu(h, w1_ref, w3_ref, w2_ref, chunk)
    o_ref[...] = _layer_norm(ALPHA * x + (1.0 + g_ref[0]) * y, lg_ref[...], lb_ref[...])


def _ffn_call(x2d, sc, sh, g2, w1, w3, w2, lg, lb, tiles_per_batch, tm, chunk):
    t, d = x2d.shape
    per_b = pl.BlockSpec((1, 1, d), lambda i: (i // tiles_per_batch, 0, 0))
    vec = pl.BlockSpec((1, d), lambda i: (0, 0))
    resident = lambda a: pl.BlockSpec(a.shape, lambda i: (0, 0), pipeline_mode=pl.Buffered(1))
    return pl.pallas_call(
        functools.partial(_ffn_kernel, chunk=chunk),
        out_shape=jax.ShapeDtypeStruct((t, d), F32),
        grid=(t // tm,),
        in_specs=[pl.BlockSpec((tm, d), lambda i: (i, 0)), per_b, per_b, per_b,
                  resident(w1), resident(w3), resident(w2), vec, vec],
        out_specs=pl.BlockSpec((tm, d), lambda i: (i, 0)),
        compiler_params=_cparams(("parallel",), FFN_VMEM_LIMIT),
        name="ffn_dense",
    )(x2d, sc, sh, g2, w1, w3, w2, lg, lb)


ROUTE_G_OFF = 2
ROUTE_RANK_OFF = 4


def _router_kernel(x_ref, sc_ref, sh_ref, wr_ref, tri_ref, h_ref, r_ref, cnt_ref, run_ref):
    @pl.when(pl.program_id(0) == 0)
    def _():
        run_ref[...] = jnp.zeros_like(run_ref)

    h = x_ref[...] * (1.0 + sc_ref[0]) + sh_ref[0]
    h_ref[...] = h.astype(BF16)
    logits = jnp.dot(h, wr_ref[...], preferred_element_type=F32, precision=lax.Precision.HIGHEST)
    lane = _lane_ids(logits.shape)
    lg = jnp.where(lane < N_EXPERTS, logits, NEG_INF)
    m1 = jnp.max(lg, axis=-1, keepdims=True)
    i1 = jnp.min(jnp.where(lg == m1, lane, LANES), axis=-1, keepdims=True)
    lg2 = jnp.where(lane == i1, NEG_INF, lg)
    m2 = jnp.max(lg2, axis=-1, keepdims=True)
    i2 = jnp.min(jnp.where(lg2 == m2, lane, LANES), axis=-1, keepdims=True)
    e2 = jnp.exp(m2 - m1)
    den = 1.0 + e2
    chosen = (lane == i1) | (lane == i2)
    cum = _dot(tri_ref[...], chosen.astype(BF16)) + run_ref[...]
    rank1 = jnp.sum(jnp.where(lane == i1, cum, 0.0), axis=-1, keepdims=True) - 1.0
    rank2 = jnp.sum(jnp.where(lane == i2, cum, 0.0), axis=-1, keepdims=True) - 1.0
    run_ref[...] = cum[cum.shape[0] - 1:, :]
    cnt_ref[...] = cum[cum.shape[0] - 1:, :]
    out = jnp.where(lane == 0, i1.astype(F32), 0.0)
    out = jnp.where(lane == 1, i2.astype(F32), out)
    out = jnp.where(lane == ROUTE_G_OFF, 1.0 / den, out)
    out = jnp.where(lane == ROUTE_G_OFF + 1, e2 / den, out)
    out = jnp.where(lane == ROUTE_RANK_OFF, rank1, out)
    out = jnp.where(lane == ROUTE_RANK_OFF + 1, rank2, out)
    r_ref[...] = out


def _router_call(x2d, sc, sh, wr_p, tiles_per_batch, tm):
    t, d = x2d.shape
    per_b = pl.BlockSpec((1, 1, d), lambda i: (i // tiles_per_batch, 0, 0))
    tri = (jnp.arange(tm)[:, None] >= jnp.arange(tm)[None, :]).astype(BF16)
    return pl.pallas_call(
        _router_kernel,
        out_shape=[jax.ShapeDtypeStruct((t, d), BF16), jax.ShapeDtypeStruct((t, LANES), F32),
                   jax.ShapeDtypeStruct((1, LANES), F32)],
        grid=(t // tm,),
        in_specs=[pl.BlockSpec((tm, d), lambda i: (i, 0)), per_b, per_b,
                  pl.BlockSpec(wr_p.shape, lambda i: (0, 0)), pl.BlockSpec((tm, tm), lambda i: (0, 0))],
        out_specs=[pl.BlockSpec((tm, d), lambda i: (i, 0)), pl.BlockSpec((tm, LANES), lambda i: (i, 0)),
                   pl.BlockSpec((1, LANES), lambda i: (0, 0))],
        scratch_shapes=[pltpu.VMEM((1, LANES), F32)],
        compiler_params=_cparams(("arbitrary",)),
        name="moe_router",
    )(x2d, sc, sh, wr_p, tri)


def _expert_kernel(te_ref, nu_ref, x_ref, w1_ref, w3_ref, w2_ref, o_ref, *, chunk):
    i = pl.program_id(0)

    @pl.when(i < nu_ref[0])
    def _():
        o_ref[...] = _swiglu(x_ref[...], w1_ref.at[0], w3_ref.at[0], w2_ref.at[0], chunk).astype(o_ref.dtype)

    @pl.when(i >= nu_ref[0])
    def _():
        o_ref[...] = jnp.zeros_like(o_ref)


def _expert_call(tile_expert, n_used, xs, w1, w3, w2, tm, chunk):
    p, d = xs.shape
    ff = w1.shape[2]
    wspec = lambda shape: pl.BlockSpec(shape, lambda i, te, nu: (te[i], 0, 0), pipeline_mode=pl.Buffered(1))
    return pl.pallas_call(
        functools.partial(_expert_kernel, chunk=chunk),
        out_shape=jax.ShapeDtypeStruct((p, d), BF16),
        grid_spec=pltpu.PrefetchScalarGridSpec(
            num_scalar_prefetch=2,
            grid=(p // tm,),
            in_specs=[pl.BlockSpec((tm, d), lambda i, te, nu: (i, 0)),
                      wspec((1, d, ff)), wspec((1, d, ff)), wspec((1, ff, d))],
            out_specs=pl.BlockSpec((tm, d), lambda i, te, nu: (i, 0))),
        compiler_params=_cparams(("arbitrary",), FFN_VMEM_LIMIT),
        name="moe_experts",
    )(tile_expert, n_used, xs, w1, w3, w2)


def _combine_kernel(x_ref, ya_ref, yb_ref, r_ref, g_ref, lg_ref, lb_ref, o_ref):
    r = r_ref[...]
    lane = _lane_ids(r.shape)
    ga = jnp.sum(jnp.where(lane == ROUTE_G_OFF, r, 0.0), axis=-1, keepdims=True)
    gb = jnp.sum(jnp.where(lane == ROUTE_G_OFF + 1, r, 0.0), axis=-1, keepdims=True)
    y = ga * ya_ref[...].astype(F32) + gb * yb_ref[...].astype(F32)
    v = ALPHA * x_ref[...] + (1.0 + g_ref[0]) * y
    o_ref[...] = _layer_norm(v, lg_ref[...], lb_ref[...])


def _combine_call(x2d, ya, yb, route, g2, lg, lb, tiles_per_batch, tm):
    t, d = x2d.shape
    tok = pl.BlockSpec((tm, d), lambda i: (i, 0))
    per_b = pl.BlockSpec((1, 1, d), lambda i: (i // tiles_per_batch, 0, 0))
    vec = pl.BlockSpec((1, d), lambda i: (0, 0))
    return pl.pallas_call(
        _combine_kernel,
        out_shape=jax.ShapeDtypeStruct((t, d), F32),
        grid=(t // tm,),
        in_specs=[tok, tok, tok, pl.BlockSpec((tm, LANES), lambda i: (i, 0)), per_b, vec, vec],
        out_specs=tok,
        compiler_params=_cparams(("parallel",)),
        name="moe_combine_ln",
    )(x2d, ya, yb, route, g2, lg, lb)


def _dispatch_plan(route, counts, tm):
    t = route.shape[0]
    counts = counts[0, :N_EXPERTS].astype(jnp.int32)
    padded = ((counts + tm - 1) // tm) * tm
    ends = jnp.cumsum(padded)
    starts = ends - padded
    e = route[:, 0:2].astype(jnp.int32)
    rank = route[:, ROUTE_RANK_OFF:ROUTE_RANK_OFF + 2].astype(jnp.int32)
    dest = starts[e] + rank
    n_rows = 2 * t + N_EXPERTS * tm
    tok = jnp.arange(t, dtype=jnp.int32)
    row_token = jnp.zeros((n_rows,), jnp.int32).at[dest[:, 0]].set(tok).at[dest[:, 1]].set(tok)
    tile_start = jnp.arange(n_rows // tm, dtype=jnp.int32) * tm
    tile_expert = jnp.minimum(jnp.sum((tile_start[:, None] >= ends[None, :]).astype(jnp.int32), axis=1),
                              N_EXPERTS - 1)
    n_used = (ends[-1] // tm).astype(jnp.int32).reshape(1)
    return row_token, dest, tile_expert, n_used


def _moe_layer(x2d, sc, sh, g2, wr_p, w1, w3, w2, lg, lb, tiles_per_batch, tm, tme, tf):
    h, route, counts = _router_call(x2d, sc, sh, wr_p, tiles_per_batch, tm)
    row_token, dest, tile_expert, n_used = _dispatch_plan(route, counts, tme)
    xs = jnp.take(h, row_token, axis=0)
    ys = _expert_call(tile_expert, n_used, xs, w1, w3, w2, tme, tf)
    ya = jnp.take(ys, dest[:, 0], axis=0)
    yb = jnp.take(ys, dest[:, 1], axis=0)
    return _combine_call(x2d, ya, yb, route, g2, lg, lb, tiles_per_batch, tm)


def _trig_kernel(a_ref, c_ref, s_ref):
    a = a_ref[...]
    c_ref[...] = jnp.cos(a)
    s_ref[...] = jnp.sin(a)


def _trig_call(ang2d):
    n = ang2d.shape[0]
    tr = math.gcd(n, 1024)
    spec = pl.BlockSpec((tr, LANES), lambda i: (i, 0))
    return pl.pallas_call(
        _trig_kernel,
        out_shape=[jax.ShapeDtypeStruct(ang2d.shape, F32)] * 2,
        grid=(n // tr,),
        in_specs=[spec],
        out_specs=[spec, spec],
        compiler_params=_cparams(("parallel",)),
        name="rope_trig",
    )(ang2d)


def _rope_tables(positions):
    pos = positions.astype(F32)[..., None]
    lane = jnp.arange(LANES)
    inv64 = ROPE_THETA ** (-jnp.arange(0, RET_D, 2, dtype=F32) / RET_D)
    invm = ROPE_THETA ** (-jnp.arange(0, MLA_ROPE, 2, dtype=F32) / MLA_ROPE)
    n64 = inv64.shape[0]
    ang = pos * jnp.concatenate([inv64, invm])
    cos, sin = _trig_call(ang.reshape(-1, LANES))
    cos, sin = cos.reshape(ang.shape), sin.reshape(ang.shape)
    cos64, sin64, cosm, sinm = cos[..., :n64], sin[..., :n64], cos[..., n64:], sin[..., n64:]
    sign64 = jnp.where((lane % 64) < 32, -1.0, 1.0)
    c64 = jnp.tile(cos64, (1, 1, LANES // 32))
    s64 = jnp.tile(sin64, (1, 1, LANES // 32)) * sign64
    in_rope = (lane >= KR_LANE) & (lane < KR_LANE + MLA_ROPE)
    signm = jnp.where(lane < KR_LANE + MLA_ROPE // 2, -1.0, 1.0)
    cm = jnp.where(in_rope, jnp.tile(cosm, (1, 1, LANES // 16)), 1.0)
    sm = jnp.where(in_rope, jnp.tile(sinm, (1, 1, LANES // 16)) * signm, 0.0)
    return c64, s64, cm, sm


def _prep_mixer_weights(w_in, w_uq, w_ukv):
    d = w_in.shape[0]
    kr_cols = jnp.zeros((d, LANES), w_in.dtype).at[:, KR_LANE:KR_LANE + MLA_ROPE].set(
        w_in[:, MLA_OFF + MLA_Q_RANK + MLA_KV_RANK:])
    w_in_p = jnp.concatenate([w_in[:, :MLA_OFF + MLA_Q_RANK + MLA_KV_RANK], kr_cols], axis=1).astype(BF16)
    uq = w_uq.reshape(MLA_Q_RANK, MLA_HEADS, MLA_NOPE + MLA_ROPE)
    wuq_p = jnp.pad(uq, ((0, 0), (0, 0), (0, LANES - MLA_NOPE - MLA_ROPE))).reshape(MLA_Q_RANK, MLA_PAD).astype(BF16)
    ukv = w_ukv.reshape(MLA_KV_RANK, MLA_HEADS, MLA_NOPE + MLA_V)
    wkn_p = jnp.pad(ukv[:, :, :MLA_NOPE], ((0, 0), (0, 0), (0, LANES - MLA_NOPE))).reshape(MLA_KV_RANK, MLA_PAD).astype(BF16)
    wv = ukv[:, :, MLA_NOPE:].reshape(MLA_KV_RANK, MLA_VW).astype(BF16)
    return w_in_p, wuq_p, wkn_p, wv


def kernel(x, c, positions, w_in, ret_gn_g, mla_qn_g, mla_kvn_g, w_uq, w_ukv, w_out, w_ada, b_ada, ln1_g, ln1_b, ln2_g, ln2_b, w1_dense, w3_dense, w2_dense, w_router, w1_moe, w3_moe, w2_moe):
    b, s, d = x.shape
    tabs = _rope_tables(positions)
    ret_tabs = _ret_tables()
    mod = _ada_call(c, w_ada, b_ada)
    tm_tok = 512
    tiles_per_batch = s // tm_tok
    for l in range(w_in.shape[0]):
        sh1, sc1, g1, sh2, sc2, g2 = [mod[l, :, j * d:(j + 1) * d].reshape(b, 1, d) for j in range(6)]
        w_in_p, wuq_p, wkn_p, wv = _prep_mixer_weights(w_in[l], w_uq[l], w_ukv[l])
        (rq, rk, rv, rg), dil_views, (mq, mk, mv) = _inproj_call(
            x, sc1, sh1, w_in_p, tabs, mla_qn_g[l].reshape(1, -1), mla_kvn_g[l].reshape(1, -1), wuq_p, wkn_p, wv)
        ya = _ret_call(rq, rk, rv, rg, ret_gn_g[l].reshape(1, -1), ret_tabs)
        yb = _dil_call(dil_views)
        yc = _mla_call(mq, mk, mv)
        wo = w_out[l].astype(BF16)
        x = _outproj_call(ya, yb, yc, wo[:RET_W], wo[RET_W:RET_W + DIL_WD], wo[RET_W + DIL_WD:], x, g1,
                          ln1_g[l].reshape(1, d), ln1_b[l].reshape(1, d))
        x2d = x.reshape(b * s, d)
        lg, lb = ln2_g[l].reshape(1, d), ln2_b[l].reshape(1, d)
        if l % 2 == 0:
            j = l // 2
            x2d = _ffn_call(x2d, sc2, sh2, g2, w1_dense[j].astype(BF16), w3_dense[j].astype(BF16),
                            w2_dense[j].astype(BF16), lg, lb, tiles_per_batch, tm_tok, FF_CHUNK)
        else:
            j = l // 2
            wr_p = jnp.pad(w_router[j], ((0, 0), (0, LANES - N_EXPERTS)))
            x2d = _moe_layer(x2d, sc2, sh2, g2, wr_p, w1_moe[j].astype(BF16), w3_moe[j].astype(BF16),
                             w2_moe[j].astype(BF16), lg, lb, tiles_per_batch, tm_tok, 512, FF_CHUNK)
        x = x2d.reshape(b, s, d)
    return x
```

```python
import functools
import math

import jax
import jax.numpy as jnp
from jax import lax
from jax.experimental import pallas as pl
from jax.experimental.pallas import tpu as pltpu

D_MODEL = 1024
DEPTH = 4
RET_HEADS = 4
RET_D = 64
RET_CHUNK = 128
DIL_HEADS = 6
DIL_D = 64
DIL_DILATIONS = (1, 4, 16)
DIL_W = 128
MLA_HEADS = 6
MLA_Q_RANK = 384
MLA_KV_RANK = 256
MLA_NOPE = 64
MLA_ROPE = 32
MLA_V = 64
N_EXPERTS = 8
ROPE_THETA = 10000.0
LN_EPS = 1e-5
RMS_EPS = 1e-6
ALPHA = (2.0 * DEPTH) ** 0.25

LANES = 128
RET_W = RET_HEADS * RET_D
DIL_WD = DIL_HEADS * DIL_D
MLA_PAD = MLA_HEADS * LANES
MLA_VW = MLA_HEADS * MLA_V
RET_IN = 4 * RET_W
DIL_IN = 3 * DIL_WD
MLA_OFF = RET_IN + DIL_IN
D_IN_PAD = MLA_OFF + MLA_Q_RANK + MLA_KV_RANK + LANES
KR_LANE = MLA_NOPE

VMEM_LIMIT = 48 * 1024 * 1024
FFN_VMEM_LIMIT = 56 * 1024 * 1024
FF_CHUNK = 512
EXPERT_TILE = 512
CAST_BLOCK_BYTES = 4 * 1024 * 1024
BF16 = jnp.bfloat16
F32 = jnp.float32
NEG_INF = float("-inf")
LOG2E = math.log2(math.e)
MLA_Q_SCALE = (MLA_NOPE + MLA_ROPE) ** -0.5 * LOG2E


def _cparams(sem, vmem_limit=VMEM_LIMIT):
    return pltpu.CompilerParams(dimension_semantics=sem, vmem_limit_bytes=vmem_limit)


def _dot(a, b):
    return jnp.dot(a, b, preferred_element_type=F32)


def _dot_nt(a, b):
    return lax.dot_general(a, b, (((1,), (1,)), ((), ())), preferred_element_type=F32)


def _dot_tn(a, b):
    return lax.dot_general(a, b, (((0,), (0,)), ((), ())), preferred_element_type=F32)


def _silu(x):
    return x * (1.0 / (1.0 + jnp.exp(-x)))


def _layer_norm(v, g, b):
    mu = jnp.mean(v, axis=-1, keepdims=True)
    d = v - mu
    var = jnp.mean(d * d, axis=-1, keepdims=True)
    return d * lax.rsqrt(var + LN_EPS) * g + b


def _lane_ids(shape):
    return lax.broadcasted_iota(jnp.int32, shape, len(shape) - 1)


def _rope_group(x, cos, sin_signed, first_half, half):
    fwd = pltpu.roll(x, LANES - half, 1)
    bwd = pltpu.roll(x, half, 1)
    return x * cos + jnp.where(first_half, fwd, bwd) * sin_signed


def _cast_kernel(w_ref, o_ref):
    o_ref[...] = w_ref[...].astype(BF16)


def _cast_call(w):
    cols = w.shape[-1]
    w2d = w.reshape(-1, cols)
    rows = w2d.shape[0]
    pack = 16
    tr = max(pack, CAST_BLOCK_BYTES // (4 * cols) // pack * pack)
    while rows % tr:
        tr -= pack
    spec = pl.BlockSpec((tr, cols), lambda i: (i, 0))
    out = pl.pallas_call(
        _cast_kernel,
        out_shape=jax.ShapeDtypeStruct(w2d.shape, BF16),
        grid=(rows // tr,),
        in_specs=[spec],
        out_specs=spec,
        compiler_params=_cparams(("parallel",)),
        name="cast_bf16",
    )(w2d)
    return out.reshape(w.shape)


def _ada_kernel(c_ref, w_ref, b_ref, o_ref):
    cond = _silu(c_ref[...])
    o_ref[0] = jnp.dot(cond, w_ref[0], preferred_element_type=F32,
                       precision=lax.Precision.HIGHEST) + b_ref[0]


def _ada_call(c, w_ada, b_ada):
    nl, d, n = w_ada.shape
    b = c.shape[0]
    tn = 1536
    return pl.pallas_call(
        _ada_kernel,
        out_shape=jax.ShapeDtypeStruct((nl, b, n), F32),
        grid=(nl, n // tn),
        in_specs=[pl.BlockSpec((b, d), lambda l, j: (0, 0)),
                  pl.BlockSpec((1, d, tn), lambda l, j: (l, 0, j)),
                  pl.BlockSpec((1, 1, tn), lambda l, j: (l, 0, j))],
        out_specs=pl.BlockSpec((1, b, tn), lambda l, j: (l, 0, j)),
        compiler_params=_cparams(("arbitrary", "arbitrary")),
        name="ada_mod",
    )(c, w_ada, b_ada.reshape(nl, 1, n))


def _inproj_kernel(x_ref, sc_ref, sh_ref, w_ref, c64_ref, s64_ref, cm_ref, sm_ref,
                   qg_ref, kvg_ref, wuq_ref, wkn_ref, wv_ref,
                   rq_ref, rk_ref, rv_ref, rg_ref,
                   dq1_ref, dk1_ref, dv1_ref, dq4_ref, dk4_ref, dv4_ref, dq16_ref, dk16_ref, dv16_ref,
                   mq_ref, mk_ref, mv_ref, scr_ref, *, tm):
    h = (x_ref[0] * (1.0 + sc_ref[0]) + sh_ref[0]).astype(BF16)
    c64 = c64_ref[0]
    s64 = s64_ref[0]
    cm = cm_ref[0]
    sm = sm_ref[0]
    lane = _lane_ids((1, LANES))
    first64 = (lane % 64) < 32
    firstm = lane < (KR_LANE + MLA_ROPE // 2)

    def rope64(zc):
        return _rope_group(zc, c64, s64, first64, 32)

    def ropem(zc):
        return _rope_group(zc, cm, sm, firstm, MLA_ROPE // 2)

    za = _dot(h, w_ref[:, 0:RET_IN])
    for j in range(RET_W // LANES):
        sl = slice(j * LANES, (j + 1) * LANES)
        rq_ref[0, :, sl] = rope64(za[:, j * LANES:(j + 1) * LANES]).astype(BF16)
        kc = za[:, RET_W + j * LANES:RET_W + (j + 1) * LANES]
        rk_ref[0, :, sl] = (rope64(kc) * (RET_D ** -0.5)).astype(BF16)
    rv_ref[0] = za[:, 2 * RET_W:3 * RET_W].astype(BF16)
    rg_ref[0] = za[:, 3 * RET_W:4 * RET_W]

    zb = _dot(h, w_ref[:, RET_IN:MLA_OFF])
    q_scale = (DIL_D ** -0.5) * LOG2E
    views = ((dq1_ref, dq4_ref, dq16_ref), (dk1_ref, dk4_ref, dk16_ref), (dv1_ref, dv4_ref, dv16_ref))
    for a, (n1, n4, n16) in enumerate(views):
        for j in range(DIL_WD // LANES):
            zc = zb[:, a * DIL_WD + j * LANES:a * DIL_WD + (j + 1) * LANES]
            if a == 0:
                zc = rope64(zc) * q_scale
            elif a == 1:
                zc = rope64(zc)
            scr_ref[j] = zc
            n1[0, :, j * LANES:(j + 1) * LANES] = zc.astype(BF16)
        for r, ref in ((4, n4), (16, n16)):
            for rho in range(r):
                for j in range(DIL_WD // LANES):
                    ref[0, :, rho * DIL_WD + j * LANES:rho * DIL_WD + (j + 1) * LANES] = (
                        scr_ref[j, pl.ds(rho, tm // r, stride=r), :].astype(BF16))

    zc = _dot(h, w_ref[:, MLA_OFF:D_IN_PAD])
    cq = zc[:, 0:MLA_Q_RANK]
    ckv = zc[:, MLA_Q_RANK:MLA_Q_RANK + MLA_KV_RANK]
    kr = ropem(zc[:, MLA_Q_RANK + MLA_KV_RANK:])
    cqn = (cq * lax.rsqrt(jnp.mean(cq * cq, axis=-1, keepdims=True) + RMS_EPS) * qg_ref[...]).astype(BF16)
    ckvn = (ckv * lax.rsqrt(jnp.mean(ckv * ckv, axis=-1, keepdims=True) + RMS_EPS) * kvg_ref[...]).astype(BF16)
    q = _dot(cqn, wuq_ref[...])
    kn = _dot(ckvn, wkn_ref[...])
    for hh in range(MLA_HEADS):
        sl = slice(hh * LANES, (hh + 1) * LANES)
        mq_ref[0, :, sl] = (ropem(q[:, hh * LANES:(hh + 1) * LANES]) * MLA_Q_SCALE).astype(BF16)
        mk_ref[0, :, sl] = (kn[:, hh * LANES:(hh + 1) * LANES] + kr).astype(BF16)
    mv_ref[0] = _dot(ckvn, wv_ref[...]).astype(BF16)


def _inproj_call(x, sc, sh, w_in_p, tabs, qg, kvg, wuq_p, wkn_p, wv, tm=512):
    b, s, d = x.shape
    c64, s64, cm, sm = tabs
    tok = lambda w: pl.BlockSpec((1, tm, w), lambda i, j: (i, j, 0))
    per_b = pl.BlockSpec((1, 1, d), lambda i, j: (i, 0, 0))
    full = lambda a: pl.BlockSpec(a.shape, lambda i, j: (0,) * a.ndim)
    outs = [(1, RET_W, BF16), (1, RET_W, BF16), (1, RET_W, BF16), (1, RET_W, F32)]
    outs += [(r, DIL_WD, BF16) for r in DIL_DILATIONS for _ in range(3)]
    outs += [(1, MLA_PAD, BF16), (1, MLA_PAD, BF16), (1, MLA_VW, BF16)]
    res = pl.pallas_call(
        functools.partial(_inproj_kernel, tm=tm),
        out_shape=[jax.ShapeDtypeStruct((b, s // r, r * w), dt) for r, w, dt in outs],
        grid=(b, s // tm),
        in_specs=[tok(d), per_b, per_b, full(w_in_p), tok(LANES), tok(LANES), tok(LANES), tok(LANES),
                  full(qg), full(kvg), full(wuq_p), full(wkn_p), full(wv)],
        out_specs=[pl.BlockSpec((1, tm // r, r * w), lambda i, j: (i, j, 0)) for r, w, _ in outs],
        scratch_shapes=[pltpu.VMEM((DIL_WD // LANES, tm, LANES), F32)],
        compiler_params=_cparams(("parallel", "parallel")),
        name="in_proj",
    )(x, sc, sh, w_in_p, c64, s64, cm, sm, qg, kvg, wuq_p, wkn_p, wv)
    ret = res[0:4]
    dil = [res[4 + 3 * i:7 + 3 * i] for i in range(len(DIL_DILATIONS))]
    mla = res[4 + 3 * len(DIL_DILATIONS):]
    return ret, dil, mla


def _ret_kernel(q_ref, k_ref, v_ref, g_ref, gn_ref, intra_ref, xi_ref, zeta_ref, dmat_ref, bmask_ref,
                o_ref, state_ref, *, chunks):
    @pl.when(pl.program_id(1) == 0)
    def _():
        state_ref[...] = jnp.zeros_like(state_ref)

    lane = _lane_ids((1, LANES))
    head_a = lane < RET_D
    c = RET_CHUNK
    npair = RET_W // LANES
    items = [(ci, p) for ci in range(chunks) for p in range(npair)]
    view = lambda ref, ci, p: ref[0, ci * c:(ci + 1) * c, p * LANES:(p + 1) * LANES]

    inner, kv_inc = {}, {}
    for ci, p in items:
        cols = slice(p * LANES, (p + 1) * LANES)
        q, k, v = view(q_ref, ci, p), view(k_ref, ci, p), view(v_ref, ci, p)
        zero = jnp.zeros_like(q)
        s_a = _dot_nt(jnp.where(head_a, q, zero), k) * intra_ref[2 * p]
        s_b = _dot_nt(jnp.where(head_a, zero, q), k) * intra_ref[2 * p + 1]
        s_cat = jnp.concatenate([s_a, s_b], axis=1).astype(BF16)
        v_stack = jnp.concatenate([jnp.where(head_a, v, zero), jnp.where(head_a, zero, v)], axis=0)
        inner[(ci, p)] = _dot(s_cat, v_stack)
        kz = (k.astype(F32) * zeta_ref[:, cols]).astype(BF16)
        kv_inc[(ci, p)] = _dot_tn(kz, v) * bmask_ref[...]

    outs = {}
    for p in range(npair):
        cols = slice(p * LANES, (p + 1) * LANES)
        state = state_ref[p]
        for ci in range(chunks):
            cross = _dot(view(q_ref, ci, p), state.astype(BF16)) * xi_ref[:, cols]
            outs[(ci, p)] = inner[(ci, p)] + cross
            state = state * dmat_ref[p] + kv_inc[(ci, p)]
        state_ref[p] = state

    inv_n = 1.0 / RET_D
    for ci, p in items:
        cols = slice(p * LANES, (p + 1) * LANES)
        o = outs[(ci, p)]
        mu = jnp.where(head_a,
                       jnp.sum(jnp.where(head_a, o, 0.0), axis=-1, keepdims=True),
                       jnp.sum(jnp.where(head_a, 0.0, o), axis=-1, keepdims=True)) * inv_n
        d = o - mu
        dd = d * d
        var = jnp.where(head_a,
                        jnp.sum(jnp.where(head_a, dd, 0.0), axis=-1, keepdims=True),
                        jnp.sum(jnp.where(head_a, 0.0, dd), axis=-1, keepdims=True)) * inv_n
        on = d * lax.rsqrt(var + LN_EPS) * gn_ref[:, cols]
        o_ref[0, ci * c:(ci + 1) * c, cols] = (_silu(view(g_ref, ci, p)) * on).astype(BF16)


def _ret_tables():
    c = RET_CHUNK
    log_g = jnp.log(1.0 - 2.0 ** (-5.0 - jnp.arange(RET_HEADS, dtype=F32)))
    idx = jnp.arange(c, dtype=F32)
    diff = idx[:, None] - idx[None, :]
    intra = jnp.where(diff >= 0, jnp.exp(jnp.maximum(diff, 0.0) * log_g[:, None, None]), 0.0)
    xi = jnp.exp((idx[:, None] + 1.0) * log_g[None, :])
    zeta = jnp.exp((c - 1.0 - idx[:, None]) * log_g[None, :])
    decay = jnp.exp(c * log_g)
    xi_l = jnp.repeat(xi, RET_D, axis=1)
    zeta_l = jnp.repeat(zeta, RET_D, axis=1)
    r = jnp.arange(LANES)
    same = (r[:, None] // RET_D) == (r[None, :] // RET_D)
    bmask = same.astype(F32)
    dec_l = jnp.repeat(decay, RET_D).reshape(RET_W // LANES, 1, LANES)
    dmat = bmask[None] * dec_l
    return intra, xi_l, zeta_l, dmat, bmask


def _ret_call(rq, rk, rv, rg, gn, tables, chunks=8):
    b, s, w = rq.shape
    tm = chunks * RET_CHUNK
    intra, xi_l, zeta_l, dmat, bmask = tables
    tok = pl.BlockSpec((1, tm, w), lambda i, j: (i, j, 0))
    full = lambda a: pl.BlockSpec(a.shape, lambda i, j: (0,) * a.ndim)
    return pl.pallas_call(
        functools.partial(_ret_kernel, chunks=chunks),
        out_shape=jax.ShapeDtypeStruct((b, s, w), BF16),
        grid=(b, s // tm),
        in_specs=[tok, tok, tok, tok, full(gn), full(intra), full(xi_l), full(zeta_l), full(dmat), full(bmask)],
        out_specs=tok,
        scratch_shapes=[pltpu.VMEM((w // LANES, LANES, LANES), F32)],
        compiler_params=_cparams(("parallel", "arbitrary")),
        name="retention",
    )(rq, rk, rv, rg, gn, intra, xi_l, zeta_l, dmat, bmask)


def _dil_band(first_block):
    w = DIL_W
    qi = lax.broadcasted_iota(jnp.int32, (w, 2 * w), 0)
    kj = lax.broadcasted_iota(jnp.int32, (w, 2 * w), 1)
    dist = w + qi - kj
    band = (dist >= 0) & (dist <= w)
    if first_block is False:
        return band
    return band & (jnp.logical_not(first_block) | (kj >= w))


DIL_GROUP = 2


def _dil_attend(items, valid_of):
    head_a = _lane_ids((1, LANES)) < DIL_D
    scores = []
    for idx, (q, kcat, _) in enumerate(items):
        zero = jnp.zeros_like(q)
        valid = valid_of(idx)
        scores.append([jnp.where(valid, _dot_nt(qh, kcat), NEG_INF)
                       for qh in (jnp.where(head_a, q, zero), jnp.where(head_a, zero, q))])
    soft = []
    for scs in scores:
        ms, ls, ps = [], [], []
        for sc in scs:
            m = jnp.max(sc, axis=-1, keepdims=True)
            pe = jnp.exp2(sc - m)
            ms.append(m)
            ls.append(jnp.sum(pe, axis=-1, keepdims=True))
            ps.append(pe.astype(BF16))
        soft.append((ms, ls, jnp.concatenate(ps, axis=1)))
    out = []
    for (_, _, vcat), (ms, ls, pcat) in zip(items, soft):
        zv = jnp.zeros_like(vcat)
        v_stack = jnp.concatenate([jnp.where(head_a, vcat, zv), jnp.where(head_a, zv, vcat)], axis=0)
        out.append((_dot(pcat, v_stack), ms, ls))
    return out


def _dil_normalise(acc, ms, ls):
    head_a = _lane_ids((1, LANES)) < DIL_D
    o = acc * jnp.where(head_a, 1.0 / ls[0], 1.0 / ls[1])
    lse = jnp.where(head_a, ms[0] + jnp.log2(ls[0]), ms[1] + jnp.log2(ls[1]))
    return o, lse


def _dil_partial_kernel(q_ref, kp_ref, kc_ref, vp_ref, vc_ref, o_ref, lse_ref, *, r):
    valid = _dil_band(pl.program_id(1) == 0)
    npair = DIL_WD // LANES
    for g0 in range(0, r, DIL_GROUP):
        rhos = range(g0, min(g0 + DIL_GROUP, r))
        items = []
        for rho in rhos:
            for p in range(npair):
                cols = slice(rho * DIL_WD + p * LANES, rho * DIL_WD + (p + 1) * LANES)
                items.append((q_ref[0, :, cols],
                              jnp.concatenate([kp_ref[0, :, cols], kc_ref[0, :, cols]], axis=0),
                              jnp.concatenate([vp_ref[0, :, cols], vc_ref[0, :, cols]], axis=0)))
        res = _dil_attend(items, lambda idx: valid)
        for gi, rho in enumerate(rhos):
            rows = pl.ds(rho, DIL_W, stride=r)
            for p in range(npair):
                o, lse = _dil_normalise(*res[gi * npair + p])
                o_ref[0, p, rows, :] = o
                lse_ref[0, p, rows, :] = lse


def _dil_final_kernel(q_ref, k_ref, kh_ref, v_ref, vh_ref, o4_ref, lse4_ref, o16_ref, lse16_ref, o_ref, *, nblk):
    w = DIL_W
    npair = DIL_WD // LANES
    band_first = _dil_band(pl.program_id(1) == 0)
    band = _dil_band(False)

    for g0 in range(0, nblk, DIL_GROUP):
        blocks = range(g0, min(g0 + DIL_GROUP, nblk))
        items = []
        for i in blocks:
            rows = slice(i * w, (i + 1) * w)
            for p in range(npair):
                cols = slice(p * LANES, (p + 1) * LANES)
                if i == 0:
                    kcat = jnp.concatenate([kh_ref[0, :, cols], k_ref[0, rows, cols]], axis=0)
                    vcat = jnp.concatenate([vh_ref[0, :, cols], v_ref[0, rows, cols]], axis=0)
                else:
                    kcat = k_ref[0, (i - 1) * w:(i + 1) * w, cols]
                    vcat = v_ref[0, (i - 1) * w:(i + 1) * w, cols]
                items.append((q_ref[0, rows, cols], kcat, vcat))
        res = _dil_attend(items, lambda idx: band_first if blocks[idx // npair] == 0 else band)
        for gi, i in enumerate(blocks):
            rows = slice(i * w, (i + 1) * w)
            for p in range(npair):
                o1, lse1 = _dil_normalise(*res[gi * npair + p])
                lse4 = lse4_ref[0, p, rows, :]
                lse16 = lse16_ref[0, p, rows, :]
                top = jnp.maximum(jnp.maximum(lse1, lse4), lse16)
                w1, w4, w16 = jnp.exp2(lse1 - top), jnp.exp2(lse4 - top), jnp.exp2(lse16 - top)
                num = w1 * o1 + w4 * o4_ref[0, p, rows, :] + w16 * o16_ref[0, p, rows, :]
                o_ref[0, rows, p * LANES:(p + 1) * LANES] = (num / (w1 + w4 + w16)).astype(BF16)


def _dil_call(views, nblk=4):
    (q1, k1, v1) = views[0]
    b, s, w = q1.shape
    npair = w // LANES
    partial = []
    for r, (q, k, v) in zip(DIL_DILATIONS[1:], views[1:]):
        cur = pl.BlockSpec((1, DIL_W, r * w), lambda bb, n: (bb, n, 0))
        prev = pl.BlockSpec((1, DIL_W, r * w), lambda bb, n: (bb, jnp.maximum(n - 1, 0), 0))
        partial += pl.pallas_call(
            functools.partial(_dil_partial_kernel, r=r),
            out_shape=[jax.ShapeDtypeStruct((b, npair, s, LANES), F32)] * 2,
            grid=(b, s // (r * DIL_W)),
            in_specs=[cur, prev, cur, prev, cur],
            out_specs=[pl.BlockSpec((1, npair, r * DIL_W, LANES), lambda bb, n: (bb, 0, n, 0))] * 2,
            compiler_params=_cparams(("parallel", "arbitrary")),
            name=f"dilated_r{r}",
        )(q, k, k, v, v)
    tm = nblk * DIL_W
    tok = lambda width: pl.BlockSpec((1, tm, width), lambda bb, n: (bb, n, 0))
    halo = pl.BlockSpec((1, DIL_W, w), lambda bb, n: (bb, jnp.maximum(n * nblk - 1, 0), 0))
    accs = pl.BlockSpec((1, npair, tm, LANES), lambda bb, n: (bb, 0, n, 0))
    return pl.pallas_call(
        functools.partial(_dil_final_kernel, nblk=nblk),
        out_shape=jax.ShapeDtypeStruct((b, s, w), BF16),
        grid=(b, s // tm),
        in_specs=[tok(w), tok(w), halo, tok(w), halo, accs, accs, accs, accs],
        out_specs=tok(w),
        compiler_params=_cparams(("parallel", "arbitrary")),
        name="dilated_r1_merge",
    )(q1, k1, k1, v1, v1, *partial)


def _mla_kernel(q_ref, k_ref, v_ref, o_ref, m_ref, l_ref, acc_ref, *, t, sub):
    qi = pl.program_id(2)
    ki = pl.program_id(3)
    head_a = _lane_ids((1, LANES)) < MLA_V
    nsub = t // sub

    @pl.when(ki == 0)
    def _():
        m_ref[...] = jnp.full_like(m_ref, NEG_INF)
        l_ref[...] = jnp.zeros_like(l_ref)
        acc_ref[...] = jnp.zeros_like(acc_ref)

    def v_stack(c):
        v = v_ref[0, c * sub:(c + 1) * sub]
        zv = jnp.zeros_like(v)
        return jnp.concatenate([jnp.where(head_a, v, zv), jnp.where(head_a, zv, v)], axis=0)

    def column(c, rs, diag_r):
        vs = v_stack(c)
        scores = {}
        for r in rs:
            rows = slice(r * sub, (r + 1) * sub)
            for j in range(2):
                q = q_ref[0, rows, j * LANES:(j + 1) * LANES]
                k = k_ref[0, c * sub:(c + 1) * sub, j * LANES:(j + 1) * LANES]
                s = _dot_nt(q, k)
                if r == diag_r:
                    row = lax.broadcasted_iota(jnp.int32, (sub, sub), 0)
                    colk = lax.broadcasted_iota(jnp.int32, (sub, sub), 1)
                    s = jnp.where(colk <= row, s, NEG_INF)
                scores[(r, j)] = s
        probs = {}
        for r in rs:
            rows = slice(r * sub, (r + 1) * sub)
            ps, alphas = [], []
            for j in range(2):
                s = scores[(r, j)]
                m_old = m_ref[j, rows]
                m_new = jnp.maximum(m_old, jnp.max(s, axis=-1, keepdims=True))
                alpha = jnp.exp2(m_old - m_new)
                pe = jnp.exp2(s - jnp.concatenate([m_new] * (sub // LANES), axis=1))
                l_ref[j, rows] = alpha * l_ref[j, rows] + jnp.sum(pe, axis=-1, keepdims=True)
                m_ref[j, rows] = m_new
                ps.append(pe.astype(BF16))
                alphas.append(alpha)
            probs[r] = (jnp.concatenate(ps, axis=1), jnp.where(head_a, alphas[0], alphas[1]))
        for r in rs:
            rows = slice(r * sub, (r + 1) * sub)
            pcat, alpha = probs[r]
            acc_ref[rows] = acc_ref[rows] * alpha + _dot(pcat, vs)

    @pl.when(ki < qi)
    def _():
        for c in range(nsub):
            column(c, range(nsub), None)

    @pl.when(ki == qi)
    def _():
        for c in range(nsub):
            column(c, range(c, nsub), c)
        o_ref[0] = (acc_ref[...] / jnp.where(head_a, l_ref[0], l_ref[1])).astype(BF16)


def _mla_call(mq, mk, mv, t=1024, sub=512):
    b, s, _ = mq.shape
    n = s // t
    return pl.pallas_call(
        functools.partial(_mla_kernel, t=t, sub=sub),
        out_shape=jax.ShapeDtypeStruct((b, s, MLA_VW), BF16),
        grid=(b, MLA_HEADS // 2, n, n),
        in_specs=[pl.BlockSpec((1, t, 2 * LANES), lambda bb, p, i, j: (bb, i, p)),
                  pl.BlockSpec((1, t, 2 * LANES), lambda bb, p, i, j: (bb, jnp.minimum(j, i), p)),
                  pl.BlockSpec((1, t, LANES), lambda bb, p, i, j: (bb, jnp.minimum(j, i), p))],
        out_specs=pl.BlockSpec((1, t, LANES), lambda bb, p, i, j: (bb, i, p)),
        scratch_shapes=[pltpu.VMEM((2, t, LANES), F32), pltpu.VMEM((2, t, LANES), F32),
                        pltpu.VMEM((t, LANES), F32)],
        compiler_params=_cparams(("parallel", "parallel", "parallel", "arbitrary")),
        name="mla_attention",
    )(mq, mk, mv)


def _outproj_kernel(ya_ref, yb_ref, yc_ref, w_ref, x_ref, g_ref, lg_ref, lb_ref, o_ref):
    a, b = RET_W, RET_W + DIL_WD
    y = (_dot(ya_ref[0], w_ref[0, 0:a, :]) + _dot(yb_ref[0], w_ref[0, a:b, :])
         + _dot(yc_ref[0], w_ref[0, b:, :]))
    v = ALPHA * x_ref[0] + (1.0 + g_ref[0]) * y
    o_ref[0] = _layer_norm(v, lg_ref[...], lb_ref[...])


def _outproj_call(ya, yb, yc, w_out, layer, x, g1, lg, lb, tm=512):
    b, s, d = x.shape
    tok = lambda w: pl.BlockSpec((1, tm, w), lambda i, j: (i, j, 0))
    per_b = pl.BlockSpec((1, 1, d), lambda i, j: (i, 0, 0))
    full = lambda a: pl.BlockSpec(a.shape, lambda i, j: (0,) * a.ndim)
    return pl.pallas_call(
        _outproj_kernel,
        out_shape=jax.ShapeDtypeStruct((b, s, d), F32),
        grid=(b, s // tm),
        in_specs=[tok(ya.shape[-1]), tok(yb.shape[-1]), tok(yc.shape[-1]),
                  pl.BlockSpec((1,) + w_out.shape[1:], lambda i, j: (layer, 0, 0)),
                  tok(d), per_b, full(lg), full(lb)],
        out_specs=tok(d),
        compiler_params=_cparams(("parallel", "parallel")),
        name="out_proj_ln",
    )(ya, yb, yc, w_out, x, g1, lg, lb)


def _swiglu(h, w1_ref, w3_ref, w2_ref, chunk):
    ff = w1_ref.shape[1]
    y = None
    for a in range(0, ff, chunk):
        b = min(a + chunk, ff)
        mid = (_silu(_dot(h, w1_ref[:, a:b])) * _dot(h, w3_ref[:, a:b])).astype(BF16)
        part = _dot(mid, w2_ref[a:b, :])
        y = part if y is None else y + part
    return y


def _ffn_kernel(x_ref, sc_ref, sh_ref, g_ref, w1_ref, w3_ref, w2_ref, lg_ref, lb_ref, o_ref, *, chunk):
    x = x_ref[...]
    h = (x * (1.0 + sc_ref[0]) + sh_ref[0]).astype(BF16)
    y = _swiglu(h, w1_ref.at[0], w3_ref.at[0], w2_ref.at[0], chunk)
    o_ref[...] = _layer_norm(ALPHA * x + (1.0 + g_ref[0]) * y, lg_ref[...], lb_ref[...])


def _ffn_call(x2d, sc, sh, g2, w1, w3, w2, layer, lg, lb, tiles_per_batch, tm, chunk):
    t, d = x2d.shape
    per_b = pl.BlockSpec((1, 1, d), lambda i: (i // tiles_per_batch, 0, 0))
    vec = pl.BlockSpec((1, d), lambda i: (0, 0))
    resident = lambda a: pl.BlockSpec((1,) + a.shape[1:], lambda i: (layer, 0, 0), pipeline_mode=pl.Buffered(1))
    return pl.pallas_call(
        functools.partial(_ffn_kernel, chunk=chunk),
        out_shape=jax.ShapeDtypeStruct((t, d), F32),
        grid=(t // tm,),
        in_specs=[pl.BlockSpec((tm, d), lambda i: (i, 0)), per_b, per_b, per_b,
                  resident(w1), resident(w3), resident(w2), vec, vec],
        out_specs=pl.BlockSpec((tm, d), lambda i: (i, 0)),
        compiler_params=_cparams(("parallel",), FFN_VMEM_LIMIT),
        name="ffn_dense",
    )(x2d, sc, sh, g2, w1, w3, w2, lg, lb)


ROUTE_G_OFF = 2
ROUTE_RANK_OFF = 4


def _router_kernel(x_ref, sc_ref, sh_ref, wr_ref, tri_ref, h_ref, r_ref, cnt_ref, run_ref):
    @pl.when(pl.program_id(0) == 0)
    def _():
        run_ref[...] = jnp.zeros_like(run_ref)

    h = x_ref[...] * (1.0 + sc_ref[0]) + sh_ref[0]
    h_ref[...] = h.astype(BF16)
    logits = jnp.dot(h, wr_ref[...], preferred_element_type=F32, precision=lax.Precision.HIGHEST)
    lane = _lane_ids(logits.shape)
    lg = jnp.where(lane < N_EXPERTS, logits, NEG_INF)
    m1 = jnp.max(lg, axis=-1, keepdims=True)
    i1 = jnp.min(jnp.where(lg == m1, lane, LANES), axis=-1, keepdims=True)
    lg2 = jnp.where(lane == i1, NEG_INF, lg)
    m2 = jnp.max(lg2, axis=-1, keepdims=True)
    i2 = jnp.min(jnp.where(lg2 == m2, lane, LANES), axis=-1, keepdims=True)
    e2 = jnp.exp(m2 - m1)
    den = 1.0 + e2
    chosen = (lane == i1) | (lane == i2)
    cum = _dot(tri_ref[...], chosen.astype(BF16)) + run_ref[...]
    rank1 = jnp.sum(jnp.where(lane == i1, cum, 0.0), axis=-1, keepdims=True) - 1.0
    rank2 = jnp.sum(jnp.where(lane == i2, cum, 0.0), axis=-1, keepdims=True) - 1.0
    run_ref[...] = cum[cum.shape[0] - 1:, :]
    cnt_ref[...] = cum[cum.shape[0] - 1:, :]
    out = jnp.where(lane == 0, i1.astype(F32), 0.0)
    out = jnp.where(lane == 1, i2.astype(F32), out)
    out = jnp.where(lane == ROUTE_G_OFF, 1.0 / den, out)
    out = jnp.where(lane == ROUTE_G_OFF + 1, e2 / den, out)
    out = jnp.where(lane == ROUTE_RANK_OFF, rank1, out)
    out = jnp.where(lane == ROUTE_RANK_OFF + 1, rank2, out)
    r_ref[...] = out


def _router_call(x2d, sc, sh, wr_p, tiles_per_batch, tm):
    t, d = x2d.shape
    per_b = pl.BlockSpec((1, 1, d), lambda i: (i // tiles_per_batch, 0, 0))
    tri = (jnp.arange(tm)[:, None] >= jnp.arange(tm)[None, :]).astype(BF16)
    return pl.pallas_call(
        _router_kernel,
        out_shape=[jax.ShapeDtypeStruct((t, d), BF16), jax.ShapeDtypeStruct((t, LANES), F32),
                   jax.ShapeDtypeStruct((1, LANES), F32)],
        grid=(t // tm,),
        in_specs=[pl.BlockSpec((tm, d), lambda i: (i, 0)), per_b, per_b,
                  pl.BlockSpec(wr_p.shape, lambda i: (0, 0)), pl.BlockSpec((tm, tm), lambda i: (0, 0))],
        out_specs=[pl.BlockSpec((tm, d), lambda i: (i, 0)), pl.BlockSpec((tm, LANES), lambda i: (i, 0)),
                   pl.BlockSpec((1, LANES), lambda i: (0, 0))],
        scratch_shapes=[pltpu.VMEM((1, LANES), F32)],
        compiler_params=_cparams(("arbitrary",)),
        name="moe_router",
    )(x2d, sc, sh, wr_p, tri)


def _expert_kernel(te_ref, nu_ref, x_ref, w1_ref, w3_ref, w2_ref, o_ref, *, chunk):
    i = pl.program_id(0)

    @pl.when(i < nu_ref[0])
    def _():
        o_ref[...] = _swiglu(x_ref[...], w1_ref.at[0, 0], w3_ref.at[0, 0], w2_ref.at[0, 0], chunk).astype(o_ref.dtype)

    @pl.when(i >= nu_ref[0])
    def _():
        o_ref[...] = jnp.zeros_like(o_ref)


def _expert_call(tile_expert, n_used, xs, w1, w3, w2, layer, tm, chunk):
    p, d = xs.shape
    ff = w1.shape[3]
    wspec = lambda shape: pl.BlockSpec((1,) + shape, lambda i, te, nu: (layer, te[i], 0, 0),
                                       pipeline_mode=pl.Buffered(1))
    return pl.pallas_call(
        functools.partial(_expert_kernel, chunk=chunk),
        out_shape=jax.ShapeDtypeStruct((p, d), BF16),
        grid_spec=pltpu.PrefetchScalarGridSpec(
            num_scalar_prefetch=2,
            grid=(p // tm,),
            in_specs=[pl.BlockSpec((tm, d), lambda i, te, nu: (i, 0)),
                      wspec((1, d, ff)), wspec((1, d, ff)), wspec((1, ff, d))],
            out_specs=pl.BlockSpec((tm, d), lambda i, te, nu: (i, 0))),
        compiler_params=_cparams(("arbitrary",), FFN_VMEM_LIMIT),
        name="moe_experts",
    )(tile_expert, n_used, xs, w1, w3, w2)


def _combine_kernel(x_ref, ya_ref, yb_ref, r_ref, g_ref, lg_ref, lb_ref, o_ref):
    r = r_ref[...]
    lane = _lane_ids(r.shape)
    ga = jnp.sum(jnp.where(lane == ROUTE_G_OFF, r, 0.0), axis=-1, keepdims=True)
    gb = jnp.sum(jnp.where(lane == ROUTE_G_OFF + 1, r, 0.0), axis=-1, keepdims=True)
    y = ga * ya_ref[...].astype(F32) + gb * yb_ref[...].astype(F32)
    v = ALPHA * x_ref[...] + (1.0 + g_ref[0]) * y
    o_ref[...] = _layer_norm(v, lg_ref[...], lb_ref[...])


def _combine_call(x2d, ya, yb, route, g2, lg, lb, tiles_per_batch, tm):
    t, d = x2d.shape
    tok = pl.BlockSpec((tm, d), lambda i: (i, 0))
    per_b = pl.BlockSpec((1, 1, d), lambda i: (i // tiles_per_batch, 0, 0))
    vec = pl.BlockSpec((1, d), lambda i: (0, 0))
    return pl.pallas_call(
        _combine_kernel,
        out_shape=jax.ShapeDtypeStruct((t, d), F32),
        grid=(t // tm,),
        in_specs=[tok, tok, tok, pl.BlockSpec((tm, LANES), lambda i: (i, 0)), per_b, vec, vec],
        out_specs=tok,
        compiler_params=_cparams(("parallel",)),
        name="moe_combine_ln",
    )(x2d, ya, yb, route, g2, lg, lb)


def _dispatch_plan(route, counts, tm):
    t = route.shape[0]
    counts = counts[0, :N_EXPERTS].astype(jnp.int32)
    padded = ((counts + tm - 1) // tm) * tm
    ends = jnp.cumsum(padded)
    starts = ends - padded
    e = route[:, 0:2].astype(jnp.int32)
    rank = route[:, ROUTE_RANK_OFF:ROUTE_RANK_OFF + 2].astype(jnp.int32)
    dest = starts[e] + rank
    n_rows = 2 * t + N_EXPERTS * tm
    tok = jnp.arange(t, dtype=jnp.int32)
    row_token = jnp.zeros((n_rows,), jnp.int32).at[dest.T.reshape(-1)].set(
        jnp.concatenate([tok, tok]), mode="promise_in_bounds", unique_indices=True)
    tile_start = jnp.arange(n_rows // tm, dtype=jnp.int32) * tm
    tile_expert = jnp.minimum(jnp.sum((tile_start[:, None] >= ends[None, :]).astype(jnp.int32), axis=1),
                              N_EXPERTS - 1)
    n_used = (ends[-1] // tm).astype(jnp.int32).reshape(1)
    return row_token, dest, tile_expert, n_used


def _moe_layer(x2d, sc, sh, g2, wr_p, w1, w3, w2, layer, lg, lb, tiles_per_batch, tm, tme, tf):
    h, route, counts = _router_call(x2d, sc, sh, wr_p, tiles_per_batch, tm)
    row_token, dest, tile_expert, n_used = _dispatch_plan(route, counts, tme)
    xs = h.at[row_token].get(mode="promise_in_bounds")
    ys = _expert_call(tile_expert, n_used, xs, w1, w3, w2, layer, tme, tf)
    ya = ys.at[dest[:, 0]].get(mode="promise_in_bounds")
    yb = ys.at[dest[:, 1]].get(mode="promise_in_bounds")
    return _combine_call(x2d, ya, yb, route, g2, lg, lb, tiles_per_batch, tm)


def _trig_kernel(a_ref, c_ref, s_ref):
    a = a_ref[...]
    c_ref[...] = jnp.cos(a)
    s_ref[...] = jnp.sin(a)


def _trig_call(ang2d):
    n = ang2d.shape[0]
    tr = math.gcd(n, 1024)
    spec = pl.BlockSpec((tr, LANES), lambda i: (i, 0))
    return pl.pallas_call(
        _trig_kernel,
        out_shape=[jax.ShapeDtypeStruct(ang2d.shape, F32)] * 2,
        grid=(n // tr,),
        in_specs=[spec],
        out_specs=[spec, spec],
        compiler_params=_cparams(("parallel",)),
        name="rope_trig",
    )(ang2d)


def _rope_tables(positions):
    pos = positions.astype(F32)[..., None]
    lane = jnp.arange(LANES)
    inv64 = ROPE_THETA ** (-jnp.arange(0, RET_D, 2, dtype=F32) / RET_D)
    invm = ROPE_THETA ** (-jnp.arange(0, MLA_ROPE, 2, dtype=F32) / MLA_ROPE)
    n64 = inv64.shape[0]
    ang = pos * jnp.concatenate([inv64, invm])
    cos, sin = _trig_call(ang.reshape(-1, LANES))
    cos, sin = cos.reshape(ang.shape), sin.reshape(ang.shape)
    cos64, sin64, cosm, sinm = cos[..., :n64], sin[..., :n64], cos[..., n64:], sin[..., n64:]
    sign64 = jnp.where((lane % 64) < 32, -1.0, 1.0)
    c64 = jnp.tile(cos64, (1, 1, LANES // 32))
    s64 = jnp.tile(sin64, (1, 1, LANES // 32)) * sign64
    in_rope = (lane >= KR_LANE) & (lane < KR_LANE + MLA_ROPE)
    signm = jnp.where(lane < KR_LANE + MLA_ROPE // 2, -1.0, 1.0)
    cm = jnp.where(in_rope, jnp.tile(cosm, (1, 1, LANES // 16)), 1.0)
    sm = jnp.where(in_rope, jnp.tile(sinm, (1, 1, LANES // 16)) * signm, 0.0)
    return c64, s64, cm, sm


def _prep_mixer_weights(w_in, w_uq, w_ukv):
    d = w_in.shape[0]
    kr_cols = jnp.zeros((d, LANES), w_in.dtype).at[:, KR_LANE:KR_LANE + MLA_ROPE].set(
        w_in[:, MLA_OFF + MLA_Q_RANK + MLA_KV_RANK:])
    w_in_p = jnp.concatenate([w_in[:, :MLA_OFF + MLA_Q_RANK + MLA_KV_RANK], kr_cols], axis=1).astype(BF16)
    uq = w_uq.reshape(MLA_Q_RANK, MLA_HEADS, MLA_NOPE + MLA_ROPE)
    wuq_p = jnp.pad(uq, ((0, 0), (0, 0), (0, LANES - MLA_NOPE - MLA_ROPE))).reshape(MLA_Q_RANK, MLA_PAD).astype(BF16)
    ukv = w_ukv.reshape(MLA_KV_RANK, MLA_HEADS, MLA_NOPE + MLA_V)
    wkn_p = jnp.pad(ukv[:, :, :MLA_NOPE], ((0, 0), (0, 0), (0, LANES - MLA_NOPE))).reshape(MLA_KV_RANK, MLA_PAD).astype(BF16)
    wv = ukv[:, :, MLA_NOPE:].reshape(MLA_KV_RANK, MLA_VW).astype(BF16)
    return w_in_p, wuq_p, wkn_p, wv


def kernel(x, c, positions, w_in, ret_gn_g, mla_qn_g, mla_kvn_g, w_uq, w_ukv, w_out, w_ada, b_ada, ln1_g, ln1_b, ln2_g, ln2_b, w1_dense, w3_dense, w2_dense, w_router, w1_moe, w3_moe, w2_moe):
    b, s, d = x.shape
    tabs = _rope_tables(positions)
    ret_tabs = _ret_tables()
    mod = _ada_call(c, w_ada, b_ada)
    tm_tok = 512
    tiles_per_batch = s // tm_tok
    w_out_b, w1d, w3d, w2d, w1m, w3m, w2m = [
        _cast_call(w) for w in (w_out, w1_dense, w3_dense, w2_dense, w1_moe, w3_moe, w2_moe)]
    for l in range(w_in.shape[0]):
        sh1, sc1, g1, sh2, sc2, g2 = [mod[l, :, j * d:(j + 1) * d].reshape(b, 1, d) for j in range(6)]
        w_in_p, wuq_p, wkn_p, wv = _prep_mixer_weights(w_in[l], w_uq[l], w_ukv[l])
        (rq, rk, rv, rg), dil_views, (mq, mk, mv) = _inproj_call(
            x, sc1, sh1, w_in_p, tabs, mla_qn_g[l].reshape(1, -1), mla_kvn_g[l].reshape(1, -1), wuq_p, wkn_p, wv)
        ya = _ret_call(rq, rk, rv, rg, ret_gn_g[l].reshape(1, -1), ret_tabs)
        yb = _dil_call(dil_views)
        yc = _mla_call(mq, mk, mv)
        x = _outproj_call(ya, yb, yc, w_out_b, l, x, g1, ln1_g[l].reshape(1, d), ln1_b[l].reshape(1, d))
        x2d = x.reshape(b * s, d)
        lg, lb = ln2_g[l].reshape(1, d), ln2_b[l].reshape(1, d)
        j = l // 2
        if l % 2 == 0:
            x2d = _ffn_call(x2d, sc2, sh2, g2, w1d, w3d, w2d, j, lg, lb, tiles_per_batch, tm_tok, FF_CHUNK)
        else:
            wr_p = jnp.pad(w_router[j], ((0, 0), (0, LANES - N_EXPERTS)))
            x2d = _moe_layer(x2d, sc2, sh2, g2, wr_p, w1m, w3m, w2m, j, lg, lb, tiles_per_batch, tm_tok,
                             EXPERT_TILE, FF_CHUNK)
        x = x2d.reshape(b, s, d)
    return x
```

```python
import functools
import math

import jax
import jax.numpy as jnp
from jax import lax
from jax.experimental import pallas as pl
from jax.experimental.pallas import tpu as pltpu

D_MODEL = 1024
DEPTH = 4
RET_HEADS = 4
RET_D = 64
RET_CHUNK = 128
DIL_HEADS = 6
DIL_D = 64
DIL_DILATIONS = (1, 4, 16)
DIL_W = 128
MLA_HEADS = 6
MLA_Q_RANK = 384
MLA_KV_RANK = 256
MLA_NOPE = 64
MLA_ROPE = 32
MLA_V = 64
N_EXPERTS = 8
ROPE_THETA = 10000.0
LN_EPS = 1e-5
RMS_EPS = 1e-6
ALPHA = (2.0 * DEPTH) ** 0.25

LANES = 128
RET_W = RET_HEADS * RET_D
DIL_WD = DIL_HEADS * DIL_D
MLA_PAD = MLA_HEADS * LANES
MLA_VW = MLA_HEADS * MLA_V
RET_IN = 4 * RET_W
DIL_IN = 3 * DIL_WD
MLA_OFF = RET_IN + DIL_IN
D_IN_PAD = MLA_OFF + MLA_Q_RANK + MLA_KV_RANK + LANES
KR_LANE = MLA_NOPE

VMEM_LIMIT = 48 * 1024 * 1024
FFN_VMEM_LIMIT = 56 * 1024 * 1024
FF_CHUNK = 512
EXPERT_TILE = 512
CAST_BLOCK_BYTES = 4 * 1024 * 1024
BF16 = jnp.bfloat16
F32 = jnp.float32
NEG_INF = float("-inf")
LOG2E = math.log2(math.e)
MLA_Q_SCALE = (MLA_NOPE + MLA_ROPE) ** -0.5 * LOG2E


def _cparams(sem, vmem_limit=VMEM_LIMIT):
    return pltpu.CompilerParams(dimension_semantics=sem, vmem_limit_bytes=vmem_limit)


def _dot(a, b):
    return jnp.dot(a, b, preferred_element_type=F32)


def _dot_nt(a, b):
    return lax.dot_general(a, b, (((1,), (1,)), ((), ())), preferred_element_type=F32)


def _dot_tn(a, b):
    return lax.dot_general(a, b, (((0,), (0,)), ((), ())), preferred_element_type=F32)


def _silu(x):
    return x * (1.0 / (1.0 + jnp.exp(-x)))


def _layer_norm(v, g, b):
    mu = jnp.mean(v, axis=-1, keepdims=True)
    d = v - mu
    var = jnp.mean(d * d, axis=-1, keepdims=True)
    return d * lax.rsqrt(var + LN_EPS) * g + b


def _lane_ids(shape):
    return lax.broadcasted_iota(jnp.int32, shape, len(shape) - 1)


def _rope_group(x, cos, sin_signed, first_half, half):
    fwd = pltpu.roll(x, LANES - half, 1)
    bwd = pltpu.roll(x, half, 1)
    return x * cos + jnp.where(first_half, fwd, bwd) * sin_signed


def _cast_kernel(w_ref, o_ref):
    o_ref[...] = w_ref[...].astype(BF16)


def _cast_call(w):
    cols = w.shape[-1]
    w2d = w.reshape(-1, cols)
    rows = w2d.shape[0]
    pack = 16
    tr = max(pack, CAST_BLOCK_BYTES // (4 * cols) // pack * pack)
    while rows % tr:
        tr -= pack
    spec = pl.BlockSpec((tr, cols), lambda i: (i, 0))
    out = pl.pallas_call(
        _cast_kernel,
        out_shape=jax.ShapeDtypeStruct(w2d.shape, BF16),
        grid=(rows // tr,),
        in_specs=[spec],
        out_specs=spec,
        compiler_params=_cparams(("parallel",)),
        name="cast_bf16",
    )(w2d)
    return out.reshape(w.shape)


def _ada_kernel(c_ref, w_ref, b_ref, o_ref):
    cond = _silu(c_ref[...])
    o_ref[0] = jnp.dot(cond, w_ref[0], preferred_element_type=F32,
                       precision=lax.Precision.HIGHEST) + b_ref[0]


def _ada_call(c, w_ada, b_ada):
    nl, d, n = w_ada.shape
    b = c.shape[0]
    tn = 1536
    return pl.pallas_call(
        _ada_kernel,
        out_shape=jax.ShapeDtypeStruct((nl, b, n), F32),
        grid=(nl, n // tn),
        in_specs=[pl.BlockSpec((b, d), lambda l, j: (0, 0)),
                  pl.BlockSpec((1, d, tn), lambda l, j: (l, 0, j)),
                  pl.BlockSpec((1, 1, tn), lambda l, j: (l, 0, j))],
        out_specs=pl.BlockSpec((1, b, tn), lambda l, j: (l, 0, j)),
        compiler_params=_cparams(("arbitrary", "arbitrary")),
        name="ada_mod",
    )(c, w_ada, b_ada.reshape(nl, 1, n))


def _inproj_kernel(x_ref, sc_ref, sh_ref, w_ref, c64_ref, s64_ref, cm_ref, sm_ref,
                   qg_ref, kvg_ref, wuq_ref, wkn_ref, wv_ref,
                   rq_ref, rk_ref, rv_ref, rg_ref,
                   dq1_ref, dk1_ref, dv1_ref, dq4_ref, dk4_ref, dv4_ref, dq16_ref, dk16_ref, dv16_ref,
                   mq_ref, mk_ref, mv_ref, scr_ref, *, tm):
    h = (x_ref[0] * (1.0 + sc_ref[0]) + sh_ref[0]).astype(BF16)
    c64 = c64_ref[0]
    s64 = s64_ref[0]
    cm = cm_ref[0]
    sm = sm_ref[0]
    lane = _lane_ids((1, LANES))
    first64 = (lane % 64) < 32
    firstm = lane < (KR_LANE + MLA_ROPE // 2)

    def rope64(zc):
        return _rope_group(zc, c64, s64, first64, 32)

    def ropem(zc):
        return _rope_group(zc, cm, sm, firstm, MLA_ROPE // 2)

    za = _dot(h, w_ref[:, 0:RET_IN])
    for j in range(RET_W // LANES):
        sl = slice(j * LANES, (j + 1) * LANES)
        rq_ref[0, :, sl] = rope64(za[:, j * LANES:(j + 1) * LANES]).astype(BF16)
        kc = za[:, RET_W + j * LANES:RET_W + (j + 1) * LANES]
        rk_ref[0, :, sl] = (rope64(kc) * (RET_D ** -0.5)).astype(BF16)
    rv_ref[0] = za[:, 2 * RET_W:3 * RET_W].astype(BF16)
    rg_ref[0] = za[:, 3 * RET_W:4 * RET_W]

    zb = _dot(h, w_ref[:, RET_IN:MLA_OFF])
    q_scale = (DIL_D ** -0.5) * LOG2E
    views = ((dq1_ref, dq4_ref, dq16_ref), (dk1_ref, dk4_ref, dk16_ref), (dv1_ref, dv4_ref, dv16_ref))
    for a, (n1, n4, n16) in enumerate(views):
        for j in range(DIL_WD // LANES):
            zc = zb[:, a * DIL_WD + j * LANES:a * DIL_WD + (j + 1) * LANES]
            if a == 0:
                zc = rope64(zc) * q_scale
            elif a == 1:
                zc = rope64(zc)
            scr_ref[j] = zc
            n1[0, :, j * LANES:(j + 1) * LANES] = zc.astype(BF16)
        for r, ref in ((4, n4), (16, n16)):
            for rho in range(r):
                for j in range(DIL_WD // LANES):
                    ref[0, :, rho * DIL_WD + j * LANES:rho * DIL_WD + (j + 1) * LANES] = (
                        scr_ref[j, pl.ds(rho, tm // r, stride=r), :].astype(BF16))

    zc = _dot(h, w_ref[:, MLA_OFF:D_IN_PAD])
    cq = zc[:, 0:MLA_Q_RANK]
    ckv = zc[:, MLA_Q_RANK:MLA_Q_RANK + MLA_KV_RANK]
    kr = ropem(zc[:, MLA_Q_RANK + MLA_KV_RANK:])
    cqn = (cq * lax.rsqrt(jnp.mean(cq * cq, axis=-1, keepdims=True) + RMS_EPS) * qg_ref[...]).astype(BF16)
    ckvn = (ckv * lax.rsqrt(jnp.mean(ckv * ckv, axis=-1, keepdims=True) + RMS_EPS) * kvg_ref[...]).astype(BF16)
    q = _dot(cqn, wuq_ref[...])
    kn = _dot(ckvn, wkn_ref[...])
    for hh in range(MLA_HEADS):
        sl = slice(hh * LANES, (hh + 1) * LANES)
        mq_ref[0, :, sl] = (ropem(q[:, hh * LANES:(hh + 1) * LANES]) * MLA_Q_SCALE).astype(BF16)
        mk_ref[0, :, sl] = (kn[:, hh * LANES:(hh + 1) * LANES] + kr).astype(BF16)
    mv_ref[0] = _dot(ckvn, wv_ref[...]).astype(BF16)


def _inproj_call(x, sc, sh, w_in_p, tabs, qg, kvg, wuq_p, wkn_p, wv, tm=512):
    b, s, d = x.shape
    c64, s64, cm, sm = tabs
    tok = lambda w: pl.BlockSpec((1, tm, w), lambda i, j: (i, j, 0))
    per_b = pl.BlockSpec((1, 1, d), lambda i, j: (i, 0, 0))
    full = lambda a: pl.BlockSpec(a.shape, lambda i, j: (0,) * a.ndim)
    outs = [(1, RET_W, BF16), (1, RET_W, BF16), (1, RET_W, BF16), (1, RET_W, F32)]
    outs += [(r, DIL_WD, BF16) for r in DIL_DILATIONS for _ in range(3)]
    outs += [(1, MLA_PAD, BF16), (1, MLA_PAD, BF16), (1, MLA_VW, BF16)]
    res = pl.pallas_call(
        functools.partial(_inproj_kernel, tm=tm),
        out_shape=[jax.ShapeDtypeStruct((b, s // r, r * w), dt) for r, w, dt in outs],
        grid=(b, s // tm),
        in_specs=[tok(d), per_b, per_b, full(w_in_p), tok(LANES), tok(LANES), tok(LANES), tok(LANES),
                  full(qg), full(kvg), full(wuq_p), full(wkn_p), full(wv)],
        out_specs=[pl.BlockSpec((1, tm // r, r * w), lambda i, j: (i, j, 0)) for r, w, _ in outs],
        scratch_shapes=[pltpu.VMEM((DIL_WD // LANES, tm, LANES), F32)],
        compiler_params=_cparams(("parallel", "parallel")),
        name="in_proj",
    )(x, sc, sh, w_in_p, c64, s64, cm, sm, qg, kvg, wuq_p, wkn_p, wv)
    ret = res[0:4]
    dil = [res[4 + 3 * i:7 + 3 * i] for i in range(len(DIL_DILATIONS))]
    mla = res[4 + 3 * len(DIL_DILATIONS):]
    return ret, dil, mla


def _ret_kernel(q_ref, k_ref, v_ref, g_ref, gn_ref, intra_ref, xi_ref, zeta_ref, dmat_ref, bmask_ref,
                o_ref, state_ref, *, chunks):
    @pl.when(pl.program_id(1) == 0)
    def _():
        state_ref[...] = jnp.zeros_like(state_ref)

    lane = _lane_ids((1, LANES))
    head_a = lane < RET_D
    c = RET_CHUNK
    npair = RET_W // LANES
    items = [(ci, p) for ci in range(chunks) for p in range(npair)]
    view = lambda ref, ci, p: ref[0, ci * c:(ci + 1) * c, p * LANES:(p + 1) * LANES]

    inner, kv_inc = {}, {}
    for ci, p in items:
        cols = slice(p * LANES, (p + 1) * LANES)
        q, k, v = view(q_ref, ci, p), view(k_ref, ci, p), view(v_ref, ci, p)
        zero = jnp.zeros_like(q)
        s_a = _dot_nt(jnp.where(head_a, q, zero), k) * intra_ref[2 * p]
        s_b = _dot_nt(jnp.where(head_a, zero, q), k) * intra_ref[2 * p + 1]
        s_cat = jnp.concatenate([s_a, s_b], axis=1).astype(BF16)
        v_stack = jnp.concatenate([jnp.where(head_a, v, zero), jnp.where(head_a, zero, v)], axis=0)
        inner[(ci, p)] = _dot(s_cat, v_stack)
        kz = (k.astype(F32) * zeta_ref[:, cols]).astype(BF16)
        kv_inc[(ci, p)] = _dot_tn(kz, v) * bmask_ref[...]

    outs = {}
    for p in range(npair):
        cols = slice(p * LANES, (p + 1) * LANES)
        state = state_ref[p]
        for ci in range(chunks):
            cross = _dot(view(q_ref, ci, p), state.astype(BF16)) * xi_ref[:, cols]
            outs[(ci, p)] = inner[(ci, p)] + cross
            state = state * dmat_ref[p] + kv_inc[(ci, p)]
        state_ref[p] = state

    inv_n = 1.0 / RET_D
    for ci, p in items:
        cols = slice(p * LANES, (p + 1) * LANES)
        o = outs[(ci, p)]
        mu = jnp.where(head_a,
                       jnp.sum(jnp.where(head_a, o, 0.0), axis=-1, keepdims=True),
                       jnp.sum(jnp.where(head_a, 0.0, o), axis=-1, keepdims=True)) * inv_n
        d = o - mu
        dd = d * d
        var = jnp.where(head_a,
                        jnp.sum(jnp.where(head_a, dd, 0.0), axis=-1, keepdims=True),
                        jnp.sum(jnp.where(head_a, 0.0, dd), axis=-1, keepdims=True)) * inv_n
        on = d * lax.rsqrt(var + LN_EPS) * gn_ref[:, cols]
        o_ref[0, ci * c:(ci + 1) * c, cols] = (_silu(view(g_ref, ci, p)) * on).astype(BF16)


def _ret_tables():
    c = RET_CHUNK
    log_g = jnp.log(1.0 - 2.0 ** (-5.0 - jnp.arange(RET_HEADS, dtype=F32)))
    idx = jnp.arange(c, dtype=F32)
    diff = idx[:, None] - idx[None, :]
    intra = jnp.where(diff >= 0, jnp.exp(jnp.maximum(diff, 0.0) * log_g[:, None, None]), 0.0)
    xi = jnp.exp((idx[:, None] + 1.0) * log_g[None, :])
    zeta = jnp.exp((c - 1.0 - idx[:, None]) * log_g[None, :])
    decay = jnp.exp(c * log_g)
    xi_l = jnp.repeat(xi, RET_D, axis=1)
    zeta_l = jnp.repeat(zeta, RET_D, axis=1)
    r = jnp.arange(LANES)
    same = (r[:, None] // RET_D) == (r[None, :] // RET_D)
    bmask = same.astype(F32)
    dec_l = jnp.repeat(decay, RET_D).reshape(RET_W // LANES, 1, LANES)
    dmat = bmask[None] * dec_l
    return intra, xi_l, zeta_l, dmat, bmask


def _ret_call(rq, rk, rv, rg, gn, tables, chunks=8):
    b, s, w = rq.shape
    tm = chunks * RET_CHUNK
    intra, xi_l, zeta_l, dmat, bmask = tables
    tok = pl.BlockSpec((1, tm, w), lambda i, j: (i, j, 0))
    full = lambda a: pl.BlockSpec(a.shape, lambda i, j: (0,) * a.ndim)
    return pl.pallas_call(
        functools.partial(_ret_kernel, chunks=chunks),
        out_shape=jax.ShapeDtypeStruct((b, s, w), BF16),
        grid=(b, s // tm),
        in_specs=[tok, tok, tok, tok, full(gn), full(intra), full(xi_l), full(zeta_l), full(dmat), full(bmask)],
        out_specs=tok,
        scratch_shapes=[pltpu.VMEM((w // LANES, LANES, LANES), F32)],
        compiler_params=_cparams(("parallel", "arbitrary")),
        name="retention",
    )(rq, rk, rv, rg, gn, intra, xi_l, zeta_l, dmat, bmask)


def _dil_band(first_block):
    w = DIL_W
    qi = lax.broadcasted_iota(jnp.int32, (w, 2 * w), 0)
    kj = lax.broadcasted_iota(jnp.int32, (w, 2 * w), 1)
    dist = w + qi - kj
    band = (dist >= 0) & (dist <= w)
    if first_block is False:
        return band
    return band & (jnp.logical_not(first_block) | (kj >= w))


DIL_GROUP = 2


def _dil_attend(items, valid_of):
    head_a = _lane_ids((1, LANES)) < DIL_D
    scores = []
    for idx, (q, kcat, _) in enumerate(items):
        zero = jnp.zeros_like(q)
        valid = valid_of(idx)
        scores.append([jnp.where(valid, _dot_nt(qh, kcat), NEG_INF)
                       for qh in (jnp.where(head_a, q, zero), jnp.where(head_a, zero, q))])
    soft = []
    for scs in scores:
        ms, ls, ps = [], [], []
        for sc in scs:
            m = jnp.max(sc, axis=-1, keepdims=True)
            pe = jnp.exp2(sc - m)
            ms.append(m)
            ls.append(jnp.sum(pe, axis=-1, keepdims=True))
            ps.append(pe.astype(BF16))
        soft.append((ms, ls, jnp.concatenate(ps, axis=1)))
    out = []
    for (_, _, vcat), (ms, ls, pcat) in zip(items, soft):
        zv = jnp.zeros_like(vcat)
        v_stack = jnp.concatenate([jnp.where(head_a, vcat, zv), jnp.where(head_a, zv, vcat)], axis=0)
        out.append((_dot(pcat, v_stack), ms, ls))
    return out


def _dil_normalise(acc, ms, ls):
    head_a = _lane_ids((1, LANES)) < DIL_D
    o = acc * jnp.where(head_a, 1.0 / ls[0], 1.0 / ls[1])
    lse = jnp.where(head_a, ms[0] + jnp.log2(ls[0]), ms[1] + jnp.log2(ls[1]))
    return o, lse


def _dil_partial_kernel(q_ref, kp_ref, kc_ref, vp_ref, vc_ref, o_ref, lse_ref, *, r):
    valid = _dil_band(pl.program_id(1) == 0)
    npair = DIL_WD // LANES
    for g0 in range(0, r, DIL_GROUP):
        rhos = range(g0, min(g0 + DIL_GROUP, r))
        items = []
        for rho in rhos:
            for p in range(npair):
                cols = slice(rho * DIL_WD + p * LANES, rho * DIL_WD + (p + 1) * LANES)
                items.append((q_ref[0, :, cols],
                              jnp.concatenate([kp_ref[0, :, cols], kc_ref[0, :, cols]], axis=0),
                              jnp.concatenate([vp_ref[0, :, cols], vc_ref[0, :, cols]], axis=0)))
        res = _dil_attend(items, lambda idx: valid)
        for gi, rho in enumerate(rhos):
            rows = pl.ds(rho, DIL_W, stride=r)
            for p in range(npair):
                o, lse = _dil_normalise(*res[gi * npair + p])
                o_ref[0, p, rows, :] = o
                lse_ref[0, p, rows, :] = lse


def _dil_final_kernel(q_ref, k_ref, kh_ref, v_ref, vh_ref, o4_ref, lse4_ref, o16_ref, lse16_ref, o_ref, *, nblk):
    w = DIL_W
    npair = DIL_WD // LANES
    band_first = _dil_band(pl.program_id(1) == 0)
    band = _dil_band(False)

    for g0 in range(0, nblk, DIL_GROUP):
        blocks = range(g0, min(g0 + DIL_GROUP, nblk))
        items = []
        for i in blocks:
            rows = slice(i * w, (i + 1) * w)
            for p in range(npair):
                cols = slice(p * LANES, (p + 1) * LANES)
                if i == 0:
                    kcat = jnp.concatenate([kh_ref[0, :, cols], k_ref[0, rows, cols]], axis=0)
                    vcat = jnp.concatenate([vh_ref[0, :, cols], v_ref[0, rows, cols]], axis=0)
                else:
                    kcat = k_ref[0, (i - 1) * w:(i + 1) * w, cols]
                    vcat = v_ref[0, (i - 1) * w:(i + 1) * w, cols]
                items.append((q_ref[0, rows, cols], kcat, vcat))
        res = _dil_attend(items, lambda idx: band_first if blocks[idx // npair] == 0 else band)
        for gi, i in enumerate(blocks):
            rows = slice(i * w, (i + 1) * w)
            for p in range(npair):
                o1, lse1 = _dil_normalise(*res[gi * npair + p])
                lse4 = lse4_ref[0, p, rows, :]
                lse16 = lse16_ref[0, p, rows, :]
                top = jnp.maximum(jnp.maximum(lse1, lse4), lse16)
                w1, w4, w16 = jnp.exp2(lse1 - top), jnp.exp2(lse4 - top), jnp.exp2(lse16 - top)
                num = w1 * o1 + w4 * o4_ref[0, p, rows, :] + w16 * o16_ref[0, p, rows, :]
                o_ref[0, rows, p * LANES:(p + 1) * LANES] = (num / (w1 + w4 + w16)).astype(BF16)


def _dil_call(views, nblk=4):
    (q1, k1, v1) = views[0]
    b, s, w = q1.shape
    npair = w // LANES
    partial = []
    for r, (q, k, v) in zip(DIL_DILATIONS[1:], views[1:]):
        cur = pl.BlockSpec((1, DIL_W, r * w), lambda bb, n: (bb, n, 0))
        prev = pl.BlockSpec((1, DIL_W, r * w), lambda bb, n: (bb, jnp.maximum(n - 1, 0), 0))
        partial += pl.pallas_call(
            functools.partial(_dil_partial_kernel, r=r),
            out_shape=[jax.ShapeDtypeStruct((b, npair, s, LANES), F32)] * 2,
            grid=(b, s // (r * DIL_W)),
            in_specs=[cur, prev, cur, prev, cur],
            out_specs=[pl.BlockSpec((1, npair, r * DIL_W, LANES), lambda bb, n: (bb, 0, n, 0))] * 2,
            compiler_params=_cparams(("parallel", "arbitrary")),
            name=f"dilated_r{r}",
        )(q, k, k, v, v)
    tm = nblk * DIL_W
    tok = lambda width: pl.BlockSpec((1, tm, width), lambda bb, n: (bb, n, 0))
    halo = pl.BlockSpec((1, DIL_W, w), lambda bb, n: (bb, jnp.maximum(n * nblk - 1, 0), 0))
    accs = pl.BlockSpec((1, npair, tm, LANES), lambda bb, n: (bb, 0, n, 0))
    return pl.pallas_call(
        functools.partial(_dil_final_kernel, nblk=nblk),
        out_shape=jax.ShapeDtypeStruct((b, s, w), BF16),
        grid=(b, s // tm),
        in_specs=[tok(w), tok(w), halo, tok(w), halo, accs, accs, accs, accs],
        out_specs=tok(w),
        compiler_params=_cparams(("parallel", "arbitrary")),
        name="dilated_r1_merge",
    )(q1, k1, k1, v1, v1, *partial)


def _mla_kernel(q_ref, k_ref, v_ref, o_ref, m_ref, l_ref, acc_ref, *, t, sub):
    qi = pl.program_id(2)
    ki = pl.program_id(3)
    head_a = _lane_ids((1, LANES)) < MLA_V
    nsub = t // sub

    @pl.when(ki == 0)
    def _():
        m_ref[...] = jnp.full_like(m_ref, NEG_INF)
        l_ref[...] = jnp.zeros_like(l_ref)
        acc_ref[...] = jnp.zeros_like(acc_ref)

    def v_stack(c):
        v = v_ref[0, c * sub:(c + 1) * sub]
        zv = jnp.zeros_like(v)
        return jnp.concatenate([jnp.where(head_a, v, zv), jnp.where(head_a, zv, v)], axis=0)

    def column(c, rs, diag_r):
        vs = v_stack(c)
        scores = {}
        for r in rs:
            rows = slice(r * sub, (r + 1) * sub)
            for j in range(2):
                q = q_ref[0, rows, j * LANES:(j + 1) * LANES]
                k = k_ref[0, c * sub:(c + 1) * sub, j * LANES:(j + 1) * LANES]
                s = _dot_nt(q, k)
                if r == diag_r:
                    row = lax.broadcasted_iota(jnp.int32, (sub, sub), 0)
                    colk = lax.broadcasted_iota(jnp.int32, (sub, sub), 1)
                    s = jnp.where(colk <= row, s, NEG_INF)
                scores[(r, j)] = s
        probs = {}
        for r in rs:
            rows = slice(r * sub, (r + 1) * sub)
            ps, alphas = [], []
            for j in range(2):
                s = scores[(r, j)]
                m_old = m_ref[j, rows]
                m_new = jnp.maximum(m_old, jnp.max(s, axis=-1, keepdims=True))
                alpha = jnp.exp2(m_old - m_new)
                pe = jnp.exp2(s - jnp.concatenate([m_new] * (sub // LANES), axis=1))
                l_ref[j, rows] = alpha * l_ref[j, rows] + jnp.sum(pe, axis=-1, keepdims=True)
                m_ref[j, rows] = m_new
                ps.append(pe.astype(BF16))
                alphas.append(alpha)
            probs[r] = (jnp.concatenate(ps, axis=1), jnp.where(head_a, alphas[0], alphas[1]))
        for r in rs:
            rows = slice(r * sub, (r + 1) * sub)
            pcat, alpha = probs[r]
            acc_ref[rows] = acc_ref[rows] * alpha + _dot(pcat, vs)

    @pl.when(ki < qi)
    def _():
        for c in range(nsub):
            column(c, range(nsub), None)

    @pl.when(ki == qi)
    def _():
        for c in range(nsub):
            column(c, range(c, nsub), c)
        o_ref[0] = (acc_ref[...] / jnp.where(head_a, l_ref[0], l_ref[1])).astype(BF16)


def _mla_call(mq, mk, mv, t=2048, sub=512):
    b, s, _ = mq.shape
    n = s // t
    return pl.pallas_call(
        functools.partial(_mla_kernel, t=t, sub=sub),
        out_shape=jax.ShapeDtypeStruct((b, s, MLA_VW), BF16),
        grid=(b, MLA_HEADS // 2, n, n),
        in_specs=[pl.BlockSpec((1, t, 2 * LANES), lambda bb, p, i, j: (bb, i, p)),
                  pl.BlockSpec((1, t, 2 * LANES), lambda bb, p, i, j: (bb, jnp.minimum(j, i), p)),
                  pl.BlockSpec((1, t, LANES), lambda bb, p, i, j: (bb, jnp.minimum(j, i), p))],
        out_specs=pl.BlockSpec((1, t, LANES), lambda bb, p, i, j: (bb, i, p)),
        scratch_shapes=[pltpu.VMEM((2, t, LANES), F32), pltpu.VMEM((2, t, LANES), F32),
                        pltpu.VMEM((t, LANES), F32)],
        compiler_params=_cparams(("parallel", "parallel", "parallel", "arbitrary")),
        name="mla_attention",
    )(mq, mk, mv)


def _outproj_kernel(ya_ref, yb_ref, yc_ref, w_ref, x_ref, g_ref, lg_ref, lb_ref, o_ref):
    a, b = RET_W, RET_W + DIL_WD
    y = (_dot(ya_ref[0], w_ref[0, 0:a, :]) + _dot(yb_ref[0], w_ref[0, a:b, :])
         + _dot(yc_ref[0], w_ref[0, b:, :]))
    v = ALPHA * x_ref[0] + (1.0 + g_ref[0]) * y
    o_ref[0] = _layer_norm(v, lg_ref[...], lb_ref[...])


def _outproj_call(ya, yb, yc, w_out, layer, x, g1, lg, lb, tm=512):
    b, s, d = x.shape
    tok = lambda w: pl.BlockSpec((1, tm, w), lambda i, j: (i, j, 0))
    per_b = pl.BlockSpec((1, 1, d), lambda i, j: (i, 0, 0))
    full = lambda a: pl.BlockSpec(a.shape, lambda i, j: (0,) * a.ndim)
    return pl.pallas_call(
        _outproj_kernel,
        out_shape=jax.ShapeDtypeStruct((b, s, d), F32),
        grid=(b, s // tm),
        in_specs=[tok(ya.shape[-1]), tok(yb.shape[-1]), tok(yc.shape[-1]),
                  pl.BlockSpec((1,) + w_out.shape[1:], lambda i, j: (layer, 0, 0)),
                  tok(d), per_b, full(lg), full(lb)],
        out_specs=tok(d),
        compiler_params=_cparams(("parallel", "parallel")),
        name="out_proj_ln",
    )(ya, yb, yc, w_out, x, g1, lg, lb)


def _swiglu(h, w1_ref, w3_ref, w2_ref, chunk):
    ff = w1_ref.shape[1]
    y = None
    for a in range(0, ff, chunk):
        b = min(a + chunk, ff)
        mid = (_silu(_dot(h, w1_ref[:, a:b])) * _dot(h, w3_ref[:, a:b])).astype(BF16)
        part = _dot(mid, w2_ref[a:b, :])
        y = part if y is None else y + part
    return y


def _load_weights_bf16(w1_hbm, w3_hbm, w2_hbm, w1_ref, w3_ref, w2_ref, stage_cols, stage_rows, sem, chunk):
    ff = w1_ref.shape[1]
    jobs = []
    for a in range(0, ff, chunk):
        b = min(a + chunk, ff)
        for src, dst in ((w1_hbm, w1_ref), (w3_hbm, w3_ref)):
            jobs.append((src.at[:, a:b], lambda s, n=b - a: stage_cols.at[s, :, 0:n], dst.at[:, a:b]))
        jobs.append((w2_hbm.at[a:b, :], lambda s, n=b - a: stage_rows.at[s, 0:n, :], w2_ref.at[a:b, :]))

    def copy(k):
        src, stage, _ = jobs[k]
        return pltpu.make_async_copy(src, stage(k % 2), sem.at[k % 2])

    copy(0).start()
    for k in range(len(jobs)):
        if k + 1 < len(jobs):
            copy(k + 1).start()
        copy(k).wait()
        _, stage, dst = jobs[k]
        dst[...] = stage(k % 2)[...].astype(BF16)


def _ffn_kernel(x_ref, sc_ref, sh_ref, g_ref, w1_hbm, w3_hbm, w2_hbm, lg_ref, lb_ref, o_ref,
                w1_ref, w3_ref, w2_ref, stage_cols, stage_rows, sem, *, layer, chunk):
    @pl.when(pl.program_id(0) == 0)
    def _():
        _load_weights_bf16(w1_hbm.at[layer], w3_hbm.at[layer], w2_hbm.at[layer], w1_ref, w3_ref, w2_ref,
                           stage_cols, stage_rows, sem, chunk)

    x = x_ref[...]
    h = (x * (1.0 + sc_ref[0]) + sh_ref[0]).astype(BF16)
    y = _swiglu(h, w1_ref, w3_ref, w2_ref, chunk)
    o_ref[...] = _layer_norm(ALPHA * x + (1.0 + g_ref[0]) * y, lg_ref[...], lb_ref[...])


def _weight_scratch(d, ff, chunk):
    return [pltpu.VMEM((d, ff), BF16), pltpu.VMEM((d, ff), BF16), pltpu.VMEM((ff, d), BF16),
            pltpu.VMEM((2, d, chunk), F32), pltpu.VMEM((2, chunk, d), F32), pltpu.SemaphoreType.DMA((2,))]


def _ffn_call(x2d, sc, sh, g2, w1, w3, w2, layer, lg, lb, tiles_per_batch, tm, chunk):
    t, d = x2d.shape
    ff = w1.shape[2]
    per_b = pl.BlockSpec((1, 1, d), lambda i: (i // tiles_per_batch, 0, 0))
    vec = pl.BlockSpec((1, d), lambda i: (0, 0))
    hbm = pl.BlockSpec(memory_space=pl.ANY)
    return pl.pallas_call(
        functools.partial(_ffn_kernel, layer=layer, chunk=chunk),
        out_shape=jax.ShapeDtypeStruct((t, d), F32),
        grid=(t // tm,),
        in_specs=[pl.BlockSpec((tm, d), lambda i: (i, 0)), per_b, per_b, per_b, hbm, hbm, hbm, vec, vec],
        out_specs=pl.BlockSpec((tm, d), lambda i: (i, 0)),
        scratch_shapes=_weight_scratch(d, ff, chunk),
        compiler_params=_cparams(("arbitrary",), FFN_VMEM_LIMIT),
        name="ffn_dense",
    )(x2d, sc, sh, g2, w1, w3, w2, lg, lb)


ROUTE_G_OFF = 2
ROUTE_RANK_OFF = 4


def _router_kernel(x_ref, sc_ref, sh_ref, wr_ref, tri_ref, h_ref, r_ref, cnt_ref, run_ref):
    @pl.when(pl.program_id(0) == 0)
    def _():
        run_ref[...] = jnp.zeros_like(run_ref)

    h = x_ref[...] * (1.0 + sc_ref[0]) + sh_ref[0]
    h_ref[...] = h.astype(BF16)
    logits = jnp.dot(h, wr_ref[...], preferred_element_type=F32, precision=lax.Precision.HIGHEST)
    lane = _lane_ids(logits.shape)
    lg = jnp.where(lane < N_EXPERTS, logits, NEG_INF)
    m1 = jnp.max(lg, axis=-1, keepdims=True)
    i1 = jnp.min(jnp.where(lg == m1, lane, LANES), axis=-1, keepdims=True)
    lg2 = jnp.where(lane == i1, NEG_INF, lg)
    m2 = jnp.max(lg2, axis=-1, keepdims=True)
    i2 = jnp.min(jnp.where(lg2 == m2, lane, LANES), axis=-1, keepdims=True)
    e2 = jnp.exp(m2 - m1)
    den = 1.0 + e2
    chosen = (lane == i1) | (lane == i2)
    cum = _dot(tri_ref[...], chosen.astype(BF16)) + run_ref[...]
    rank1 = jnp.sum(jnp.where(lane == i1, cum, 0.0), axis=-1, keepdims=True) - 1.0
    rank2 = jnp.sum(jnp.where(lane == i2, cum, 0.0), axis=-1, keepdims=True) - 1.0
    run_ref[...] = cum[cum.shape[0] - 1:, :]
    cnt_ref[...] = cum[cum.shape[0] - 1:, :]
    out = jnp.where(lane == 0, i1.astype(F32), 0.0)
    out = jnp.where(lane == 1, i2.astype(F32), out)
    out = jnp.where(lane == ROUTE_G_OFF, 1.0 / den, out)
    out = jnp.where(lane == ROUTE_G_OFF + 1, e2 / den, out)
    out = jnp.where(lane == ROUTE_RANK_OFF, rank1, out)
    out = jnp.where(lane == ROUTE_RANK_OFF + 1, rank2, out)
    r_ref[...] = out


def _router_call(x2d, sc, sh, wr_p, tiles_per_batch, tm):
    t, d = x2d.shape
    per_b = pl.BlockSpec((1, 1, d), lambda i: (i // tiles_per_batch, 0, 0))
    tri = (jnp.arange(tm)[:, None] >= jnp.arange(tm)[None, :]).astype(BF16)
    return pl.pallas_call(
        _router_kernel,
        out_shape=[jax.ShapeDtypeStruct((t, d), BF16), jax.ShapeDtypeStruct((t, LANES), F32),
                   jax.ShapeDtypeStruct((1, LANES), F32)],
        grid=(t // tm,),
        in_specs=[pl.BlockSpec((tm, d), lambda i: (i, 0)), per_b, per_b,
                  pl.BlockSpec(wr_p.shape, lambda i: (0, 0)), pl.BlockSpec((tm, tm), lambda i: (0, 0))],
        out_specs=[pl.BlockSpec((tm, d), lambda i: (i, 0)), pl.BlockSpec((tm, LANES), lambda i: (i, 0)),
                   pl.BlockSpec((1, LANES), lambda i: (0, 0))],
        scratch_shapes=[pltpu.VMEM((1, LANES), F32)],
        compiler_params=_cparams(("arbitrary",)),
        name="moe_router",
    )(x2d, sc, sh, wr_p, tri)


def _expert_kernel(te_ref, nu_ref, x_ref, w1_hbm, w3_hbm, w2_hbm, o_ref,
                   w1_ref, w3_ref, w2_ref, stage_cols, stage_rows, sem, *, layer, chunk):
    i = pl.program_id(0)
    used = i < nu_ref[0]
    e = te_ref[i]
    new_expert = (i == 0) | (e != te_ref[jnp.maximum(i - 1, 0)])

    @pl.when(used & new_expert)
    def _():
        _load_weights_bf16(w1_hbm.at[layer, e], w3_hbm.at[layer, e], w2_hbm.at[layer, e], w1_ref, w3_ref, w2_ref,
                           stage_cols, stage_rows, sem, chunk)

    @pl.when(used)
    def _():
        o_ref[...] = _swiglu(x_ref[...], w1_ref, w3_ref, w2_ref, chunk).astype(o_ref.dtype)

    @pl.when(jnp.logical_not(used))
    def _():
        o_ref[...] = jnp.zeros_like(o_ref)


def _expert_call(tile_expert, n_used, xs, w1, w3, w2, layer, tm, chunk):
    p, d = xs.shape
    ff = w1.shape[3]
    hbm = pl.BlockSpec(memory_space=pl.ANY)
    return pl.pallas_call(
        functools.partial(_expert_kernel, layer=layer, chunk=chunk),
        out_shape=jax.ShapeDtypeStruct((p, d), BF16),
        grid_spec=pltpu.PrefetchScalarGridSpec(
            num_scalar_prefetch=2,
            grid=(p // tm,),
            in_specs=[pl.BlockSpec((tm, d), lambda i, te, nu: (i, 0)), hbm, hbm, hbm],
            out_specs=pl.BlockSpec((tm, d), lambda i, te, nu: (i, 0)),
            scratch_shapes=_weight_scratch(d, ff, chunk)),
        compiler_params=_cparams(("arbitrary",), FFN_VMEM_LIMIT),
        name="moe_experts",
    )(tile_expert, n_used, xs, w1, w3, w2)


def _combine_kernel(x_ref, ya_ref, yb_ref, r_ref, g_ref, lg_ref, lb_ref, o_ref):
    r = r_ref[...]
    lane = _lane_ids(r.shape)
    ga = jnp.sum(jnp.where(lane == ROUTE_G_OFF, r, 0.0), axis=-1, keepdims=True)
    gb = jnp.sum(jnp.where(lane == ROUTE_G_OFF + 1, r, 0.0), axis=-1, keepdims=True)
    y = ga * ya_ref[...].astype(F32) + gb * yb_ref[...].astype(F32)
    v = ALPHA * x_ref[...] + (1.0 + g_ref[0]) * y
    o_ref[...] = _layer_norm(v, lg_ref[...], lb_ref[...])


def _combine_call(x2d, ya, yb, route, g2, lg, lb, tiles_per_batch, tm):
    t, d = x2d.shape
    tok = pl.BlockSpec((tm, d), lambda i: (i, 0))
    per_b = pl.BlockSpec((1, 1, d), lambda i: (i // tiles_per_batch, 0, 0))
    vec = pl.BlockSpec((1, d), lambda i: (0, 0))
    return pl.pallas_call(
        _combine_kernel,
        out_shape=jax.ShapeDtypeStruct((t, d), F32),
        grid=(t // tm,),
        in_specs=[tok, tok, tok, pl.BlockSpec((tm, LANES), lambda i: (i, 0)), per_b, vec, vec],
        out_specs=tok,
        compiler_params=_cparams(("parallel",)),
        name="moe_combine_ln",
    )(x2d, ya, yb, route, g2, lg, lb)


def _dispatch_plan(route, counts, tm):
    t = route.shape[0]
    counts = counts[0, :N_EXPERTS].astype(jnp.int32)
    padded = ((counts + tm - 1) // tm) * tm
    ends = jnp.cumsum(padded)
    starts = ends - padded
    e = route[:, 0:2].astype(jnp.int32)
    rank = route[:, ROUTE_RANK_OFF:ROUTE_RANK_OFF + 2].astype(jnp.int32)
    dest = starts[e] + rank
    n_rows = 2 * t + N_EXPERTS * tm
    tok = jnp.arange(t, dtype=jnp.int32)
    row_token = jnp.zeros((n_rows,), jnp.int32).at[dest.T.reshape(-1)].set(
        jnp.concatenate([tok, tok]), mode="promise_in_bounds", unique_indices=True)
    tile_start = jnp.arange(n_rows // tm, dtype=jnp.int32) * tm
    tile_expert = jnp.minimum(jnp.sum((tile_start[:, None] >= ends[None, :]).astype(jnp.int32), axis=1),
                              N_EXPERTS - 1)
    n_used = (ends[-1] // tm).astype(jnp.int32).reshape(1)
    return row_token, dest, tile_expert, n_used


def _moe_layer(x2d, sc, sh, g2, wr_p, w1, w3, w2, layer, lg, lb, tiles_per_batch, tm, tme, tf):
    h, route, counts = _router_call(x2d, sc, sh, wr_p, tiles_per_batch, tm)
    row_token, dest, tile_expert, n_used = _dispatch_plan(route, counts, tme)
    xs = h.at[row_token].get(mode="promise_in_bounds")
    ys = _expert_call(tile_expert, n_used, xs, w1, w3, w2, layer, tme, tf)
    ya = ys.at[dest[:, 0]].get(mode="promise_in_bounds")
    yb = ys.at[dest[:, 1]].get(mode="promise_in_bounds")
    return _combine_call(x2d, ya, yb, route, g2, lg, lb, tiles_per_batch, tm)


def _trig_kernel(a_ref, c_ref, s_ref):
    a = a_ref[...]
    c_ref[...] = jnp.cos(a)
    s_ref[...] = jnp.sin(a)


def _trig_call(ang2d):
    n = ang2d.shape[0]
    tr = math.gcd(n, 1024)
    spec = pl.BlockSpec((tr, LANES), lambda i: (i, 0))
    return pl.pallas_call(
        _trig_kernel,
        out_shape=[jax.ShapeDtypeStruct(ang2d.shape, F32)] * 2,
        grid=(n // tr,),
        in_specs=[spec],
        out_specs=[spec, spec],
        compiler_params=_cparams(("parallel",)),
        name="rope_trig",
    )(ang2d)


def _rope_tables(positions):
    pos = positions.astype(F32)[..., None]
    lane = jnp.arange(LANES)
    inv64 = ROPE_THETA ** (-jnp.arange(0, RET_D, 2, dtype=F32) / RET_D)
    invm = ROPE_THETA ** (-jnp.arange(0, MLA_ROPE, 2, dtype=F32) / MLA_ROPE)
    n64 = inv64.shape[0]
    ang = pos * jnp.concatenate([inv64, invm])
    cos, sin = _trig_call(ang.reshape(-1, LANES))
    cos, sin = cos.reshape(ang.shape), sin.reshape(ang.shape)
    cos64, sin64, cosm, sinm = cos[..., :n64], sin[..., :n64], cos[..., n64:], sin[..., n64:]
    sign64 = jnp.where((lane % 64) < 32, -1.0, 1.0)
    c64 = jnp.tile(cos64, (1, 1, LANES // 32))
    s64 = jnp.tile(sin64, (1, 1, LANES // 32)) * sign64
    in_rope = (lane >= KR_LANE) & (lane < KR_LANE + MLA_ROPE)
    signm = jnp.where(lane < KR_LANE + MLA_ROPE // 2, -1.0, 1.0)
    cm = jnp.where(in_rope, jnp.tile(cosm, (1, 1, LANES // 16)), 1.0)
    sm = jnp.where(in_rope, jnp.tile(sinm, (1, 1, LANES // 16)) * signm, 0.0)
    return c64, s64, cm, sm


def _prep_mixer_weights(w_in, w_uq, w_ukv):
    d = w_in.shape[0]
    kr_cols = jnp.zeros((d, LANES), w_in.dtype).at[:, KR_LANE:KR_LANE + MLA_ROPE].set(
        w_in[:, MLA_OFF + MLA_Q_RANK + MLA_KV_RANK:])
    w_in_p = jnp.concatenate([w_in[:, :MLA_OFF + MLA_Q_RANK + MLA_KV_RANK], kr_cols], axis=1).astype(BF16)
    uq = w_uq.reshape(MLA_Q_RANK, MLA_HEADS, MLA_NOPE + MLA_ROPE)
    wuq_p = jnp.pad(uq, ((0, 0), (0, 0), (0, LANES - MLA_NOPE - MLA_ROPE))).reshape(MLA_Q_RANK, MLA_PAD).astype(BF16)
    ukv = w_ukv.reshape(MLA_KV_RANK, MLA_HEADS, MLA_NOPE + MLA_V)
    wkn_p = jnp.pad(ukv[:, :, :MLA_NOPE], ((0, 0), (0, 0), (0, LANES - MLA_NOPE))).reshape(MLA_KV_RANK, MLA_PAD).astype(BF16)
    wv = ukv[:, :, MLA_NOPE:].reshape(MLA_KV_RANK, MLA_VW).astype(BF16)
    return w_in_p, wuq_p, wkn_p, wv


def kernel(x, c, positions, w_in, ret_gn_g, mla_qn_g, mla_kvn_g, w_uq, w_ukv, w_out, w_ada, b_ada, ln1_g, ln1_b, ln2_g, ln2_b, w1_dense, w3_dense, w2_dense, w_router, w1_moe, w3_moe, w2_moe):
    b, s, d = x.shape
    tabs = _rope_tables(positions)
    ret_tabs = _ret_tables()
    mod = _ada_call(c, w_ada, b_ada)
    tm_tok = 512
    tiles_per_batch = s // tm_tok
    w_out_b = _cast_call(w_out)
    for l in range(w_in.shape[0]):
        sh1, sc1, g1, sh2, sc2, g2 = [mod[l, :, j * d:(j + 1) * d].reshape(b, 1, d) for j in range(6)]
        w_in_p, wuq_p, wkn_p, wv = _prep_mixer_weights(w_in[l], w_uq[l], w_ukv[l])
        (rq, rk, rv, rg), dil_views, (mq, mk, mv) = _inproj_call(
            x, sc1, sh1, w_in_p, tabs, mla_qn_g[l].reshape(1, -1), mla_kvn_g[l].reshape(1, -1), wuq_p, wkn_p, wv)
        ya = _ret_call(rq, rk, rv, rg, ret_gn_g[l].reshape(1, -1), ret_tabs)
        yb = _dil_call(dil_views)
        yc = _mla_call(mq, mk, mv)
        x = _outproj_call(ya, yb, yc, w_out_b, l, x, g1, ln1_g[l].reshape(1, d), ln1_b[l].reshape(1, d))
        x2d = x.reshape(b * s, d)
        lg, lb = ln2_g[l].reshape(1, d), ln2_b[l].reshape(1, d)
        j = l // 2
        if l % 2 == 0:
            x2d = _ffn_call(x2d, sc2, sh2, g2, w1_dense, w3_dense, w2_dense, j, lg, lb, tiles_per_batch, tm_tok,
                            FF_CHUNK)
        else:
            wr_p = jnp.pad(w_router[j], ((0, 0), (0, LANES - N_EXPERTS)))
            x2d = _moe_layer(x2d, sc2, sh2, g2, wr_p, w1_moe, w3_moe, w2_moe, j, lg, lb, tiles_per_batch, tm_tok,
                             EXPERT_TILE, FF_CHUNK)
        x = x2d.reshape(b, s, d)
    return x
```

```python
import functools
import math

import jax
import jax.numpy as jnp
from jax import lax
from jax.experimental import pallas as pl
from jax.experimental.pallas import tpu as pltpu

D_MODEL = 1024
DEPTH = 4
RET_HEADS = 4
RET_D = 64
RET_CHUNK = 128
DIL_HEADS = 6
DIL_D = 64
DIL_DILATIONS = (1, 4, 16)
DIL_W = 128
MLA_HEADS = 6
MLA_Q_RANK = 384
MLA_KV_RANK = 256
MLA_NOPE = 64
MLA_ROPE = 32
MLA_V = 64
N_EXPERTS = 8
ROPE_THETA = 10000.0
LN_EPS = 1e-5
RMS_EPS = 1e-6
ALPHA = (2.0 * DEPTH) ** 0.25

LANES = 128
RET_W = RET_HEADS * RET_D
DIL_WD = DIL_HEADS * DIL_D
MLA_PAD = MLA_HEADS * LANES
MLA_VW = MLA_HEADS * MLA_V
RET_IN = 4 * RET_W
DIL_IN = 3 * DIL_WD
MLA_OFF = RET_IN + DIL_IN
D_IN_PAD = MLA_OFF + MLA_Q_RANK + MLA_KV_RANK + LANES
KR_LANE = MLA_NOPE

VMEM_LIMIT = 48 * 1024 * 1024
FFN_VMEM_LIMIT = 56 * 1024 * 1024
FF_CHUNK = 512
EXPERT_TILE = 512
CAST_BLOCK_BYTES = 4 * 1024 * 1024
ROW_GRANULE = 16
BF16 = jnp.bfloat16
F32 = jnp.float32
NEG_INF = float("-inf")
LOG2E = math.log2(math.e)
MLA_Q_SCALE = (MLA_NOPE + MLA_ROPE) ** -0.5 * LOG2E


def _cparams(sem, vmem_limit=VMEM_LIMIT):
    return pltpu.CompilerParams(dimension_semantics=sem, vmem_limit_bytes=vmem_limit)


def _dot(a, b):
    return jnp.dot(a, b, preferred_element_type=F32)


def _dot_nt(a, b):
    return lax.dot_general(a, b, (((1,), (1,)), ((), ())), preferred_element_type=F32)


def _dot_tn(a, b):
    return lax.dot_general(a, b, (((0,), (0,)), ((), ())), preferred_element_type=F32)


def _silu(x):
    return x * (1.0 / (1.0 + jnp.exp(-x)))


def _layer_norm(v, g, b):
    mu = jnp.mean(v, axis=-1, keepdims=True)
    d = v - mu
    var = jnp.mean(d * d, axis=-1, keepdims=True)
    return d * lax.rsqrt(var + LN_EPS) * g + b


def _lane_ids(shape):
    return lax.broadcasted_iota(jnp.int32, shape, len(shape) - 1)


def _rope_group(x, cos, sin_signed, first_half, half):
    fwd = pltpu.roll(x, LANES - half, 1)
    bwd = pltpu.roll(x, half, 1)
    return x * cos + jnp.where(first_half, fwd, bwd) * sin_signed


def _cast_kernel(w_ref, o_ref):
    o_ref[...] = w_ref[...].astype(BF16)


def _cast_call(w):
    cols = w.shape[-1]
    w2d = w.reshape(-1, cols)
    rows = w2d.shape[0]
    pack = 16
    tr = max(pack, CAST_BLOCK_BYTES // (4 * cols) // pack * pack)
    while rows % tr:
        tr -= pack
    spec = pl.BlockSpec((tr, cols), lambda i: (i, 0))
    out = pl.pallas_call(
        _cast_kernel,
        out_shape=jax.ShapeDtypeStruct(w2d.shape, BF16),
        grid=(rows // tr,),
        in_specs=[spec],
        out_specs=spec,
        compiler_params=_cparams(("parallel",)),
        name="cast_bf16",
    )(w2d)
    return out.reshape(w.shape)


def _ada_kernel(c_ref, w_ref, b_ref, o_ref):
    cond = _silu(c_ref[...])
    o_ref[0] = jnp.dot(cond, w_ref[0], preferred_element_type=F32,
                       precision=lax.Precision.HIGHEST) + b_ref[0]


def _ada_call(c, w_ada, b_ada):
    nl, d, n = w_ada.shape
    b = c.shape[0]
    tn = 1536
    return pl.pallas_call(
        _ada_kernel,
        out_shape=jax.ShapeDtypeStruct((nl, b, n), F32),
        grid=(nl, n // tn),
        in_specs=[pl.BlockSpec((b, d), lambda l, j: (0, 0)),
                  pl.BlockSpec((1, d, tn), lambda l, j: (l, 0, j)),
                  pl.BlockSpec((1, 1, tn), lambda l, j: (l, 0, j))],
        out_specs=pl.BlockSpec((1, b, tn), lambda l, j: (l, 0, j)),
        compiler_params=_cparams(("arbitrary", "arbitrary")),
        name="ada_mod",
    )(c, w_ada, b_ada.reshape(nl, 1, n))


def _inproj_kernel(x_ref, sc_ref, sh_ref, w_ref, c64_ref, s64_ref, cm_ref, sm_ref,
                   qg_ref, kvg_ref, wuq_ref, wkn_ref, wv_ref,
                   rq_ref, rk_ref, rv_ref, rg_ref,
                   dq1_ref, dk1_ref, dv1_ref, dq4_ref, dk4_ref, dv4_ref, dq16_ref, dk16_ref, dv16_ref,
                   mq_ref, mk_ref, mv_ref, scr_ref, *, tm):
    h = (x_ref[0] * (1.0 + sc_ref[0]) + sh_ref[0]).astype(BF16)
    c64 = c64_ref[0]
    s64 = s64_ref[0]
    cm = cm_ref[0]
    sm = sm_ref[0]
    lane = _lane_ids((1, LANES))
    first64 = (lane % 64) < 32
    firstm = lane < (KR_LANE + MLA_ROPE // 2)

    def rope64(zc):
        return _rope_group(zc, c64, s64, first64, 32)

    def ropem(zc):
        return _rope_group(zc, cm, sm, firstm, MLA_ROPE // 2)

    za = _dot(h, w_ref[:, 0:RET_IN])
    for j in range(RET_W // LANES):
        sl = slice(j * LANES, (j + 1) * LANES)
        rq_ref[0, :, sl] = rope64(za[:, j * LANES:(j + 1) * LANES]).astype(BF16)
        kc = za[:, RET_W + j * LANES:RET_W + (j + 1) * LANES]
        rk_ref[0, :, sl] = (rope64(kc) * (RET_D ** -0.5)).astype(BF16)
    rv_ref[0] = za[:, 2 * RET_W:3 * RET_W].astype(BF16)
    rg_ref[0] = za[:, 3 * RET_W:4 * RET_W]

    zb = _dot(h, w_ref[:, RET_IN:MLA_OFF])
    q_scale = (DIL_D ** -0.5) * LOG2E
    views = ((dq1_ref, dq4_ref, dq16_ref), (dk1_ref, dk4_ref, dk16_ref), (dv1_ref, dv4_ref, dv16_ref))
    for a, (n1, n4, n16) in enumerate(views):
        for j in range(DIL_WD // LANES):
            zc = zb[:, a * DIL_WD + j * LANES:a * DIL_WD + (j + 1) * LANES]
            if a == 0:
                zc = rope64(zc) * q_scale
            elif a == 1:
                zc = rope64(zc)
            scr_ref[j] = zc
            n1[0, :, j * LANES:(j + 1) * LANES] = zc.astype(BF16)
        for r, ref in ((4, n4), (16, n16)):
            for rho in range(r):
                for j in range(DIL_WD // LANES):
                    ref[0, :, rho * DIL_WD + j * LANES:rho * DIL_WD + (j + 1) * LANES] = (
                        scr_ref[j, pl.ds(rho, tm // r, stride=r), :].astype(BF16))

    zc = _dot(h, w_ref[:, MLA_OFF:D_IN_PAD])
    cq = zc[:, 0:MLA_Q_RANK]
    ckv = zc[:, MLA_Q_RANK:MLA_Q_RANK + MLA_KV_RANK]
    kr = ropem(zc[:, MLA_Q_RANK + MLA_KV_RANK:])
    cqn = (cq * lax.rsqrt(jnp.mean(cq * cq, axis=-1, keepdims=True) + RMS_EPS) * qg_ref[...]).astype(BF16)
    ckvn = (ckv * lax.rsqrt(jnp.mean(ckv * ckv, axis=-1, keepdims=True) + RMS_EPS) * kvg_ref[...]).astype(BF16)
    q = _dot(cqn, wuq_ref[...])
    kn = _dot(ckvn, wkn_ref[...])
    for hh in range(MLA_HEADS):
        sl = slice(hh * LANES, (hh + 1) * LANES)
        mq_ref[0, :, sl] = (ropem(q[:, hh * LANES:(hh + 1) * LANES]) * MLA_Q_SCALE).astype(BF16)
        mk_ref[0, :, sl] = (kn[:, hh * LANES:(hh + 1) * LANES] + kr).astype(BF16)
    mv_ref[0] = _dot(ckvn, wv_ref[...]).astype(BF16)


def _inproj_call(x, sc, sh, w_in_p, tabs, qg, kvg, wuq_p, wkn_p, wv, tm=512):
    b, s, d = x.shape
    c64, s64, cm, sm = tabs
    tok = lambda w: pl.BlockSpec((1, tm, w), lambda i, j: (i, j, 0))
    per_b = pl.BlockSpec((1, 1, d), lambda i, j: (i, 0, 0))
    full = lambda a: pl.BlockSpec(a.shape, lambda i, j: (0,) * a.ndim)
    outs = [(1, RET_W, BF16), (1, RET_W, BF16), (1, RET_W, BF16), (1, RET_W, F32)]
    outs += [(r, DIL_WD, BF16) for r in DIL_DILATIONS for _ in range(3)]
    outs += [(1, MLA_PAD, BF16), (1, MLA_PAD, BF16), (1, MLA_VW, BF16)]
    res = pl.pallas_call(
        functools.partial(_inproj_kernel, tm=tm),
        out_shape=[jax.ShapeDtypeStruct((b, s // r, r * w), dt) for r, w, dt in outs],
        grid=(b, s // tm),
        in_specs=[tok(d), per_b, per_b, full(w_in_p), tok(LANES), tok(LANES), tok(LANES), tok(LANES),
                  full(qg), full(kvg), full(wuq_p), full(wkn_p), full(wv)],
        out_specs=[pl.BlockSpec((1, tm // r, r * w), lambda i, j: (i, j, 0)) for r, w, _ in outs],
        scratch_shapes=[pltpu.VMEM((DIL_WD // LANES, tm, LANES), F32)],
        compiler_params=_cparams(("parallel", "parallel")),
        name="in_proj",
    )(x, sc, sh, w_in_p, c64, s64, cm, sm, qg, kvg, wuq_p, wkn_p, wv)
    ret = res[0:4]
    dil = [res[4 + 3 * i:7 + 3 * i] for i in range(len(DIL_DILATIONS))]
    mla = res[4 + 3 * len(DIL_DILATIONS):]
    return ret, dil, mla


def _ret_kernel(q_ref, k_ref, v_ref, g_ref, gn_ref, intra_ref, xi_ref, zeta_ref, dmat_ref, bmask_ref,
                o_ref, state_ref, *, chunks):
    @pl.when(pl.program_id(1) == 0)
    def _():
        state_ref[...] = jnp.zeros_like(state_ref)

    lane = _lane_ids((1, LANES))
    head_a = lane < RET_D
    c = RET_CHUNK
    npair = RET_W // LANES
    items = [(ci, p) for ci in range(chunks) for p in range(npair)]
    view = lambda ref, ci, p: ref[0, ci * c:(ci + 1) * c, p * LANES:(p + 1) * LANES]

    inner, kv_inc = {}, {}
    for ci, p in items:
        cols = slice(p * LANES, (p + 1) * LANES)
        q, k, v = view(q_ref, ci, p), view(k_ref, ci, p), view(v_ref, ci, p)
        zero = jnp.zeros_like(q)
        s_a = _dot_nt(jnp.where(head_a, q, zero), k) * intra_ref[2 * p]
        s_b = _dot_nt(jnp.where(head_a, zero, q), k) * intra_ref[2 * p + 1]
        s_cat = jnp.concatenate([s_a, s_b], axis=1).astype(BF16)
        v_stack = jnp.concatenate([jnp.where(head_a, v, zero), jnp.where(head_a, zero, v)], axis=0)
        inner[(ci, p)] = _dot(s_cat, v_stack)
        kz = (k.astype(F32) * zeta_ref[:, cols]).astype(BF16)
        kv_inc[(ci, p)] = _dot_tn(kz, v) * bmask_ref[...]

    outs = {}
    for p in range(npair):
        cols = slice(p * LANES, (p + 1) * LANES)
        state = state_ref[p]
        for ci in range(chunks):
            cross = _dot(view(q_ref, ci, p), state.astype(BF16)) * xi_ref[:, cols]
            outs[(ci, p)] = inner[(ci, p)] + cross
            state = state * dmat_ref[p] + kv_inc[(ci, p)]
        state_ref[p] = state

    inv_n = 1.0 / RET_D
    for ci, p in items:
        cols = slice(p * LANES, (p + 1) * LANES)
        o = outs[(ci, p)]
        mu = jnp.where(head_a,
                       jnp.sum(jnp.where(head_a, o, 0.0), axis=-1, keepdims=True),
                       jnp.sum(jnp.where(head_a, 0.0, o), axis=-1, keepdims=True)) * inv_n
        d = o - mu
        dd = d * d
        var = jnp.where(head_a,
                        jnp.sum(jnp.where(head_a, dd, 0.0), axis=-1, keepdims=True),
                        jnp.sum(jnp.where(head_a, 0.0, dd), axis=-1, keepdims=True)) * inv_n
        on = d * lax.rsqrt(var + LN_EPS) * gn_ref[:, cols]
        o_ref[0, ci * c:(ci + 1) * c, cols] = (_silu(view(g_ref, ci, p)) * on).astype(BF16)


def _ret_tables():
    c = RET_CHUNK
    log_g = jnp.log(1.0 - 2.0 ** (-5.0 - jnp.arange(RET_HEADS, dtype=F32)))
    idx = jnp.arange(c, dtype=F32)
    diff = idx[:, None] - idx[None, :]
    intra = jnp.where(diff >= 0, jnp.exp(jnp.maximum(diff, 0.0) * log_g[:, None, None]), 0.0)
    xi = jnp.exp((idx[:, None] + 1.0) * log_g[None, :])
    zeta = jnp.exp((c - 1.0 - idx[:, None]) * log_g[None, :])
    decay = jnp.exp(c * log_g)
    xi_l = jnp.repeat(xi, RET_D, axis=1)
    zeta_l = jnp.repeat(zeta, RET_D, axis=1)
    r = jnp.arange(LANES)
    same = (r[:, None] // RET_D) == (r[None, :] // RET_D)
    bmask = same.astype(F32)
    dec_l = jnp.repeat(decay, RET_D).reshape(RET_W // LANES, 1, LANES)
    dmat = bmask[None] * dec_l
    return intra, xi_l, zeta_l, dmat, bmask


def _ret_call(rq, rk, rv, rg, gn, tables, chunks=8):
    b, s, w = rq.shape
    tm = chunks * RET_CHUNK
    intra, xi_l, zeta_l, dmat, bmask = tables
    tok = pl.BlockSpec((1, tm, w), lambda i, j: (i, j, 0))
    full = lambda a: pl.BlockSpec(a.shape, lambda i, j: (0,) * a.ndim)
    return pl.pallas_call(
        functools.partial(_ret_kernel, chunks=chunks),
        out_shape=jax.ShapeDtypeStruct((b, s, w), BF16),
        grid=(b, s // tm),
        in_specs=[tok, tok, tok, tok, full(gn), full(intra), full(xi_l), full(zeta_l), full(dmat), full(bmask)],
        out_specs=tok,
        scratch_shapes=[pltpu.VMEM((w // LANES, LANES, LANES), F32)],
        compiler_params=_cparams(("parallel", "arbitrary")),
        name="retention",
    )(rq, rk, rv, rg, gn, intra, xi_l, zeta_l, dmat, bmask)


def _dil_band(first_block):
    w = DIL_W
    qi = lax.broadcasted_iota(jnp.int32, (w, 2 * w), 0)
    kj = lax.broadcasted_iota(jnp.int32, (w, 2 * w), 1)
    dist = w + qi - kj
    band = (dist >= 0) & (dist <= w)
    if first_block is False:
        return band
    return band & (jnp.logical_not(first_block) | (kj >= w))


DIL_GROUP = 2


def _dil_attend(items, valid_of):
    head_a = _lane_ids((1, LANES)) < DIL_D
    scores = []
    for idx, (q, kcat, _) in enumerate(items):
        zero = jnp.zeros_like(q)
        valid = valid_of(idx)
        scores.append([jnp.where(valid, _dot_nt(qh, kcat), NEG_INF)
                       for qh in (jnp.where(head_a, q, zero), jnp.where(head_a, zero, q))])
    soft = []
    for scs in scores:
        ms, ls, ps = [], [], []
        for sc in scs:
            m = jnp.max(sc, axis=-1, keepdims=True)
            pe = jnp.exp2(sc - m)
            ms.append(m)
            ls.append(jnp.sum(pe, axis=-1, keepdims=True))
            ps.append(pe.astype(BF16))
        soft.append((ms, ls, jnp.concatenate(ps, axis=1)))
    out = []
    for (_, _, vcat), (ms, ls, pcat) in zip(items, soft):
        zv = jnp.zeros_like(vcat)
        v_stack = jnp.concatenate([jnp.where(head_a, vcat, zv), jnp.where(head_a, zv, vcat)], axis=0)
        out.append((_dot(pcat, v_stack), ms, ls))
    return out


def _dil_normalise(acc, ms, ls):
    head_a = _lane_ids((1, LANES)) < DIL_D
    o = acc * jnp.where(head_a, 1.0 / ls[0], 1.0 / ls[1])
    lse = jnp.where(head_a, ms[0] + jnp.log2(ls[0]), ms[1] + jnp.log2(ls[1]))
    return o, lse


def _dil_partial_kernel(q_ref, kp_ref, kc_ref, vp_ref, vc_ref, o_ref, lse_ref, *, r):
    valid = _dil_band(pl.program_id(1) == 0)
    npair = DIL_WD // LANES
    for g0 in range(0, r, DIL_GROUP):
        rhos = range(g0, min(g0 + DIL_GROUP, r))
        items = []
        for rho in rhos:
            for p in range(npair):
                cols = slice(rho * DIL_WD + p * LANES, rho * DIL_WD + (p + 1) * LANES)
                items.append((q_ref[0, :, cols],
                              jnp.concatenate([kp_ref[0, :, cols], kc_ref[0, :, cols]], axis=0),
                              jnp.concatenate([vp_ref[0, :, cols], vc_ref[0, :, cols]], axis=0)))
        res = _dil_attend(items, lambda idx: valid)
        for gi, rho in enumerate(rhos):
            rows = pl.ds(rho, DIL_W, stride=r)
            for p in range(npair):
                o, lse = _dil_normalise(*res[gi * npair + p])
                o_ref[0, p, rows, :] = o
                lse_ref[0, p, rows, :] = lse


def _dil_final_kernel(q_ref, k_ref, kh_ref, v_ref, vh_ref, o4_ref, lse4_ref, o16_ref, lse16_ref, o_ref, *, nblk):
    w = DIL_W
    npair = DIL_WD // LANES
    band_first = _dil_band(pl.program_id(1) == 0)
    band = _dil_band(False)

    for g0 in range(0, nblk, DIL_GROUP):
        blocks = range(g0, min(g0 + DIL_GROUP, nblk))
        items = []
        for i in blocks:
            rows = slice(i * w, (i + 1) * w)
            for p in range(npair):
                cols = slice(p * LANES, (p + 1) * LANES)
                if i == 0:
                    kcat = jnp.concatenate([kh_ref[0, :, cols], k_ref[0, rows, cols]], axis=0)
                    vcat = jnp.concatenate([vh_ref[0, :, cols], v_ref[0, rows, cols]], axis=0)
                else:
                    kcat = k_ref[0, (i - 1) * w:(i + 1) * w, cols]
                    vcat = v_ref[0, (i - 1) * w:(i + 1) * w, cols]
                items.append((q_ref[0, rows, cols], kcat, vcat))
        res = _dil_attend(items, lambda idx: band_first if blocks[idx // npair] == 0 else band)
        for gi, i in enumerate(blocks):
            rows = slice(i * w, (i + 1) * w)
            for p in range(npair):
                o1, lse1 = _dil_normalise(*res[gi * npair + p])
                lse4 = lse4_ref[0, p, rows, :]
                lse16 = lse16_ref[0, p, rows, :]
                top = jnp.maximum(jnp.maximum(lse1, lse4), lse16)
                w1, w4, w16 = jnp.exp2(lse1 - top), jnp.exp2(lse4 - top), jnp.exp2(lse16 - top)
                num = w1 * o1 + w4 * o4_ref[0, p, rows, :] + w16 * o16_ref[0, p, rows, :]
                o_ref[0, rows, p * LANES:(p + 1) * LANES] = (num / (w1 + w4 + w16)).astype(BF16)


def _dil_call(views, nblk=4):
    (q1, k1, v1) = views[0]
    b, s, w = q1.shape
    npair = w // LANES
    partial = []
    for r, (q, k, v) in zip(DIL_DILATIONS[1:], views[1:]):
        cur = pl.BlockSpec((1, DIL_W, r * w), lambda bb, n: (bb, n, 0))
        prev = pl.BlockSpec((1, DIL_W, r * w), lambda bb, n: (bb, jnp.maximum(n - 1, 0), 0))
        partial += pl.pallas_call(
            functools.partial(_dil_partial_kernel, r=r),
            out_shape=[jax.ShapeDtypeStruct((b, npair, s, LANES), F32)] * 2,
            grid=(b, s // (r * DIL_W)),
            in_specs=[cur, prev, cur, prev, cur],
            out_specs=[pl.BlockSpec((1, npair, r * DIL_W, LANES), lambda bb, n: (bb, 0, n, 0))] * 2,
            compiler_params=_cparams(("parallel", "arbitrary")),
            name=f"dilated_r{r}",
        )(q, k, k, v, v)
    tm = nblk * DIL_W
    tok = lambda width: pl.BlockSpec((1, tm, width), lambda bb, n: (bb, n, 0))
    halo = pl.BlockSpec((1, DIL_W, w), lambda bb, n: (bb, jnp.maximum(n * nblk - 1, 0), 0))
    accs = pl.BlockSpec((1, npair, tm, LANES), lambda bb, n: (bb, 0, n, 0))
    return pl.pallas_call(
        functools.partial(_dil_final_kernel, nblk=nblk),
        out_shape=jax.ShapeDtypeStruct((b, s, w), BF16),
        grid=(b, s // tm),
        in_specs=[tok(w), tok(w), halo, tok(w), halo, accs, accs, accs, accs],
        out_specs=tok(w),
        compiler_params=_cparams(("parallel", "arbitrary")),
        name="dilated_r1_merge",
    )(q1, k1, k1, v1, v1, *partial)


def _mla_kernel(q_ref, k_ref, v_ref, o_ref, m_ref, l_ref, acc_ref, *, t, sub):
    qi = pl.program_id(2)
    ki = pl.program_id(3)
    head_a = _lane_ids((1, LANES)) < MLA_V
    nsub = t // sub

    @pl.when(ki == 0)
    def _():
        m_ref[...] = jnp.full_like(m_ref, NEG_INF)
        l_ref[...] = jnp.zeros_like(l_ref)
        acc_ref[...] = jnp.zeros_like(acc_ref)

    def v_stack(c):
        v = v_ref[0, c * sub:(c + 1) * sub]
        zv = jnp.zeros_like(v)
        return jnp.concatenate([jnp.where(head_a, v, zv), jnp.where(head_a, zv, v)], axis=0)

    def column(c, rs, diag_r):
        vs = v_stack(c)
        scores = {}
        for r in rs:
            rows = slice(r * sub, (r + 1) * sub)
            for j in range(2):
                q = q_ref[0, rows, j * LANES:(j + 1) * LANES]
                k = k_ref[0, c * sub:(c + 1) * sub, j * LANES:(j + 1) * LANES]
                s = _dot_nt(q, k)
                if r == diag_r:
                    row = lax.broadcasted_iota(jnp.int32, (sub, sub), 0)
                    colk = lax.broadcasted_iota(jnp.int32, (sub, sub), 1)
                    s = jnp.where(colk <= row, s, NEG_INF)
                scores[(r, j)] = s
        probs = {}
        for r in rs:
            rows = slice(r * sub, (r + 1) * sub)
            ps, alphas = [], []
            for j in range(2):
                s = scores[(r, j)]
                m_old = m_ref[j, rows]
                m_new = jnp.maximum(m_old, jnp.max(s, axis=-1, keepdims=True))
                alpha = jnp.exp2(m_old - m_new)
                pe = jnp.exp2(s - jnp.concatenate([m_new] * (sub // LANES), axis=1))
                l_ref[j, rows] = alpha * l_ref[j, rows] + jnp.sum(pe, axis=-1, keepdims=True)
                m_ref[j, rows] = m_new
                ps.append(pe.astype(BF16))
                alphas.append(alpha)
            probs[r] = (jnp.concatenate(ps, axis=1), jnp.where(head_a, alphas[0], alphas[1]))
        for r in rs:
            rows = slice(r * sub, (r + 1) * sub)
            pcat, alpha = probs[r]
            acc_ref[rows] = acc_ref[rows] * alpha + _dot(pcat, vs)

    @pl.when(ki < qi)
    def _():
        for c in range(nsub):
            column(c, range(nsub), None)

    @pl.when(ki == qi)
    def _():
        for c in range(nsub):
            column(c, range(c, nsub), c)
        o_ref[0] = (acc_ref[...] / jnp.where(head_a, l_ref[0], l_ref[1])).astype(BF16)


def _mla_call(mq, mk, mv, t=2048, sub=512):
    b, s, _ = mq.shape
    n = s // t
    return pl.pallas_call(
        functools.partial(_mla_kernel, t=t, sub=sub),
        out_shape=jax.ShapeDtypeStruct((b, s, MLA_VW), BF16),
        grid=(b, MLA_HEADS // 2, n, n),
        in_specs=[pl.BlockSpec((1, t, 2 * LANES), lambda bb, p, i, j: (bb, i, p)),
                  pl.BlockSpec((1, t, 2 * LANES), lambda bb, p, i, j: (bb, jnp.minimum(j, i), p)),
                  pl.BlockSpec((1, t, LANES), lambda bb, p, i, j: (bb, jnp.minimum(j, i), p))],
        out_specs=pl.BlockSpec((1, t, LANES), lambda bb, p, i, j: (bb, i, p)),
        scratch_shapes=[pltpu.VMEM((2, t, LANES), F32), pltpu.VMEM((2, t, LANES), F32),
                        pltpu.VMEM((t, LANES), F32)],
        compiler_params=_cparams(("parallel", "parallel", "parallel", "arbitrary")),
        name="mla_attention",
    )(mq, mk, mv)


def _outproj_kernel(ya_ref, yb_ref, yc_ref, w_ref, x_ref, g_ref, lg_ref, lb_ref, o_ref):
    a, b = RET_W, RET_W + DIL_WD
    y = (_dot(ya_ref[0], w_ref[0, 0:a, :]) + _dot(yb_ref[0], w_ref[0, a:b, :])
         + _dot(yc_ref[0], w_ref[0, b:, :]))
    v = ALPHA * x_ref[0] + (1.0 + g_ref[0]) * y
    o_ref[0] = _layer_norm(v, lg_ref[...], lb_ref[...])


def _outproj_call(ya, yb, yc, w_out, layer, x, g1, lg, lb, tm=512):
    b, s, d = x.shape
    tok = lambda w: pl.BlockSpec((1, tm, w), lambda i, j: (i, j, 0))
    per_b = pl.BlockSpec((1, 1, d), lambda i, j: (i, 0, 0))
    full = lambda a: pl.BlockSpec(a.shape, lambda i, j: (0,) * a.ndim)
    return pl.pallas_call(
        _outproj_kernel,
        out_shape=jax.ShapeDtypeStruct((b, s, d), F32),
        grid=(b, s // tm),
        in_specs=[tok(ya.shape[-1]), tok(yb.shape[-1]), tok(yc.shape[-1]),
                  pl.BlockSpec((1,) + w_out.shape[1:], lambda i, j: (layer, 0, 0)),
                  tok(d), per_b, full(lg), full(lb)],
        out_specs=tok(d),
        compiler_params=_cparams(("parallel", "parallel")),
        name="out_proj_ln",
    )(ya, yb, yc, w_out, x, g1, lg, lb)


def _swiglu(h, w1_ref, w3_ref, w2_ref, chunk):
    ff = w1_ref.shape[1]
    y = None
    for a in range(0, ff, chunk):
        b = min(a + chunk, ff)
        mid = (_silu(_dot(h, w1_ref[:, a:b])) * _dot(h, w3_ref[:, a:b])).astype(BF16)
        part = _dot(mid, w2_ref[a:b, :])
        y = part if y is None else y + part
    return y


def _load_weights_bf16(w1_hbm, w3_hbm, w2_hbm, w1_ref, w3_ref, w2_ref, stage_cols, stage_rows, sem, chunk):
    ff = w1_ref.shape[1]
    jobs = []
    for a in range(0, ff, chunk):
        b = min(a + chunk, ff)
        for src, dst in ((w1_hbm, w1_ref), (w3_hbm, w3_ref)):
            jobs.append((src.at[:, a:b], lambda s, n=b - a: stage_cols.at[s, :, 0:n], dst.at[:, a:b]))
        jobs.append((w2_hbm.at[a:b, :], lambda s, n=b - a: stage_rows.at[s, 0:n, :], w2_ref.at[a:b, :]))

    def copy(k):
        src, stage, _ = jobs[k]
        return pltpu.make_async_copy(src, stage(k % 2), sem.at[k % 2])

    copy(0).start()
    for k in range(len(jobs)):
        if k + 1 < len(jobs):
            copy(k + 1).start()
        copy(k).wait()
        _, stage, dst = jobs[k]
        dst[...] = stage(k % 2)[...].astype(BF16)


def _ffn_kernel(x_ref, sc_ref, sh_ref, g_ref, w1_hbm, w3_hbm, w2_hbm, lg_ref, lb_ref, o_ref,
                w1_ref, w3_ref, w2_ref, stage_cols, stage_rows, sem, *, layer, chunk):
    @pl.when(pl.program_id(0) == 0)
    def _():
        _load_weights_bf16(w1_hbm.at[layer], w3_hbm.at[layer], w2_hbm.at[layer], w1_ref, w3_ref, w2_ref,
                           stage_cols, stage_rows, sem, chunk)

    x = x_ref[...]
    h = (x * (1.0 + sc_ref[0]) + sh_ref[0]).astype(BF16)
    y = _swiglu(h, w1_ref, w3_ref, w2_ref, chunk)
    o_ref[...] = _layer_norm(ALPHA * x + (1.0 + g_ref[0]) * y, lg_ref[...], lb_ref[...])


def _weight_scratch(d, ff, chunk):
    return [pltpu.VMEM((d, ff), BF16), pltpu.VMEM((d, ff), BF16), pltpu.VMEM((ff, d), BF16),
            pltpu.VMEM((2, d, chunk), F32), pltpu.VMEM((2, chunk, d), F32), pltpu.SemaphoreType.DMA((2,))]


def _ffn_call(x2d, sc, sh, g2, w1, w3, w2, layer, lg, lb, tiles_per_batch, tm, chunk):
    t, d = x2d.shape
    ff = w1.shape[2]
    per_b = pl.BlockSpec((1, 1, d), lambda i: (i // tiles_per_batch, 0, 0))
    vec = pl.BlockSpec((1, d), lambda i: (0, 0))
    hbm = pl.BlockSpec(memory_space=pl.ANY)
    return pl.pallas_call(
        functools.partial(_ffn_kernel, layer=layer, chunk=chunk),
        out_shape=jax.ShapeDtypeStruct((t, d), F32),
        grid=(t // tm,),
        in_specs=[pl.BlockSpec((tm, d), lambda i: (i, 0)), per_b, per_b, per_b, hbm, hbm, hbm, vec, vec],
        out_specs=pl.BlockSpec((tm, d), lambda i: (i, 0)),
        scratch_shapes=_weight_scratch(d, ff, chunk),
        compiler_params=_cparams(("arbitrary",), FFN_VMEM_LIMIT),
        name="ffn_dense",
    )(x2d, sc, sh, g2, w1, w3, w2, lg, lb)


ROUTE_G_OFF = 2
ROUTE_RANK_OFF = 4


def _router_kernel(x_ref, sc_ref, sh_ref, wr_ref, tri_ref, h_ref, r_ref, cnt_ref):
    h = x_ref[...] * (1.0 + sc_ref[0]) + sh_ref[0]
    h_ref[...] = h.astype(BF16)
    logits = jnp.dot(h, wr_ref[...], preferred_element_type=F32, precision=lax.Precision.HIGHEST)
    lane = _lane_ids(logits.shape)
    lg = jnp.where(lane < N_EXPERTS, logits, NEG_INF)
    m1 = jnp.max(lg, axis=-1, keepdims=True)
    i1 = jnp.min(jnp.where(lg == m1, lane, LANES), axis=-1, keepdims=True)
    lg2 = jnp.where(lane == i1, NEG_INF, lg)
    m2 = jnp.max(lg2, axis=-1, keepdims=True)
    i2 = jnp.min(jnp.where(lg2 == m2, lane, LANES), axis=-1, keepdims=True)
    e2 = jnp.exp(m2 - m1)
    den = 1.0 + e2
    chosen = (lane == i1) | (lane == i2)
    cum = _dot(tri_ref[...], chosen.astype(BF16))
    rank1 = jnp.sum(jnp.where(lane == i1, cum, 0.0), axis=-1, keepdims=True) - 1.0
    rank2 = jnp.sum(jnp.where(lane == i2, cum, 0.0), axis=-1, keepdims=True) - 1.0
    cnt_ref[0] = cum[cum.shape[0] - 1:, :]
    out = jnp.where(lane == 0, i1.astype(F32), 0.0)
    out = jnp.where(lane == 1, i2.astype(F32), out)
    out = jnp.where(lane == ROUTE_G_OFF, 1.0 / den, out)
    out = jnp.where(lane == ROUTE_G_OFF + 1, e2 / den, out)
    out = jnp.where(lane == ROUTE_RANK_OFF, rank1, out)
    out = jnp.where(lane == ROUTE_RANK_OFF + 1, rank2, out)
    r_ref[...] = out


def _router_call(x2d, sc, sh, wr_p, tiles_per_batch, tm):
    t, d = x2d.shape
    per_b = pl.BlockSpec((1, 1, d), lambda i: (i // tiles_per_batch, 0, 0))
    tri = (jnp.arange(tm)[:, None] >= jnp.arange(tm)[None, :]).astype(BF16)
    return pl.pallas_call(
        _router_kernel,
        out_shape=[jax.ShapeDtypeStruct((t, d), BF16), jax.ShapeDtypeStruct((t, LANES), F32),
                   jax.ShapeDtypeStruct((t // tm, 1, LANES), F32)],
        grid=(t // tm,),
        in_specs=[pl.BlockSpec((tm, d), lambda i: (i, 0)), per_b, per_b,
                  pl.BlockSpec(wr_p.shape, lambda i: (0, 0)), pl.BlockSpec((tm, tm), lambda i: (0, 0))],
        out_specs=[pl.BlockSpec((tm, d), lambda i: (i, 0)), pl.BlockSpec((tm, LANES), lambda i: (i, 0)),
                   pl.BlockSpec((1, 1, LANES), lambda i: (i, 0, 0))],
        compiler_params=_cparams(("parallel",)),
        name="moe_router",
    )(x2d, sc, sh, wr_p, tri)


def _expert_kernel(te_ref, nu_ref, x_ref, w1_hbm, w3_hbm, w2_hbm, o_ref,
                   w1_ref, w3_ref, w2_ref, stage_cols, stage_rows, sem, *, layer, chunk):
    i = pl.program_id(0)
    used = i < nu_ref[0]
    e = te_ref[i]
    new_expert = (i == 0) | (e != te_ref[jnp.maximum(i - 1, 0)])

    @pl.when(used & new_expert)
    def _():
        _load_weights_bf16(w1_hbm.at[layer, e], w3_hbm.at[layer, e], w2_hbm.at[layer, e], w1_ref, w3_ref, w2_ref,
                           stage_cols, stage_rows, sem, chunk)

    @pl.when(used)
    def _():
        o_ref[...] = _swiglu(x_ref[...], w1_ref, w3_ref, w2_ref, chunk).astype(o_ref.dtype)

    @pl.when(jnp.logical_not(used))
    def _():
        o_ref[...] = jnp.zeros_like(o_ref)


def _expert_call(tile_expert, n_used, xs, w1, w3, w2, layer, tm, chunk):
    p, d = xs.shape
    ff = w1.shape[3]
    hbm = pl.BlockSpec(memory_space=pl.ANY)
    return pl.pallas_call(
        functools.partial(_expert_kernel, layer=layer, chunk=chunk),
        out_shape=jax.ShapeDtypeStruct((p, d), BF16),
        grid_spec=pltpu.PrefetchScalarGridSpec(
            num_scalar_prefetch=2,
            grid=(p // tm,),
            in_specs=[pl.BlockSpec((tm, d), lambda i, te, nu: (i, 0)), hbm, hbm, hbm],
            out_specs=pl.BlockSpec((tm, d), lambda i, te, nu: (i, 0)),
            scratch_shapes=_weight_scratch(d, ff, chunk)),
        compiler_params=_cparams(("arbitrary",), FFN_VMEM_LIMIT),
        name="moe_experts",
    )(tile_expert, n_used, xs, w1, w3, w2)


def _combine_kernel(x_ref, ya_ref, yb_ref, r_ref, g_ref, lg_ref, lb_ref, o_ref):
    r = r_ref[...]
    lane = _lane_ids(r.shape)
    ga = jnp.sum(jnp.where(lane == ROUTE_G_OFF, r, 0.0), axis=-1, keepdims=True)
    gb = jnp.sum(jnp.where(lane == ROUTE_G_OFF + 1, r, 0.0), axis=-1, keepdims=True)
    y = ga * ya_ref[...].astype(F32) + gb * yb_ref[...].astype(F32)
    v = ALPHA * x_ref[...] + (1.0 + g_ref[0]) * y
    o_ref[...] = _layer_norm(v, lg_ref[...], lb_ref[...])


def _combine_call(x2d, ya, yb, route, g2, lg, lb, tiles_per_batch, tm):
    t, d = x2d.shape
    tok = pl.BlockSpec((tm, d), lambda i: (i, 0))
    per_b = pl.BlockSpec((1, 1, d), lambda i: (i // tiles_per_batch, 0, 0))
    vec = pl.BlockSpec((1, d), lambda i: (0, 0))
    return pl.pallas_call(
        _combine_kernel,
        out_shape=jax.ShapeDtypeStruct((t, d), F32),
        grid=(t // tm,),
        in_specs=[tok, tok, tok, pl.BlockSpec((tm, LANES), lambda i: (i, 0)), per_b, vec, vec],
        out_specs=tok,
        compiler_params=_cparams(("parallel",)),
        name="moe_combine_ln",
    )(x2d, ya, yb, route, g2, lg, lb)


def _dispatch_plan(route, counts, tm, tme):
    t = route.shape[0]
    nt = t // tm
    n = counts[:, 0, :N_EXPERTS].astype(jnp.int32)
    gran = (n + ROW_GRANULE - 1) // ROW_GRANULE
    seg = gran * ROW_GRANULE
    local_off = jnp.cumsum(seg, axis=1) - seg
    region = ((jnp.sum(seg, axis=0) + tme - 1) // tme) * tme
    ends = jnp.cumsum(region)
    seg_row = (ends - region)[None, :] + jnp.cumsum(seg, axis=0) - seg
    e = route[:, 0:2].astype(jnp.int32)
    rank = route[:, ROUTE_RANK_OFF:ROUTE_RANK_OFF + 2].astype(jnp.int32)
    tile_of = (jnp.arange(t, dtype=jnp.int32) // tm)[:, None]
    dest = seg_row[tile_of, e] + rank
    n_rows = _sorted_rows(t, tm, tme)
    tile_start = jnp.arange(n_rows // tme, dtype=jnp.int32) * tme
    tile_expert = jnp.minimum(jnp.sum((tile_start[:, None] >= ends[None, :]).astype(jnp.int32), axis=1),
                              N_EXPERTS - 1)
    n_used = (ends[-1] // tme).astype(jnp.int32).reshape(1)
    tables = (seg_row.reshape(-1), local_off.reshape(-1), gran.reshape(-1))
    return dest, tables, tile_expert, n_used


def _sorted_rows(t, tm, tme):
    bound = 2 * t + (t // tm) * N_EXPERTS * (ROW_GRANULE - 1) + N_EXPERTS * (tme - 1)
    return (bound + tme - 1) // tme * tme


def _dispatch_kernel(row_ref, off_ref, gran_ref, h_ref, r_ref, xs_in, xs_out, sorted_ref, sem, *, tm):
    del xs_in
    i = pl.program_id(0)
    rt = r_ref[...].T
    e1, e2 = rt[0:1, :], rt[1:2, :]
    slot1, slot2 = rt[ROUTE_RANK_OFF:ROUTE_RANK_OFF + 1, :], rt[ROUTE_RANK_OFF + 1:ROUTE_RANK_OFF + 2, :]
    for e in range(N_EXPERTS):
        off = off_ref[i * N_EXPERTS + e].astype(F32)
        slot1 = slot1 + jnp.where(e1 == e, off, 0.0)
        slot2 = slot2 + jnp.where(e2 == e, off, 0.0)
    rows = lax.broadcasted_iota(jnp.int32, (sorted_ref.shape[0], tm), 0).astype(F32)
    perm = ((rows == slot1) | (rows == slot2)).astype(BF16)
    sorted_ref[...] = _dot(perm, h_ref[...]).astype(BF16)

    def copy(e, g):
        src = pl.multiple_of(off_ref[i * N_EXPERTS + e] + g * ROW_GRANULE, ROW_GRANULE)
        dst = pl.multiple_of(row_ref[i * N_EXPERTS + e] + g * ROW_GRANULE, ROW_GRANULE)
        return pltpu.make_async_copy(sorted_ref.at[pl.ds(src, ROW_GRANULE)], xs_out.at[pl.ds(dst, ROW_GRANULE)], sem)

    for e in range(N_EXPERTS):
        lax.fori_loop(0, gran_ref[i * N_EXPERTS + e], lambda g, c, e=e: (copy(e, g).start(), c)[1], 0)
    for e in range(N_EXPERTS):
        lax.fori_loop(0, gran_ref[i * N_EXPERTS + e], lambda g, c, e=e: (copy(e, g).wait(), c)[1], 0)


def _dispatch_call(tables, h, route, n_rows, tm):
    t, d = h.shape
    buf_rows = 2 * tm + N_EXPERTS * ROW_GRANULE
    hbm = pl.BlockSpec(memory_space=pl.ANY)
    return pl.pallas_call(
        functools.partial(_dispatch_kernel, tm=tm),
        out_shape=jax.ShapeDtypeStruct((n_rows, d), BF16),
        grid_spec=pltpu.PrefetchScalarGridSpec(
            num_scalar_prefetch=3,
            grid=(t // tm,),
            in_specs=[pl.BlockSpec((tm, d), lambda i, *_: (i, 0)), pl.BlockSpec((tm, LANES), lambda i, *_: (i, 0)),
                      hbm],
            out_specs=hbm,
            scratch_shapes=[pltpu.VMEM((buf_rows, d), BF16), pltpu.SemaphoreType.DMA(())]),
        input_output_aliases={5: 0},
        compiler_params=_cparams(("arbitrary",)),
        name="moe_dispatch",
    )(*tables, h, route, jnp.zeros((n_rows, d), BF16))


def _moe_layer(x2d, sc, sh, g2, wr_p, w1, w3, w2, layer, lg, lb, tiles_per_batch, tm, tme, tf):
    h, route, counts = _router_call(x2d, sc, sh, wr_p, tiles_per_batch, tm)
    dest, tables, tile_expert, n_used = _dispatch_plan(route, counts, tm, tme)
    xs = _dispatch_call(tables, h, route, _sorted_rows(x2d.shape[0], tm, tme), tm)
    ys = _expert_call(tile_expert, n_used, xs, w1, w3, w2, layer, tme, tf)
    ya = ys.at[dest[:, 0]].get(mode="promise_in_bounds")
    yb = ys.at[dest[:, 1]].get(mode="promise_in_bounds")
    return _combine_call(x2d, ya, yb, route, g2, lg, lb, tiles_per_batch, tm)


def _trig_kernel(a_ref, c_ref, s_ref):
    a = a_ref[...]
    c_ref[...] = jnp.cos(a)
    s_ref[...] = jnp.sin(a)


def _trig_call(ang2d):
    n = ang2d.shape[0]
    tr = math.gcd(n, 1024)
    spec = pl.BlockSpec((tr, LANES), lambda i: (i, 0))
    return pl.pallas_call(
        _trig_kernel,
        out_shape=[jax.ShapeDtypeStruct(ang2d.shape, F32)] * 2,
        grid=(n // tr,),
        in_specs=[spec],
        out_specs=[spec, spec],
        compiler_params=_cparams(("parallel",)),
        name="rope_trig",
    )(ang2d)


def _rope_tables(positions):
    pos = positions.astype(F32)[..., None]
    lane = jnp.arange(LANES)
    inv64 = ROPE_THETA ** (-jnp.arange(0, RET_D, 2, dtype=F32) / RET_D)
    invm = ROPE_THETA ** (-jnp.arange(0, MLA_ROPE, 2, dtype=F32) / MLA_ROPE)
    n64 = inv64.shape[0]
    ang = pos * jnp.concatenate([inv64, invm])
    cos, sin = _trig_call(ang.reshape(-1, LANES))
    cos, sin = cos.reshape(ang.shape), sin.reshape(ang.shape)
    cos64, sin64, cosm, sinm = cos[..., :n64], sin[..., :n64], cos[..., n64:], sin[..., n64:]
    sign64 = jnp.where((lane % 64) < 32, -1.0, 1.0)
    c64 = jnp.tile(cos64, (1, 1, LANES // 32))
    s64 = jnp.tile(sin64, (1, 1, LANES // 32)) * sign64
    in_rope = (lane >= KR_LANE) & (lane < KR_LANE + MLA_ROPE)
    signm = jnp.where(lane < KR_LANE + MLA_ROPE // 2, -1.0, 1.0)
    cm = jnp.where(in_rope, jnp.tile(cosm, (1, 1, LANES // 16)), 1.0)
    sm = jnp.where(in_rope, jnp.tile(sinm, (1, 1, LANES // 16)) * signm, 0.0)
    return c64, s64, cm, sm


def _prep_mixer_weights(w_in, w_uq, w_ukv):
    d = w_in.shape[0]
    kr_cols = jnp.zeros((d, LANES), w_in.dtype).at[:, KR_LANE:KR_LANE + MLA_ROPE].set(
        w_in[:, MLA_OFF + MLA_Q_RANK + MLA_KV_RANK:])
    w_in_p = jnp.concatenate([w_in[:, :MLA_OFF + MLA_Q_RANK + MLA_KV_RANK], kr_cols], axis=1).astype(BF16)
    uq = w_uq.reshape(MLA_Q_RANK, MLA_HEADS, MLA_NOPE + MLA_ROPE)
    wuq_p = jnp.pad(uq, ((0, 0), (0, 0), (0, LANES - MLA_NOPE - MLA_ROPE))).reshape(MLA_Q_RANK, MLA_PAD).astype(BF16)
    ukv = w_ukv.reshape(MLA_KV_RANK, MLA_HEADS, MLA_NOPE + MLA_V)
    wkn_p = jnp.pad(ukv[:, :, :MLA_NOPE], ((0, 0), (0, 0), (0, LANES - MLA_NOPE))).reshape(MLA_KV_RANK, MLA_PAD).astype(BF16)
    wv = ukv[:, :, MLA_NOPE:].reshape(MLA_KV_RANK, MLA_VW).astype(BF16)
    return w_in_p, wuq_p, wkn_p, wv


def kernel(x, c, positions, w_in, ret_gn_g, mla_qn_g, mla_kvn_g, w_uq, w_ukv, w_out, w_ada, b_ada, ln1_g, ln1_b, ln2_g, ln2_b, w1_dense, w3_dense, w2_dense, w_router, w1_moe, w3_moe, w2_moe):
    b, s, d = x.shape
    tabs = _rope_tables(positions)
    ret_tabs = _ret_tables()
    mod = _ada_call(c, w_ada, b_ada)
    tm_tok = 512
    tiles_per_batch = s // tm_tok
    w_out_b = _cast_call(w_out)
    for l in range(w_in.shape[0]):
        sh1, sc1, g1, sh2, sc2, g2 = [mod[l, :, j * d:(j + 1) * d].reshape(b, 1, d) for j in range(6)]
        w_in_p, wuq_p, wkn_p, wv = _prep_mixer_weights(w_in[l], w_uq[l], w_ukv[l])
        (rq, rk, rv, rg), dil_views, (mq, mk, mv) = _inproj_call(
            x, sc1, sh1, w_in_p, tabs, mla_qn_g[l].reshape(1, -1), mla_kvn_g[l].reshape(1, -1), wuq_p, wkn_p, wv)
        ya = _ret_call(rq, rk, rv, rg, ret_gn_g[l].reshape(1, -1), ret_tabs)
        yb = _dil_call(dil_views)
        yc = _mla_call(mq, mk, mv)
        x = _outproj_call(ya, yb, yc, w_out_b, l, x, g1, ln1_g[l].reshape(1, d), ln1_b[l].reshape(1, d))
        x2d = x.reshape(b * s, d)
        lg, lb = ln2_g[l].reshape(1, d), ln2_b[l].reshape(1, d)
        j = l // 2
        if l % 2 == 0:
            x2d = _ffn_call(x2d, sc2, sh2, g2, w1_dense, w3_dense, w2_dense, j, lg, lb, tiles_per_batch, tm_tok,
                            FF_CHUNK)
        else:
            wr_p = jnp.pad(w_router[j], ((0, 0), (0, LANES - N_EXPERTS)))
            x2d = _moe_layer(x2d, sc2, sh2, g2, wr_p, w1_moe, w3_moe, w2_moe, j, lg, lb, tiles_per_batch, tm_tok,
                             EXPERT_TILE, FF_CHUNK)
        x = x2d.reshape(b, s, d)
    return x
```

```python
import functools
import math

import jax
import jax.numpy as jnp
from jax import lax
from jax.experimental import pallas as pl
from jax.experimental.pallas import tpu as pltpu

D_MODEL = 1024
DEPTH = 4
RET_HEADS = 4
RET_D = 64
RET_CHUNK = 128
DIL_HEADS = 6
DIL_D = 64
DIL_DILATIONS = (1, 4, 16)
DIL_W = 128
MLA_HEADS = 6
MLA_Q_RANK = 384
MLA_KV_RANK = 256
MLA_NOPE = 64
MLA_ROPE = 32
MLA_V = 64
N_EXPERTS = 8
ROPE_THETA = 10000.0
LN_EPS = 1e-5
RMS_EPS = 1e-6
ALPHA = (2.0 * DEPTH) ** 0.25

LANES = 128
RET_W = RET_HEADS * RET_D
DIL_WD = DIL_HEADS * DIL_D
MLA_PAD = MLA_HEADS * LANES
MLA_VW = MLA_HEADS * MLA_V
RET_IN = 4 * RET_W
DIL_IN = 3 * DIL_WD
MLA_OFF = RET_IN + DIL_IN
D_IN_PAD = MLA_OFF + MLA_Q_RANK + MLA_KV_RANK + LANES
KR_LANE = MLA_NOPE

VMEM_LIMIT = 48 * 1024 * 1024
FFN_VMEM_LIMIT = 56 * 1024 * 1024
FF_CHUNK = 512
EXPERT_TILE = 512
CAST_BLOCK_BYTES = 4 * 1024 * 1024
ROW_GRANULE = 16
BF16 = jnp.bfloat16
F32 = jnp.float32
NEG_INF = float("-inf")
LOG2E = math.log2(math.e)
MLA_Q_SCALE = (MLA_NOPE + MLA_ROPE) ** -0.5 * LOG2E


def _cparams(sem, vmem_limit=VMEM_LIMIT):
    return pltpu.CompilerParams(dimension_semantics=sem, vmem_limit_bytes=vmem_limit)


def _dot(a, b):
    return jnp.dot(a, b, preferred_element_type=F32)


def _dot_nt(a, b):
    return lax.dot_general(a, b, (((1,), (1,)), ((), ())), preferred_element_type=F32)


def _dot_tn(a, b):
    return lax.dot_general(a, b, (((0,), (0,)), ((), ())), preferred_element_type=F32)


def _silu(x):
    return x * (1.0 / (1.0 + jnp.exp(-x)))


def _layer_norm(v, g, b):
    mu = jnp.mean(v, axis=-1, keepdims=True)
    d = v - mu
    var = jnp.mean(d * d, axis=-1, keepdims=True)
    return d * lax.rsqrt(var + LN_EPS) * g + b


def _lane_ids(shape):
    return lax.broadcasted_iota(jnp.int32, shape, len(shape) - 1)


def _rope_group(x, cos, sin_signed, first_half, half):
    fwd = pltpu.roll(x, LANES - half, 1)
    bwd = pltpu.roll(x, half, 1)
    return x * cos + jnp.where(first_half, fwd, bwd) * sin_signed


def _cast_kernel(w_ref, o_ref):
    o_ref[...] = w_ref[...].astype(BF16)


def _cast_call(w):
    cols = w.shape[-1]
    w2d = w.reshape(-1, cols)
    rows = w2d.shape[0]
    pack = 16
    tr = max(pack, CAST_BLOCK_BYTES // (4 * cols) // pack * pack)
    while rows % tr:
        tr -= pack
    spec = pl.BlockSpec((tr, cols), lambda i: (i, 0))
    out = pl.pallas_call(
        _cast_kernel,
        out_shape=jax.ShapeDtypeStruct(w2d.shape, BF16),
        grid=(rows // tr,),
        in_specs=[spec],
        out_specs=spec,
        compiler_params=_cparams(("parallel",)),
        name="cast_bf16",
    )(w2d)
    return out.reshape(w.shape)


def _ada_kernel(c_ref, w_ref, b_ref, o_ref):
    cond = _silu(c_ref[...])
    o_ref[0] = jnp.dot(cond, w_ref[0], preferred_element_type=F32,
                       precision=lax.Precision.HIGHEST) + b_ref[0]


def _ada_call(c, w_ada, b_ada):
    nl, d, n = w_ada.shape
    b = c.shape[0]
    tn = 1536
    return pl.pallas_call(
        _ada_kernel,
        out_shape=jax.ShapeDtypeStruct((nl, b, n), F32),
        grid=(nl, n // tn),
        in_specs=[pl.BlockSpec((b, d), lambda l, j: (0, 0)),
                  pl.BlockSpec((1, d, tn), lambda l, j: (l, 0, j)),
                  pl.BlockSpec((1, 1, tn), lambda l, j: (l, 0, j))],
        out_specs=pl.BlockSpec((1, b, tn), lambda l, j: (l, 0, j)),
        compiler_params=_cparams(("arbitrary", "arbitrary")),
        name="ada_mod",
    )(c, w_ada, b_ada.reshape(nl, 1, n))


def _inproj_kernel(x_ref, sc_ref, sh_ref, w_ref, c64_ref, s64_ref, cm_ref, sm_ref,
                   qg_ref, kvg_ref, wuq_ref, wkn_ref, wv_ref,
                   rq_ref, rk_ref, rv_ref, rg_ref,
                   dq1_ref, dk1_ref, dv1_ref, dq4_ref, dk4_ref, dv4_ref, dq16_ref, dk16_ref, dv16_ref,
                   mq_ref, mk_ref, mv_ref, scr_ref, *, tm):
    h = (x_ref[0] * (1.0 + sc_ref[0]) + sh_ref[0]).astype(BF16)
    c64 = c64_ref[0]
    s64 = s64_ref[0]
    cm = cm_ref[0]
    sm = sm_ref[0]
    lane = _lane_ids((1, LANES))
    first64 = (lane % 64) < 32
    firstm = lane < (KR_LANE + MLA_ROPE // 2)

    def rope64(zc):
        return _rope_group(zc, c64, s64, first64, 32)

    def ropem(zc):
        return _rope_group(zc, cm, sm, firstm, MLA_ROPE // 2)

    za = _dot(h, w_ref[:, 0:RET_IN])
    for j in range(RET_W // LANES):
        sl = slice(j * LANES, (j + 1) * LANES)
        rq_ref[0, :, sl] = rope64(za[:, j * LANES:(j + 1) * LANES]).astype(BF16)
        kc = za[:, RET_W + j * LANES:RET_W + (j + 1) * LANES]
        rk_ref[0, :, sl] = (rope64(kc) * (RET_D ** -0.5)).astype(BF16)
    rv_ref[0] = za[:, 2 * RET_W:3 * RET_W].astype(BF16)
    rg_ref[0] = za[:, 3 * RET_W:4 * RET_W]

    zb = _dot(h, w_ref[:, RET_IN:MLA_OFF])
    q_scale = (DIL_D ** -0.5) * LOG2E
    views = ((dq1_ref, dq4_ref, dq16_ref), (dk1_ref, dk4_ref, dk16_ref), (dv1_ref, dv4_ref, dv16_ref))
    for a, (n1, n4, n16) in enumerate(views):
        for j in range(DIL_WD // LANES):
            zc = zb[:, a * DIL_WD + j * LANES:a * DIL_WD + (j + 1) * LANES]
            if a == 0:
                zc = rope64(zc) * q_scale
            elif a == 1:
                zc = rope64(zc)
            scr_ref[j] = zc
            n1[0, :, j * LANES:(j + 1) * LANES] = zc.astype(BF16)
        for r, ref in ((4, n4), (16, n16)):
            for rho in range(r):
                for j in range(DIL_WD // LANES):
                    ref[0, :, rho * DIL_WD + j * LANES:rho * DIL_WD + (j + 1) * LANES] = (
                        scr_ref[j, pl.ds(rho, tm // r, stride=r), :].astype(BF16))

    zc = _dot(h, w_ref[:, MLA_OFF:D_IN_PAD])
    cq = zc[:, 0:MLA_Q_RANK]
    ckv = zc[:, MLA_Q_RANK:MLA_Q_RANK + MLA_KV_RANK]
    kr = ropem(zc[:, MLA_Q_RANK + MLA_KV_RANK:])
    cqn = (cq * lax.rsqrt(jnp.mean(cq * cq, axis=-1, keepdims=True) + RMS_EPS) * qg_ref[...]).astype(BF16)
    ckvn = (ckv * lax.rsqrt(jnp.mean(ckv * ckv, axis=-1, keepdims=True) + RMS_EPS) * kvg_ref[...]).astype(BF16)
    q = _dot(cqn, wuq_ref[...])
    kn = _dot(ckvn, wkn_ref[...])
    for hh in range(MLA_HEADS):
        sl = slice(hh * LANES, (hh + 1) * LANES)
        mq_ref[0, :, sl] = (ropem(q[:, hh * LANES:(hh + 1) * LANES]) * MLA_Q_SCALE).astype(BF16)
        mk_ref[0, :, sl] = (kn[:, hh * LANES:(hh + 1) * LANES] + kr).astype(BF16)
    mv_ref[0] = _dot(ckvn, wv_ref[...]).astype(BF16)


def _inproj_call(x, sc, sh, w_in_p, tabs, qg, kvg, wuq_p, wkn_p, wv, tm=512):
    b, s, d = x.shape
    c64, s64, cm, sm = tabs
    tok = lambda w: pl.BlockSpec((1, tm, w), lambda i, j: (i, j, 0))
    per_b = pl.BlockSpec((1, 1, d), lambda i, j: (i, 0, 0))
    full = lambda a: pl.BlockSpec(a.shape, lambda i, j: (0,) * a.ndim)
    outs = [(1, RET_W, BF16), (1, RET_W, BF16), (1, RET_W, BF16), (1, RET_W, F32)]
    outs += [(r, DIL_WD, BF16) for r in DIL_DILATIONS for _ in range(3)]
    outs += [(1, MLA_PAD, BF16), (1, MLA_PAD, BF16), (1, MLA_VW, BF16)]
    res = pl.pallas_call(
        functools.partial(_inproj_kernel, tm=tm),
        out_shape=[jax.ShapeDtypeStruct((b, s // r, r * w), dt) for r, w, dt in outs],
        grid=(b, s // tm),
        in_specs=[tok(d), per_b, per_b, full(w_in_p), tok(LANES), tok(LANES), tok(LANES), tok(LANES),
                  full(qg), full(kvg), full(wuq_p), full(wkn_p), full(wv)],
        out_specs=[pl.BlockSpec((1, tm // r, r * w), lambda i, j: (i, j, 0)) for r, w, _ in outs],
        scratch_shapes=[pltpu.VMEM((DIL_WD // LANES, tm, LANES), F32)],
        compiler_params=_cparams(("parallel", "parallel")),
        name="in_proj",
    )(x, sc, sh, w_in_p, c64, s64, cm, sm, qg, kvg, wuq_p, wkn_p, wv)
    ret = res[0:4]
    dil = [res[4 + 3 * i:7 + 3 * i] for i in range(len(DIL_DILATIONS))]
    mla = res[4 + 3 * len(DIL_DILATIONS):]
    return ret, dil, mla


def _ret_kernel(q_ref, k_ref, v_ref, g_ref, gn_ref, intra_ref, xi_ref, zeta_ref, dmat_ref, bmask_ref,
                o_ref, state_ref, *, chunks):
    @pl.when(pl.program_id(1) == 0)
    def _():
        state_ref[...] = jnp.zeros_like(state_ref)

    lane = _lane_ids((1, LANES))
    head_a = lane < RET_D
    c = RET_CHUNK
    npair = RET_W // LANES
    items = [(ci, p) for ci in range(chunks) for p in range(npair)]
    view = lambda ref, ci, p: ref[0, ci * c:(ci + 1) * c, p * LANES:(p + 1) * LANES]

    inner, kv_inc = {}, {}
    for ci, p in items:
        cols = slice(p * LANES, (p + 1) * LANES)
        q, k, v = view(q_ref, ci, p), view(k_ref, ci, p), view(v_ref, ci, p)
        zero = jnp.zeros_like(q)
        s_a = _dot_nt(jnp.where(head_a, q, zero), k) * intra_ref[2 * p]
        s_b = _dot_nt(jnp.where(head_a, zero, q), k) * intra_ref[2 * p + 1]
        s_cat = jnp.concatenate([s_a, s_b], axis=1).astype(BF16)
        v_stack = jnp.concatenate([jnp.where(head_a, v, zero), jnp.where(head_a, zero, v)], axis=0)
        inner[(ci, p)] = _dot(s_cat, v_stack)
        kz = (k.astype(F32) * zeta_ref[:, cols]).astype(BF16)
        kv_inc[(ci, p)] = _dot_tn(kz, v) * bmask_ref[...]

    outs = {}
    for p in range(npair):
        cols = slice(p * LANES, (p + 1) * LANES)
        state = state_ref[p]
        for ci in range(chunks):
            cross = _dot(view(q_ref, ci, p), state.astype(BF16)) * xi_ref[:, cols]
            outs[(ci, p)] = inner[(ci, p)] + cross
            state = state * dmat_ref[p] + kv_inc[(ci, p)]
        state_ref[p] = state

    inv_n = 1.0 / RET_D
    for ci, p in items:
        cols = slice(p * LANES, (p + 1) * LANES)
        o = outs[(ci, p)]
        mu = jnp.where(head_a,
                       jnp.sum(jnp.where(head_a, o, 0.0), axis=-1, keepdims=True),
                       jnp.sum(jnp.where(head_a, 0.0, o), axis=-1, keepdims=True)) * inv_n
        d = o - mu
        dd = d * d
        var = jnp.where(head_a,
                        jnp.sum(jnp.where(head_a, dd, 0.0), axis=-1, keepdims=True),
                        jnp.sum(jnp.where(head_a, 0.0, dd), axis=-1, keepdims=True)) * inv_n
        on = d * lax.rsqrt(var + LN_EPS) * gn_ref[:, cols]
        o_ref[0, ci * c:(ci + 1) * c, cols] = (_silu(view(g_ref, ci, p)) * on).astype(BF16)


def _ret_tables():
    c = RET_CHUNK
    log_g = jnp.log(1.0 - 2.0 ** (-5.0 - jnp.arange(RET_HEADS, dtype=F32)))
    idx = jnp.arange(c, dtype=F32)
    diff = idx[:, None] - idx[None, :]
    intra = jnp.where(diff >= 0, jnp.exp(jnp.maximum(diff, 0.0) * log_g[:, None, None]), 0.0)
    xi = jnp.exp((idx[:, None] + 1.0) * log_g[None, :])
    zeta = jnp.exp((c - 1.0 - idx[:, None]) * log_g[None, :])
    decay = jnp.exp(c * log_g)
    xi_l = jnp.repeat(xi, RET_D, axis=1)
    zeta_l = jnp.repeat(zeta, RET_D, axis=1)
    r = jnp.arange(LANES)
    same = (r[:, None] // RET_D) == (r[None, :] // RET_D)
    bmask = same.astype(F32)
    dec_l = jnp.repeat(decay, RET_D).reshape(RET_W // LANES, 1, LANES)
    dmat = bmask[None] * dec_l
    return intra, xi_l, zeta_l, dmat, bmask


def _ret_call(rq, rk, rv, rg, gn, tables, chunks=8):
    b, s, w = rq.shape
    tm = chunks * RET_CHUNK
    intra, xi_l, zeta_l, dmat, bmask = tables
    tok = pl.BlockSpec((1, tm, w), lambda i, j: (i, j, 0))
    full = lambda a: pl.BlockSpec(a.shape, lambda i, j: (0,) * a.ndim)
    return pl.pallas_call(
        functools.partial(_ret_kernel, chunks=chunks),
        out_shape=jax.ShapeDtypeStruct((b, s, w), BF16),
        grid=(b, s // tm),
        in_specs=[tok, tok, tok, tok, full(gn), full(intra), full(xi_l), full(zeta_l), full(dmat), full(bmask)],
        out_specs=tok,
        scratch_shapes=[pltpu.VMEM((w // LANES, LANES, LANES), F32)],
        compiler_params=_cparams(("parallel", "arbitrary")),
        name="retention",
    )(rq, rk, rv, rg, gn, intra, xi_l, zeta_l, dmat, bmask)


def _dil_band(first_block):
    w = DIL_W
    qi = lax.broadcasted_iota(jnp.int32, (w, 2 * w), 0)
    kj = lax.broadcasted_iota(jnp.int32, (w, 2 * w), 1)
    dist = w + qi - kj
    band = (dist >= 0) & (dist <= w)
    if first_block is False:
        return band
    return band & (jnp.logical_not(first_block) | (kj >= w))


DIL_GROUP = 2


def _dil_attend(items, valid_of):
    head_a = _lane_ids((1, LANES)) < DIL_D
    scores = []
    for idx, (q, kcat, _) in enumerate(items):
        zero = jnp.zeros_like(q)
        valid = valid_of(idx)
        scores.append([jnp.where(valid, _dot_nt(qh, kcat), NEG_INF)
                       for qh in (jnp.where(head_a, q, zero), jnp.where(head_a, zero, q))])
    soft = []
    for scs in scores:
        ms, ls, ps = [], [], []
        for sc in scs:
            m = jnp.max(sc, axis=-1, keepdims=True)
            pe = jnp.exp2(sc - m)
            ms.append(m)
            ls.append(jnp.sum(pe, axis=-1, keepdims=True))
            ps.append(pe.astype(BF16))
        soft.append((ms, ls, jnp.concatenate(ps, axis=1)))
    out = []
    for (_, _, vcat), (ms, ls, pcat) in zip(items, soft):
        zv = jnp.zeros_like(vcat)
        v_stack = jnp.concatenate([jnp.where(head_a, vcat, zv), jnp.where(head_a, zv, vcat)], axis=0)
        out.append((_dot(pcat, v_stack), ms, ls))
    return out


def _dil_normalise(acc, ms, ls):
    head_a = _lane_ids((1, LANES)) < DIL_D
    o = acc * jnp.where(head_a, 1.0 / ls[0], 1.0 / ls[1])
    lse = jnp.where(head_a, ms[0] + jnp.log2(ls[0]), ms[1] + jnp.log2(ls[1]))
    return o, lse


def _dil_partial_kernel(q_ref, kp_ref, kc_ref, vp_ref, vc_ref, o_ref, lse_ref, *, r):
    valid = _dil_band(pl.program_id(1) == 0)
    npair = DIL_WD // LANES
    for g0 in range(0, r, DIL_GROUP):
        rhos = range(g0, min(g0 + DIL_GROUP, r))
        items = []
        for rho in rhos:
            for p in range(npair):
                cols = slice(rho * DIL_WD + p * LANES, rho * DIL_WD + (p + 1) * LANES)
                items.append((q_ref[0, :, cols],
                              jnp.concatenate([kp_ref[0, :, cols], kc_ref[0, :, cols]], axis=0),
                              jnp.concatenate([vp_ref[0, :, cols], vc_ref[0, :, cols]], axis=0)))
        res = _dil_attend(items, lambda idx: valid)
        for gi, rho in enumerate(rhos):
            rows = pl.ds(rho, DIL_W, stride=r)
            for p in range(npair):
                o, lse = _dil_normalise(*res[gi * npair + p])
                o_ref[0, p, rows, :] = o
                lse_ref[0, p, rows, :] = lse


def _dil_final_kernel(q_ref, k_ref, kh_ref, v_ref, vh_ref, o4_ref, lse4_ref, o16_ref, lse16_ref, o_ref, *, nblk):
    w = DIL_W
    npair = DIL_WD // LANES
    band_first = _dil_band(pl.program_id(1) == 0)
    band = _dil_band(False)

    for g0 in range(0, nblk, DIL_GROUP):
        blocks = range(g0, min(g0 + DIL_GROUP, nblk))
        items = []
        for i in blocks:
            rows = slice(i * w, (i + 1) * w)
            for p in range(npair):
                cols = slice(p * LANES, (p + 1) * LANES)
                if i == 0:
                    kcat = jnp.concatenate([kh_ref[0, :, cols], k_ref[0, rows, cols]], axis=0)
                    vcat = jnp.concatenate([vh_ref[0, :, cols], v_ref[0, rows, cols]], axis=0)
                else:
                    kcat = k_ref[0, (i - 1) * w:(i + 1) * w, cols]
                    vcat = v_ref[0, (i - 1) * w:(i + 1) * w, cols]
                items.append((q_ref[0, rows, cols], kcat, vcat))
        res = _dil_attend(items, lambda idx: band_first if blocks[idx // npair] == 0 else band)
        for gi, i in enumerate(blocks):
            rows = slice(i * w, (i + 1) * w)
            for p in range(npair):
                o1, lse1 = _dil_normalise(*res[gi * npair + p])
                lse4 = lse4_ref[0, p, rows, :]
                lse16 = lse16_ref[0, p, rows, :]
                top = jnp.maximum(jnp.maximum(lse1, lse4), lse16)
                w1, w4, w16 = jnp.exp2(lse1 - top), jnp.exp2(lse4 - top), jnp.exp2(lse16 - top)
                num = w1 * o1 + w4 * o4_ref[0, p, rows, :] + w16 * o16_ref[0, p, rows, :]
                o_ref[0, rows, p * LANES:(p + 1) * LANES] = (num / (w1 + w4 + w16)).astype(BF16)


def _dil_call(views, nblk=4):
    (q1, k1, v1) = views[0]
    b, s, w = q1.shape
    npair = w // LANES
    partial = []
    for r, (q, k, v) in zip(DIL_DILATIONS[1:], views[1:]):
        cur = pl.BlockSpec((1, DIL_W, r * w), lambda bb, n: (bb, n, 0))
        prev = pl.BlockSpec((1, DIL_W, r * w), lambda bb, n: (bb, jnp.maximum(n - 1, 0), 0))
        partial += pl.pallas_call(
            functools.partial(_dil_partial_kernel, r=r),
            out_shape=[jax.ShapeDtypeStruct((b, npair, s, LANES), F32)] * 2,
            grid=(b, s // (r * DIL_W)),
            in_specs=[cur, prev, cur, prev, cur],
            out_specs=[pl.BlockSpec((1, npair, r * DIL_W, LANES), lambda bb, n: (bb, 0, n, 0))] * 2,
            compiler_params=_cparams(("parallel", "arbitrary")),
            name=f"dilated_r{r}",
        )(q, k, k, v, v)
    tm = nblk * DIL_W
    tok = lambda width: pl.BlockSpec((1, tm, width), lambda bb, n: (bb, n, 0))
    halo = pl.BlockSpec((1, DIL_W, w), lambda bb, n: (bb, jnp.maximum(n * nblk - 1, 0), 0))
    accs = pl.BlockSpec((1, npair, tm, LANES), lambda bb, n: (bb, 0, n, 0))
    return pl.pallas_call(
        functools.partial(_dil_final_kernel, nblk=nblk),
        out_shape=jax.ShapeDtypeStruct((b, s, w), BF16),
        grid=(b, s // tm),
        in_specs=[tok(w), tok(w), halo, tok(w), halo, accs, accs, accs, accs],
        out_specs=tok(w),
        compiler_params=_cparams(("parallel", "arbitrary")),
        name="dilated_r1_merge",
    )(q1, k1, k1, v1, v1, *partial)


def _mla_kernel(q_ref, k_ref, v_ref, o_ref, m_ref, l_ref, acc_ref, *, t, sub):
    qi = pl.program_id(2)
    ki = pl.program_id(3)
    head_a = _lane_ids((1, LANES)) < MLA_V
    nsub = t // sub

    @pl.when(ki == 0)
    def _():
        m_ref[...] = jnp.full_like(m_ref, NEG_INF)
        l_ref[...] = jnp.zeros_like(l_ref)
        acc_ref[...] = jnp.zeros_like(acc_ref)

    def v_stack(c):
        v = v_ref[0, c * sub:(c + 1) * sub]
        zv = jnp.zeros_like(v)
        return jnp.concatenate([jnp.where(head_a, v, zv), jnp.where(head_a, zv, v)], axis=0)

    def column(c, rs, diag_r):
        vs = v_stack(c)
        scores = {}
        for r in rs:
            rows = slice(r * sub, (r + 1) * sub)
            for j in range(2):
                q = q_ref[0, rows, j * LANES:(j + 1) * LANES]
                k = k_ref[0, c * sub:(c + 1) * sub, j * LANES:(j + 1) * LANES]
                s = _dot_nt(q, k)
                if r == diag_r:
                    row = lax.broadcasted_iota(jnp.int32, (sub, sub), 0)
                    colk = lax.broadcasted_iota(jnp.int32, (sub, sub), 1)
                    s = jnp.where(colk <= row, s, NEG_INF)
                scores[(r, j)] = s
        probs = {}
        for r in rs:
            rows = slice(r * sub, (r + 1) * sub)
            ps, alphas = [], []
            for j in range(2):
                s = scores[(r, j)]
                m_old = m_ref[j, rows]
                m_new = jnp.maximum(m_old, jnp.max(s, axis=-1, keepdims=True))
                alpha = jnp.exp2(m_old - m_new)
                pe = jnp.exp2(s - jnp.concatenate([m_new] * (sub // LANES), axis=1))
                l_ref[j, rows] = alpha * l_ref[j, rows] + jnp.sum(pe, axis=-1, keepdims=True)
                m_ref[j, rows] = m_new
                ps.append(pe.astype(BF16))
                alphas.append(alpha)
            probs[r] = (jnp.concatenate(ps, axis=1), jnp.where(head_a, alphas[0], alphas[1]))
        for r in rs:
            rows = slice(r * sub, (r + 1) * sub)
            pcat, alpha = probs[r]
            acc_ref[rows] = acc_ref[rows] * alpha + _dot(pcat, vs)

    @pl.when(ki < qi)
    def _():
        for c in range(nsub):
            column(c, range(nsub), None)

    @pl.when(ki == qi)
    def _():
        for c in range(nsub):
            column(c, range(c, nsub), c)
        o_ref[0] = (acc_ref[...] / jnp.where(head_a, l_ref[0], l_ref[1])).astype(BF16)


def _mla_call(mq, mk, mv, t=2048, sub=512):
    b, s, _ = mq.shape
    n = s // t
    return pl.pallas_call(
        functools.partial(_mla_kernel, t=t, sub=sub),
        out_shape=jax.ShapeDtypeStruct((b, s, MLA_VW), BF16),
        grid=(b, MLA_HEADS // 2, n, n),
        in_specs=[pl.BlockSpec((1, t, 2 * LANES), lambda bb, p, i, j: (bb, i, p)),
                  pl.BlockSpec((1, t, 2 * LANES), lambda bb, p, i, j: (bb, jnp.minimum(j, i), p)),
                  pl.BlockSpec((1, t, LANES), lambda bb, p, i, j: (bb, jnp.minimum(j, i), p))],
        out_specs=pl.BlockSpec((1, t, LANES), lambda bb, p, i, j: (bb, i, p)),
        scratch_shapes=[pltpu.VMEM((2, t, LANES), F32), pltpu.VMEM((2, t, LANES), F32),
                        pltpu.VMEM((t, LANES), F32)],
        compiler_params=_cparams(("parallel", "parallel", "parallel", "arbitrary")),
        name="mla_attention",
    )(mq, mk, mv)


def _outproj_kernel(ya_ref, yb_ref, yc_ref, w_ref, x_ref, g_ref, lg_ref, lb_ref, o_ref):
    a, b = RET_W, RET_W + DIL_WD
    y = (_dot(ya_ref[0], w_ref[0, 0:a, :]) + _dot(yb_ref[0], w_ref[0, a:b, :])
         + _dot(yc_ref[0], w_ref[0, b:, :]))
    v = ALPHA * x_ref[0] + (1.0 + g_ref[0]) * y
    o_ref[0] = _layer_norm(v, lg_ref[...], lb_ref[...])


def _outproj_call(ya, yb, yc, w_out, layer, x, g1, lg, lb, tm=512):
    b, s, d = x.shape
    tok = lambda w: pl.BlockSpec((1, tm, w), lambda i, j: (i, j, 0))
    per_b = pl.BlockSpec((1, 1, d), lambda i, j: (i, 0, 0))
    full = lambda a: pl.BlockSpec(a.shape, lambda i, j: (0,) * a.ndim)
    return pl.pallas_call(
        _outproj_kernel,
        out_shape=jax.ShapeDtypeStruct((b, s, d), F32),
        grid=(b, s // tm),
        in_specs=[tok(ya.shape[-1]), tok(yb.shape[-1]), tok(yc.shape[-1]),
                  pl.BlockSpec((1,) + w_out.shape[1:], lambda i, j: (layer, 0, 0)),
                  tok(d), per_b, full(lg), full(lb)],
        out_specs=tok(d),
        compiler_params=_cparams(("parallel", "parallel")),
        name="out_proj_ln",
    )(ya, yb, yc, w_out, x, g1, lg, lb)


def _swiglu(h, w1_ref, w3_ref, w2_ref, chunk):
    ff = w1_ref.shape[1]
    y = None
    for a in range(0, ff, chunk):
        b = min(a + chunk, ff)
        mid = (_silu(_dot(h, w1_ref[:, a:b])) * _dot(h, w3_ref[:, a:b])).astype(BF16)
        part = _dot(mid, w2_ref[a:b, :])
        y = part if y is None else y + part
    return y


def _load_weights_bf16(w1_hbm, w3_hbm, w2_hbm, w1_ref, w3_ref, w2_ref, stage_cols, stage_rows, sem, chunk):
    ff = w1_ref.shape[1]
    jobs = []
    for a in range(0, ff, chunk):
        b = min(a + chunk, ff)
        for src, dst in ((w1_hbm, w1_ref), (w3_hbm, w3_ref)):
            jobs.append((src.at[:, a:b], lambda s, n=b - a: stage_cols.at[s, :, 0:n], dst.at[:, a:b]))
        jobs.append((w2_hbm.at[a:b, :], lambda s, n=b - a: stage_rows.at[s, 0:n, :], w2_ref.at[a:b, :]))

    def copy(k):
        src, stage, _ = jobs[k]
        return pltpu.make_async_copy(src, stage(k % 2), sem.at[k % 2])

    copy(0).start()
    for k in range(len(jobs)):
        if k + 1 < len(jobs):
            copy(k + 1).start()
        copy(k).wait()
        _, stage, dst = jobs[k]
        dst[...] = stage(k % 2)[...].astype(BF16)


def _ffn_kernel(x_ref, sc_ref, sh_ref, g_ref, w1_hbm, w3_hbm, w2_hbm, lg_ref, lb_ref, o_ref,
                w1_ref, w3_ref, w2_ref, stage_cols, stage_rows, sem, *, layer, chunk):
    @pl.when(pl.program_id(0) == 0)
    def _():
        _load_weights_bf16(w1_hbm.at[layer], w3_hbm.at[layer], w2_hbm.at[layer], w1_ref, w3_ref, w2_ref,
                           stage_cols, stage_rows, sem, chunk)

    x = x_ref[...]
    h = (x * (1.0 + sc_ref[0]) + sh_ref[0]).astype(BF16)
    y = _swiglu(h, w1_ref, w3_ref, w2_ref, chunk)
    o_ref[...] = _layer_norm(ALPHA * x + (1.0 + g_ref[0]) * y, lg_ref[...], lb_ref[...])


def _weight_scratch(d, ff, chunk):
    return [pltpu.VMEM((d, ff), BF16), pltpu.VMEM((d, ff), BF16), pltpu.VMEM((ff, d), BF16),
            pltpu.VMEM((2, d, chunk), F32), pltpu.VMEM((2, chunk, d), F32), pltpu.SemaphoreType.DMA((2,))]


def _ffn_call(x2d, sc, sh, g2, w1, w3, w2, layer, lg, lb, tiles_per_batch, tm, chunk):
    t, d = x2d.shape
    ff = w1.shape[2]
    per_b = pl.BlockSpec((1, 1, d), lambda i: (i // tiles_per_batch, 0, 0))
    vec = pl.BlockSpec((1, d), lambda i: (0, 0))
    hbm = pl.BlockSpec(memory_space=pl.ANY)
    return pl.pallas_call(
        functools.partial(_ffn_kernel, layer=layer, chunk=chunk),
        out_shape=jax.ShapeDtypeStruct((t, d), F32),
        grid=(t // tm,),
        in_specs=[pl.BlockSpec((tm, d), lambda i: (i, 0)), per_b, per_b, per_b, hbm, hbm, hbm, vec, vec],
        out_specs=pl.BlockSpec((tm, d), lambda i: (i, 0)),
        scratch_shapes=_weight_scratch(d, ff, chunk),
        compiler_params=_cparams(("arbitrary",), FFN_VMEM_LIMIT),
        name="ffn_dense",
    )(x2d, sc, sh, g2, w1, w3, w2, lg, lb)


ROUTE_G_OFF = 2
ROUTE_RANK_OFF = 4


def _router_kernel(x_ref, sc_ref, sh_ref, wr_ref, tri_ref, h_ref, r_ref, cnt_ref):
    h = x_ref[...] * (1.0 + sc_ref[0]) + sh_ref[0]
    h_ref[...] = h.astype(BF16)
    logits = jnp.dot(h, wr_ref[...], preferred_element_type=F32, precision=lax.Precision.HIGHEST)
    lane = _lane_ids(logits.shape)
    lg = jnp.where(lane < N_EXPERTS, logits, NEG_INF)
    m1 = jnp.max(lg, axis=-1, keepdims=True)
    i1 = jnp.min(jnp.where(lg == m1, lane, LANES), axis=-1, keepdims=True)
    lg2 = jnp.where(lane == i1, NEG_INF, lg)
    m2 = jnp.max(lg2, axis=-1, keepdims=True)
    i2 = jnp.min(jnp.where(lg2 == m2, lane, LANES), axis=-1, keepdims=True)
    e2 = jnp.exp(m2 - m1)
    den = 1.0 + e2
    chosen = (lane == i1) | (lane == i2)
    cum = _dot(tri_ref[...], chosen.astype(BF16))
    rank1 = jnp.sum(jnp.where(lane == i1, cum, 0.0), axis=-1, keepdims=True) - 1.0
    rank2 = jnp.sum(jnp.where(lane == i2, cum, 0.0), axis=-1, keepdims=True) - 1.0
    cnt_ref[0] = cum[cum.shape[0] - 1:, :]
    out = jnp.where(lane == 0, i1.astype(F32), 0.0)
    out = jnp.where(lane == 1, i2.astype(F32), out)
    out = jnp.where(lane == ROUTE_G_OFF, 1.0 / den, out)
    out = jnp.where(lane == ROUTE_G_OFF + 1, e2 / den, out)
    out = jnp.where(lane == ROUTE_RANK_OFF, rank1, out)
    out = jnp.where(lane == ROUTE_RANK_OFF + 1, rank2, out)
    r_ref[...] = out


def _router_call(x2d, sc, sh, wr_p, tiles_per_batch, tm):
    t, d = x2d.shape
    per_b = pl.BlockSpec((1, 1, d), lambda i: (i // tiles_per_batch, 0, 0))
    tri = (jnp.arange(tm)[:, None] >= jnp.arange(tm)[None, :]).astype(BF16)
    return pl.pallas_call(
        _router_kernel,
        out_shape=[jax.ShapeDtypeStruct((t, d), BF16), jax.ShapeDtypeStruct((t, LANES), F32),
                   jax.ShapeDtypeStruct((t // tm, 1, LANES), F32)],
        grid=(t // tm,),
        in_specs=[pl.BlockSpec((tm, d), lambda i: (i, 0)), per_b, per_b,
                  pl.BlockSpec(wr_p.shape, lambda i: (0, 0)), pl.BlockSpec((tm, tm), lambda i: (0, 0))],
        out_specs=[pl.BlockSpec((tm, d), lambda i: (i, 0)), pl.BlockSpec((tm, LANES), lambda i: (i, 0)),
                   pl.BlockSpec((1, 1, LANES), lambda i: (i, 0, 0))],
        compiler_params=_cparams(("parallel",)),
        name="moe_router",
    )(x2d, sc, sh, wr_p, tri)


def _expert_kernel(te_ref, nu_ref, x_ref, w1_hbm, w3_hbm, w2_hbm, o_ref,
                   w1_ref, w3_ref, w2_ref, stage_cols, stage_rows, sem, *, layer, chunk):
    i = pl.program_id(0)
    used = i < nu_ref[0]
    e = te_ref[i]
    new_expert = (i == 0) | (e != te_ref[jnp.maximum(i - 1, 0)])

    @pl.when(used & new_expert)
    def _():
        _load_weights_bf16(w1_hbm.at[layer, e], w3_hbm.at[layer, e], w2_hbm.at[layer, e], w1_ref, w3_ref, w2_ref,
                           stage_cols, stage_rows, sem, chunk)

    @pl.when(used)
    def _():
        o_ref[...] = _swiglu(x_ref[...], w1_ref, w3_ref, w2_ref, chunk).astype(o_ref.dtype)

    @pl.when(jnp.logical_not(used))
    def _():
        o_ref[...] = jnp.zeros_like(o_ref)


def _expert_call(tile_expert, n_used, xs, w1, w3, w2, layer, tm, chunk):
    p, d = xs.shape
    ff = w1.shape[3]
    hbm = pl.BlockSpec(memory_space=pl.ANY)
    return pl.pallas_call(
        functools.partial(_expert_kernel, layer=layer, chunk=chunk),
        out_shape=jax.ShapeDtypeStruct((p, d), BF16),
        grid_spec=pltpu.PrefetchScalarGridSpec(
            num_scalar_prefetch=2,
            grid=(p // tm,),
            in_specs=[pl.BlockSpec((tm, d), lambda i, te, nu: (i, 0)), hbm, hbm, hbm],
            out_specs=pl.BlockSpec((tm, d), lambda i, te, nu: (i, 0)),
            scratch_shapes=_weight_scratch(d, ff, chunk)),
        compiler_params=_cparams(("arbitrary",), FFN_VMEM_LIMIT),
        name="moe_experts",
    )(tile_expert, n_used, xs, w1, w3, w2)


def _combine_kernel(x_ref, ya_ref, yb_ref, r_ref, g_ref, lg_ref, lb_ref, o_ref):
    r = r_ref[...]
    lane = _lane_ids(r.shape)
    ga = jnp.sum(jnp.where(lane == ROUTE_G_OFF, r, 0.0), axis=-1, keepdims=True)
    gb = jnp.sum(jnp.where(lane == ROUTE_G_OFF + 1, r, 0.0), axis=-1, keepdims=True)
    y = ga * ya_ref[...].astype(F32) + gb * yb_ref[...].astype(F32)
    v = ALPHA * x_ref[...] + (1.0 + g_ref[0]) * y
    o_ref[...] = _layer_norm(v, lg_ref[...], lb_ref[...])


def _combine_call(x2d, ya, yb, route, g2, lg, lb, tiles_per_batch, tm):
    t, d = x2d.shape
    tok = pl.BlockSpec((tm, d), lambda i: (i, 0))
    per_b = pl.BlockSpec((1, 1, d), lambda i: (i // tiles_per_batch, 0, 0))
    vec = pl.BlockSpec((1, d), lambda i: (0, 0))
    return pl.pallas_call(
        _combine_kernel,
        out_shape=jax.ShapeDtypeStruct((t, d), F32),
        grid=(t // tm,),
        in_specs=[tok, tok, tok, pl.BlockSpec((tm, LANES), lambda i: (i, 0)), per_b, vec, vec],
        out_specs=tok,
        compiler_params=_cparams(("parallel",)),
        name="moe_combine_ln",
    )(x2d, ya, yb, route, g2, lg, lb)


def _dispatch_plan(route, counts, tm, tme):
    t = route.shape[0]
    nt = t // tm
    n = counts[:, 0, :N_EXPERTS].astype(jnp.int32)
    gran = (n + ROW_GRANULE - 1) // ROW_GRANULE
    seg = gran * ROW_GRANULE
    local_off = jnp.cumsum(seg, axis=1) - seg
    region = ((jnp.sum(seg, axis=0) + tme - 1) // tme) * tme
    ends = jnp.cumsum(region)
    seg_row = (ends - region)[None, :] + jnp.cumsum(seg, axis=0) - seg
    n_rows = _sorted_rows(t, tm, tme)
    tile_start = jnp.arange(n_rows // tme, dtype=jnp.int32) * tme
    tile_expert = jnp.minimum(jnp.sum((tile_start[:, None] >= ends[None, :]).astype(jnp.int32), axis=1),
                              N_EXPERTS - 1)
    n_used = (ends[-1] // tme).astype(jnp.int32).reshape(1)
    tables = (seg_row.reshape(-1), local_off.reshape(-1), gran.reshape(-1))
    return tables, tile_expert, n_used


def _sorted_rows(t, tm, tme):
    bound = 2 * t + (t // tm) * N_EXPERTS * (ROW_GRANULE - 1) + N_EXPERTS * (tme - 1)
    return (bound + tme - 1) // tme * tme


def _dispatch_kernel(row_ref, off_ref, gran_ref, h_ref, r_ref, xs_in, xs_out, dest_ref, sorted_ref, sem, *, tm):
    del xs_in
    i = pl.program_id(0)
    rt = r_ref[...].T
    e1, e2 = rt[0:1, :], rt[1:2, :]
    rank1, rank2 = rt[ROUTE_RANK_OFF:ROUTE_RANK_OFF + 1, :], rt[ROUTE_RANK_OFF + 1:ROUTE_RANK_OFF + 2, :]
    slot1, slot2, dest1, dest2 = rank1, rank2, rank1, rank2
    for e in range(N_EXPERTS):
        off = off_ref[i * N_EXPERTS + e].astype(F32)
        row = row_ref[i * N_EXPERTS + e].astype(F32)
        slot1 = slot1 + jnp.where(e1 == e, off, 0.0)
        slot2 = slot2 + jnp.where(e2 == e, off, 0.0)
        dest1 = dest1 + jnp.where(e1 == e, row, 0.0)
        dest2 = dest2 + jnp.where(e2 == e, row, 0.0)
    sub = lax.broadcasted_iota(jnp.int32, (dest_ref.shape[1], tm), 0)
    dest_ref[0] = jnp.where(sub == 0, dest1, jnp.where(sub == 1, dest2, 0.0)).astype(jnp.int32)
    rows = lax.broadcasted_iota(jnp.int32, (sorted_ref.shape[0], tm), 0).astype(F32)
    perm = ((rows == slot1) | (rows == slot2)).astype(BF16)
    sorted_ref[...] = _dot(perm, h_ref[...]).astype(BF16)

    def copy(e, g):
        src = pl.multiple_of(off_ref[i * N_EXPERTS + e] + g * ROW_GRANULE, ROW_GRANULE)
        dst = pl.multiple_of(row_ref[i * N_EXPERTS + e] + g * ROW_GRANULE, ROW_GRANULE)
        return pltpu.make_async_copy(sorted_ref.at[pl.ds(src, ROW_GRANULE)], xs_out.at[pl.ds(dst, ROW_GRANULE)], sem)

    for e in range(N_EXPERTS):
        lax.fori_loop(0, gran_ref[i * N_EXPERTS + e], lambda g, c, e=e: (copy(e, g).start(), c)[1], 0)
    for e in range(N_EXPERTS):
        lax.fori_loop(0, gran_ref[i * N_EXPERTS + e], lambda g, c, e=e: (copy(e, g).wait(), c)[1], 0)


def _dispatch_call(tables, h, route, n_rows, tm):
    t, d = h.shape
    buf_rows = 2 * tm + N_EXPERTS * ROW_GRANULE
    hbm = pl.BlockSpec(memory_space=pl.ANY)
    sub = 8
    xs, dest = pl.pallas_call(
        functools.partial(_dispatch_kernel, tm=tm),
        out_shape=[jax.ShapeDtypeStruct((n_rows, d), BF16), jax.ShapeDtypeStruct((t // tm, sub, tm), jnp.int32)],
        grid_spec=pltpu.PrefetchScalarGridSpec(
            num_scalar_prefetch=3,
            grid=(t // tm,),
            in_specs=[pl.BlockSpec((tm, d), lambda i, *_: (i, 0)), pl.BlockSpec((tm, LANES), lambda i, *_: (i, 0)),
                      hbm],
            out_specs=[hbm, pl.BlockSpec((1, sub, tm), lambda i, *_: (i, 0, 0))],
            scratch_shapes=[pltpu.VMEM((buf_rows, d), BF16), pltpu.SemaphoreType.DMA(())]),
        input_output_aliases={5: 0},
        compiler_params=_cparams(("arbitrary",)),
        name="moe_dispatch",
    )(*tables, h, route, jnp.zeros((n_rows, d), BF16))
    return xs, dest[:, 0, :].reshape(t), dest[:, 1, :].reshape(t)


def _moe_layer(x2d, sc, sh, g2, wr_p, w1, w3, w2, layer, lg, lb, tiles_per_batch, tm, tme, tf):
    h, route, counts = _router_call(x2d, sc, sh, wr_p, tiles_per_batch, tm)
    tables, tile_expert, n_used = _dispatch_plan(route, counts, tm, tme)
    xs, dest1, dest2 = _dispatch_call(tables, h, route, _sorted_rows(x2d.shape[0], tm, tme), tm)
    ys = _expert_call(tile_expert, n_used, xs, w1, w3, w2, layer, tme, tf)
    ya = ys.at[dest1].get(mode="promise_in_bounds")
    yb = ys.at[dest2].get(mode="promise_in_bounds")
    return _combine_call(x2d, ya, yb, route, g2, lg, lb, tiles_per_batch, tm)


def _trig_kernel(a_ref, c_ref, s_ref):
    a = a_ref[...]
    c_ref[...] = jnp.cos(a)
    s_ref[...] = jnp.sin(a)


def _trig_call(ang2d):
    n = ang2d.shape[0]
    tr = math.gcd(n, 1024)
    spec = pl.BlockSpec((tr, LANES), lambda i: (i, 0))
    return pl.pallas_call(
        _trig_kernel,
        out_shape=[jax.ShapeDtypeStruct(ang2d.shape, F32)] * 2,
        grid=(n // tr,),
        in_specs=[spec],
        out_specs=[spec, spec],
        compiler_params=_cparams(("parallel",)),
        name="rope_trig",
    )(ang2d)


def _rope_tables(positions):
    pos = positions.astype(F32)[..., None]
    lane = jnp.arange(LANES)
    inv64 = ROPE_THETA ** (-jnp.arange(0, RET_D, 2, dtype=F32) / RET_D)
    invm = ROPE_THETA ** (-jnp.arange(0, MLA_ROPE, 2, dtype=F32) / MLA_ROPE)
    n64 = inv64.shape[0]
    ang = pos * jnp.concatenate([inv64, invm])
    cos, sin = _trig_call(ang.reshape(-1, LANES))
    cos, sin = cos.reshape(ang.shape), sin.reshape(ang.shape)
    cos64, sin64, cosm, sinm = cos[..., :n64], sin[..., :n64], cos[..., n64:], sin[..., n64:]
    sign64 = jnp.where((lane % 64) < 32, -1.0, 1.0)
    c64 = jnp.tile(cos64, (1, 1, LANES // 32))
    s64 = jnp.tile(sin64, (1, 1, LANES // 32)) * sign64
    in_rope = (lane >= KR_LANE) & (lane < KR_LANE + MLA_ROPE)
    signm = jnp.where(lane < KR_LANE + MLA_ROPE // 2, -1.0, 1.0)
    cm = jnp.where(in_rope, jnp.tile(cosm, (1, 1, LANES // 16)), 1.0)
    sm = jnp.where(in_rope, jnp.tile(sinm, (1, 1, LANES // 16)) * signm, 0.0)
    return c64, s64, cm, sm


def _prep_mixer_weights(w_in, w_uq, w_ukv):
    d = w_in.shape[0]
    kr_cols = jnp.zeros((d, LANES), w_in.dtype).at[:, KR_LANE:KR_LANE + MLA_ROPE].set(
        w_in[:, MLA_OFF + MLA_Q_RANK + MLA_KV_RANK:])
    w_in_p = jnp.concatenate([w_in[:, :MLA_OFF + MLA_Q_RANK + MLA_KV_RANK], kr_cols], axis=1).astype(BF16)
    uq = w_uq.reshape(MLA_Q_RANK, MLA_HEADS, MLA_NOPE + MLA_ROPE)
    wuq_p = jnp.pad(uq, ((0, 0), (0, 0), (0, LANES - MLA_NOPE - MLA_ROPE))).reshape(MLA_Q_RANK, MLA_PAD).astype(BF16)
    ukv = w_ukv.reshape(MLA_KV_RANK, MLA_HEADS, MLA_NOPE + MLA_V)
    wkn_p = jnp.pad(ukv[:, :, :MLA_NOPE], ((0, 0), (0, 0), (0, LANES - MLA_NOPE))).reshape(MLA_KV_RANK, MLA_PAD).astype(BF16)
    wv = ukv[:, :, MLA_NOPE:].reshape(MLA_KV_RANK, MLA_VW).astype(BF16)
    return w_in_p, wuq_p, wkn_p, wv


def kernel(x, c, positions, w_in, ret_gn_g, mla_qn_g, mla_kvn_g, w_uq, w_ukv, w_out, w_ada, b_ada, ln1_g, ln1_b, ln2_g, ln2_b, w1_dense, w3_dense, w2_dense, w_router, w1_moe, w3_moe, w2_moe):
    b, s, d = x.shape
    tabs = _rope_tables(positions)
    ret_tabs = _ret_tables()
    mod = _ada_call(c, w_ada, b_ada)
    tm_tok = 512
    tiles_per_batch = s // tm_tok
    w_out_b = _cast_call(w_out)
    for l in range(w_in.shape[0]):
        sh1, sc1, g1, sh2, sc2, g2 = [mod[l, :, j * d:(j + 1) * d].reshape(b, 1, d) for j in range(6)]
        w_in_p, wuq_p, wkn_p, wv = _prep_mixer_weights(w_in[l], w_uq[l], w_ukv[l])
        (rq, rk, rv, rg), dil_views, (mq, mk, mv) = _inproj_call(
            x, sc1, sh1, w_in_p, tabs, mla_qn_g[l].reshape(1, -1), mla_kvn_g[l].reshape(1, -1), wuq_p, wkn_p, wv)
        ya = _ret_call(rq, rk, rv, rg, ret_gn_g[l].reshape(1, -1), ret_tabs)
        yb = _dil_call(dil_views)
        yc = _mla_call(mq, mk, mv)
        x = _outproj_call(ya, yb, yc, w_out_b, l, x, g1, ln1_g[l].reshape(1, d), ln1_b[l].reshape(1, d))
        x2d = x.reshape(b * s, d)
        lg, lb = ln2_g[l].reshape(1, d), ln2_b[l].reshape(1, d)
        j = l // 2
        if l % 2 == 0:
            x2d = _ffn_call(x2d, sc2, sh2, g2, w1_dense, w3_dense, w2_dense, j, lg, lb, tiles_per_batch, tm_tok,
                            FF_CHUNK)
        else:
            wr_p = jnp.pad(w_router[j], ((0, 0), (0, LANES - N_EXPERTS)))
            x2d = _moe_layer(x2d, sc2, sh2, g2, wr_p, w1_moe, w3_moe, w2_moe, j, lg, lb, tiles_per_batch, tm_tok,
                             EXPERT_TILE, FF_CHUNK)
        x = x2d.reshape(b, s, d)
    return x
```

```python
import functools
import math

import jax
import jax.numpy as jnp
from jax import lax
from jax.experimental import pallas as pl
from jax.experimental.pallas import tpu as pltpu

D_MODEL = 1024
DEPTH = 4
RET_HEADS = 4
RET_D = 64
RET_CHUNK = 128
DIL_HEADS = 6
DIL_D = 64
DIL_DILATIONS = (1, 4, 16)
DIL_W = 128
MLA_HEADS = 6
MLA_Q_RANK = 384
MLA_KV_RANK = 256
MLA_NOPE = 64
MLA_ROPE = 32
MLA_V = 64
N_EXPERTS = 8
ROPE_THETA = 10000.0
LN_EPS = 1e-5
RMS_EPS = 1e-6
ALPHA = (2.0 * DEPTH) ** 0.25

LANES = 128
RET_W = RET_HEADS * RET_D
DIL_WD = DIL_HEADS * DIL_D
MLA_PAD = MLA_HEADS * LANES
MLA_VW = MLA_HEADS * MLA_V
RET_IN = 4 * RET_W
DIL_IN = 3 * DIL_WD
MLA_OFF = RET_IN + DIL_IN
D_IN_PAD = MLA_OFF + MLA_Q_RANK + MLA_KV_RANK + LANES
KR_LANE = MLA_NOPE

VMEM_LIMIT = 48 * 1024 * 1024
FFN_VMEM_LIMIT = 56 * 1024 * 1024
FF_CHUNK = 512
EXPERT_TILE = 512
CAST_BLOCK_BYTES = 4 * 1024 * 1024
ROW_GRANULE = 16
BF16 = jnp.bfloat16
F32 = jnp.float32
NEG_INF = float("-inf")
LOG2E = math.log2(math.e)
MLA_Q_SCALE = (MLA_NOPE + MLA_ROPE) ** -0.5 * LOG2E


def _cparams(sem, vmem_limit=VMEM_LIMIT):
    return pltpu.CompilerParams(dimension_semantics=sem, vmem_limit_bytes=vmem_limit)


def _dot(a, b):
    return jnp.dot(a, b, preferred_element_type=F32)


def _dot_nt(a, b):
    return lax.dot_general(a, b, (((1,), (1,)), ((), ())), preferred_element_type=F32)


def _dot_tn(a, b):
    return lax.dot_general(a, b, (((0,), (0,)), ((), ())), preferred_element_type=F32)


def _silu(x):
    return x * (1.0 / (1.0 + jnp.exp(-x)))


def _layer_norm(v, g, b):
    mu = jnp.mean(v, axis=-1, keepdims=True)
    d = v - mu
    var = jnp.mean(d * d, axis=-1, keepdims=True)
    return d * lax.rsqrt(var + LN_EPS) * g + b


def _lane_ids(shape):
    return lax.broadcasted_iota(jnp.int32, shape, len(shape) - 1)


def _rope_group(x, cos, sin_signed, first_half, half):
    fwd = pltpu.roll(x, LANES - half, 1)
    bwd = pltpu.roll(x, half, 1)
    return x * cos + jnp.where(first_half, fwd, bwd) * sin_signed


def _cast_kernel(w_ref, o_ref):
    o_ref[...] = w_ref[...].astype(BF16)


def _cast_call(w):
    cols = w.shape[-1]
    w2d = w.reshape(-1, cols)
    rows = w2d.shape[0]
    pack = 16
    tr = max(pack, CAST_BLOCK_BYTES // (4 * cols) // pack * pack)
    while rows % tr:
        tr -= pack
    spec = pl.BlockSpec((tr, cols), lambda i: (i, 0))
    out = pl.pallas_call(
        _cast_kernel,
        out_shape=jax.ShapeDtypeStruct(w2d.shape, BF16),
        grid=(rows // tr,),
        in_specs=[spec],
        out_specs=spec,
        compiler_params=_cparams(("parallel",)),
        name="cast_bf16",
    )(w2d)
    return out.reshape(w.shape)


def _ada_kernel(c_ref, w_ref, b_ref, o_ref):
    cond = _silu(c_ref[...])
    o_ref[0] = jnp.dot(cond, w_ref[0], preferred_element_type=F32,
                       precision=lax.Precision.HIGHEST) + b_ref[0]


def _ada_call(c, w_ada, b_ada):
    nl, d, n = w_ada.shape
    b = c.shape[0]
    tn = 1536
    return pl.pallas_call(
        _ada_kernel,
        out_shape=jax.ShapeDtypeStruct((nl, b, n), F32),
        grid=(nl, n // tn),
        in_specs=[pl.BlockSpec((b, d), lambda l, j: (0, 0)),
                  pl.BlockSpec((1, d, tn), lambda l, j: (l, 0, j)),
                  pl.BlockSpec((1, 1, tn), lambda l, j: (l, 0, j))],
        out_specs=pl.BlockSpec((1, b, tn), lambda l, j: (l, 0, j)),
        compiler_params=_cparams(("arbitrary", "arbitrary")),
        name="ada_mod",
    )(c, w_ada, b_ada.reshape(nl, 1, n))


def _inproj_kernel(x_ref, sc_ref, sh_ref, w_ref, c64_ref, s64_ref, cm_ref, sm_ref,
                   qg_ref, kvg_ref, wuq_ref, wkn_ref, wv_ref,
                   rq_ref, rk_ref, rv_ref, rg_ref,
                   dq1_ref, dk1_ref, dv1_ref, dq4_ref, dk4_ref, dv4_ref, dq16_ref, dk16_ref, dv16_ref,
                   mq_ref, mk_ref, mv_ref, scr_ref, *, tm):
    h = (x_ref[0] * (1.0 + sc_ref[0]) + sh_ref[0]).astype(BF16)
    c64 = c64_ref[0]
    s64 = s64_ref[0]
    cm = cm_ref[0]
    sm = sm_ref[0]
    lane = _lane_ids((1, LANES))
    first64 = (lane % 64) < 32
    firstm = lane < (KR_LANE + MLA_ROPE // 2)

    def rope64(zc):
        return _rope_group(zc, c64, s64, first64, 32)

    def ropem(zc):
        return _rope_group(zc, cm, sm, firstm, MLA_ROPE // 2)

    za = _dot(h, w_ref[:, 0:RET_IN])
    for j in range(RET_W // LANES):
        sl = slice(j * LANES, (j + 1) * LANES)
        rq_ref[0, :, sl] = rope64(za[:, j * LANES:(j + 1) * LANES]).astype(BF16)
        kc = za[:, RET_W + j * LANES:RET_W + (j + 1) * LANES]
        rk_ref[0, :, sl] = (rope64(kc) * (RET_D ** -0.5)).astype(BF16)
    rv_ref[0] = za[:, 2 * RET_W:3 * RET_W].astype(BF16)
    rg_ref[0] = za[:, 3 * RET_W:4 * RET_W]

    zb = _dot(h, w_ref[:, RET_IN:MLA_OFF])
    q_scale = (DIL_D ** -0.5) * LOG2E
    views = ((dq1_ref, dq4_ref, dq16_ref), (dk1_ref, dk4_ref, dk16_ref), (dv1_ref, dv4_ref, dv16_ref))
    for a, (n1, n4, n16) in enumerate(views):
        for j in range(DIL_WD // LANES):
            zc = zb[:, a * DIL_WD + j * LANES:a * DIL_WD + (j + 1) * LANES]
            if a == 0:
                zc = rope64(zc) * q_scale
            elif a == 1:
                zc = rope64(zc)
            scr_ref[j] = zc
            n1[0, :, j * LANES:(j + 1) * LANES] = zc.astype(BF16)
        for r, ref in ((4, n4), (16, n16)):
            for rho in range(r):
                for j in range(DIL_WD // LANES):
                    ref[0, :, rho * DIL_WD + j * LANES:rho * DIL_WD + (j + 1) * LANES] = (
                        scr_ref[j, pl.ds(rho, tm // r, stride=r), :].astype(BF16))

    zc = _dot(h, w_ref[:, MLA_OFF:D_IN_PAD])
    cq = zc[:, 0:MLA_Q_RANK]
    ckv = zc[:, MLA_Q_RANK:MLA_Q_RANK + MLA_KV_RANK]
    kr = ropem(zc[:, MLA_Q_RANK + MLA_KV_RANK:])
    cqn = (cq * lax.rsqrt(jnp.mean(cq * cq, axis=-1, keepdims=True) + RMS_EPS) * qg_ref[...]).astype(BF16)
    ckvn = (ckv * lax.rsqrt(jnp.mean(ckv * ckv, axis=-1, keepdims=True) + RMS_EPS) * kvg_ref[...]).astype(BF16)
    q = _dot(cqn, wuq_ref[...])
    kn = _dot(ckvn, wkn_ref[...])
    for hh in range(MLA_HEADS):
        sl = slice(hh * LANES, (hh + 1) * LANES)
        mq_ref[0, :, sl] = (ropem(q[:, hh * LANES:(hh + 1) * LANES]) * MLA_Q_SCALE).astype(BF16)
        mk_ref[0, :, sl] = (kn[:, hh * LANES:(hh + 1) * LANES] + kr).astype(BF16)
    mv_ref[0] = _dot(ckvn, wv_ref[...]).astype(BF16)


def _inproj_call(x, sc, sh, w_in_p, tabs, qg, kvg, wuq_p, wkn_p, wv, tm=512):
    b, s, d = x.shape
    c64, s64, cm, sm = tabs
    tok = lambda w: pl.BlockSpec((1, tm, w), lambda i, j: (i, j, 0))
    per_b = pl.BlockSpec((1, 1, d), lambda i, j: (i, 0, 0))
    full = lambda a: pl.BlockSpec(a.shape, lambda i, j: (0,) * a.ndim)
    outs = [(1, RET_W, BF16), (1, RET_W, BF16), (1, RET_W, BF16), (1, RET_W, F32)]
    outs += [(r, DIL_WD, BF16) for r in DIL_DILATIONS for _ in range(3)]
    outs += [(1, MLA_PAD, BF16), (1, MLA_PAD, BF16), (1, MLA_VW, BF16)]
    res = pl.pallas_call(
        functools.partial(_inproj_kernel, tm=tm),
        out_shape=[jax.ShapeDtypeStruct((b, s // r, r * w), dt) for r, w, dt in outs],
        grid=(b, s // tm),
        in_specs=[tok(d), per_b, per_b, full(w_in_p), tok(LANES), tok(LANES), tok(LANES), tok(LANES),
                  full(qg), full(kvg), full(wuq_p), full(wkn_p), full(wv)],
        out_specs=[pl.BlockSpec((1, tm // r, r * w), lambda i, j: (i, j, 0)) for r, w, _ in outs],
        scratch_shapes=[pltpu.VMEM((DIL_WD // LANES, tm, LANES), F32)],
        compiler_params=_cparams(("parallel", "parallel")),
        name="in_proj",
    )(x, sc, sh, w_in_p, c64, s64, cm, sm, qg, kvg, wuq_p, wkn_p, wv)
    ret = res[0:4]
    dil = [res[4 + 3 * i:7 + 3 * i] for i in range(len(DIL_DILATIONS))]
    mla = res[4 + 3 * len(DIL_DILATIONS):]
    return ret, dil, mla


def _ret_kernel(q_ref, k_ref, v_ref, g_ref, gn_ref, intra_ref, xi_ref, zeta_ref, dmat_ref, bmask_ref,
                o_ref, state_ref, *, chunks):
    @pl.when(pl.program_id(1) == 0)
    def _():
        state_ref[...] = jnp.zeros_like(state_ref)

    lane = _lane_ids((1, LANES))
    head_a = lane < RET_D
    c = RET_CHUNK
    npair = RET_W // LANES
    items = [(ci, p) for ci in range(chunks) for p in range(npair)]
    view = lambda ref, ci, p: ref[0, ci * c:(ci + 1) * c, p * LANES:(p + 1) * LANES]

    inner, kv_inc = {}, {}
    for ci, p in items:
        cols = slice(p * LANES, (p + 1) * LANES)
        q, k, v = view(q_ref, ci, p), view(k_ref, ci, p), view(v_ref, ci, p)
        zero = jnp.zeros_like(q)
        s_a = _dot_nt(jnp.where(head_a, q, zero), k) * intra_ref[2 * p]
        s_b = _dot_nt(jnp.where(head_a, zero, q), k) * intra_ref[2 * p + 1]
        s_cat = jnp.concatenate([s_a, s_b], axis=1).astype(BF16)
        v_stack = jnp.concatenate([jnp.where(head_a, v, zero), jnp.where(head_a, zero, v)], axis=0)
        inner[(ci, p)] = _dot(s_cat, v_stack)
        kz = (k.astype(F32) * zeta_ref[:, cols]).astype(BF16)
        kv_inc[(ci, p)] = _dot_tn(kz, v) * bmask_ref[...]

    outs = {}
    for p in range(npair):
        cols = slice(p * LANES, (p + 1) * LANES)
        state = state_ref[p]
        for ci in range(chunks):
            cross = _dot(view(q_ref, ci, p), state.astype(BF16)) * xi_ref[:, cols]
            outs[(ci, p)] = inner[(ci, p)] + cross
            state = state * dmat_ref[p] + kv_inc[(ci, p)]
        state_ref[p] = state

    inv_n = 1.0 / RET_D
    for ci, p in items:
        cols = slice(p * LANES, (p + 1) * LANES)
        o = outs[(ci, p)]
        mu = jnp.where(head_a,
                       jnp.sum(jnp.where(head_a, o, 0.0), axis=-1, keepdims=True),
                       jnp.sum(jnp.where(head_a, 0.0, o), axis=-1, keepdims=True)) * inv_n
        d = o - mu
        dd = d * d
        var = jnp.where(head_a,
                        jnp.sum(jnp.where(head_a, dd, 0.0), axis=-1, keepdims=True),
                        jnp.sum(jnp.where(head_a, 0.0, dd), axis=-1, keepdims=True)) * inv_n
        on = d * lax.rsqrt(var + LN_EPS) * gn_ref[:, cols]
        o_ref[0, ci * c:(ci + 1) * c, cols] = (_silu(view(g_ref, ci, p)) * on).astype(BF16)


def _ret_tables():
    c = RET_CHUNK
    log_g = jnp.log(1.0 - 2.0 ** (-5.0 - jnp.arange(RET_HEADS, dtype=F32)))
    idx = jnp.arange(c, dtype=F32)
    diff = idx[:, None] - idx[None, :]
    intra = jnp.where(diff >= 0, jnp.exp(jnp.maximum(diff, 0.0) * log_g[:, None, None]), 0.0)
    xi = jnp.exp((idx[:, None] + 1.0) * log_g[None, :])
    zeta = jnp.exp((c - 1.0 - idx[:, None]) * log_g[None, :])
    decay = jnp.exp(c * log_g)
    xi_l = jnp.repeat(xi, RET_D, axis=1)
    zeta_l = jnp.repeat(zeta, RET_D, axis=1)
    r = jnp.arange(LANES)
    same = (r[:, None] // RET_D) == (r[None, :] // RET_D)
    bmask = same.astype(F32)
    dec_l = jnp.repeat(decay, RET_D).reshape(RET_W // LANES, 1, LANES)
    dmat = bmask[None] * dec_l
    return intra, xi_l, zeta_l, dmat, bmask


def _ret_call(rq, rk, rv, rg, gn, tables, chunks=8):
    b, s, w = rq.shape
    tm = chunks * RET_CHUNK
    intra, xi_l, zeta_l, dmat, bmask = tables
    tok = pl.BlockSpec((1, tm, w), lambda i, j: (i, j, 0))
    full = lambda a: pl.BlockSpec(a.shape, lambda i, j: (0,) * a.ndim)
    return pl.pallas_call(
        functools.partial(_ret_kernel, chunks=chunks),
        out_shape=jax.ShapeDtypeStruct((b, s, w), BF16),
        grid=(b, s // tm),
        in_specs=[tok, tok, tok, tok, full(gn), full(intra), full(xi_l), full(zeta_l), full(dmat), full(bmask)],
        out_specs=tok,
        scratch_shapes=[pltpu.VMEM((w // LANES, LANES, LANES), F32)],
        compiler_params=_cparams(("parallel", "arbitrary")),
        name="retention",
    )(rq, rk, rv, rg, gn, intra, xi_l, zeta_l, dmat, bmask)


def _dil_band(first_block):
    w = DIL_W
    qi = lax.broadcasted_iota(jnp.int32, (w, 2 * w), 0)
    kj = lax.broadcasted_iota(jnp.int32, (w, 2 * w), 1)
    dist = w + qi - kj
    band = (dist >= 0) & (dist <= w)
    if first_block is False:
        return band
    return band & (jnp.logical_not(first_block) | (kj >= w))


DIL_GROUP = 2


def _dil_attend(items, valid_of):
    head_a = _lane_ids((1, LANES)) < DIL_D
    scores = []
    for idx, (q, kcat, _) in enumerate(items):
        zero = jnp.zeros_like(q)
        valid = valid_of(idx)
        scores.append([jnp.where(valid, _dot_nt(qh, kcat), NEG_INF)
                       for qh in (jnp.where(head_a, q, zero), jnp.where(head_a, zero, q))])
    soft = []
    for scs in scores:
        ms, ls, ps = [], [], []
        for sc in scs:
            m = jnp.max(sc, axis=-1, keepdims=True)
            pe = jnp.exp2(sc - m)
            ms.append(m)
            ls.append(jnp.sum(pe, axis=-1, keepdims=True))
            ps.append(pe.astype(BF16))
        soft.append((ms, ls, jnp.concatenate(ps, axis=1)))
    out = []
    for (_, _, vcat), (ms, ls, pcat) in zip(items, soft):
        zv = jnp.zeros_like(vcat)
        v_stack = jnp.concatenate([jnp.where(head_a, vcat, zv), jnp.where(head_a, zv, vcat)], axis=0)
        out.append((_dot(pcat, v_stack), ms, ls))
    return out


def _dil_normalise(acc, ms, ls):
    head_a = _lane_ids((1, LANES)) < DIL_D
    o = acc * jnp.where(head_a, 1.0 / ls[0], 1.0 / ls[1])
    lse = jnp.where(head_a, ms[0] + jnp.log2(ls[0]), ms[1] + jnp.log2(ls[1]))
    return o, lse


def _dil_partial_kernel(q_ref, kp_ref, kc_ref, vp_ref, vc_ref, o_ref, lse_ref, *, r):
    valid = _dil_band(pl.program_id(1) == 0)
    npair = DIL_WD // LANES
    for g0 in range(0, r, DIL_GROUP):
        rhos = range(g0, min(g0 + DIL_GROUP, r))
        items = []
        for rho in rhos:
            for p in range(npair):
                cols = slice(rho * DIL_WD + p * LANES, rho * DIL_WD + (p + 1) * LANES)
                items.append((q_ref[0, :, cols],
                              jnp.concatenate([kp_ref[0, :, cols], kc_ref[0, :, cols]], axis=0),
                              jnp.concatenate([vp_ref[0, :, cols], vc_ref[0, :, cols]], axis=0)))
        res = _dil_attend(items, lambda idx: valid)
        for gi, rho in enumerate(rhos):
            rows = pl.ds(rho, DIL_W, stride=r)
            for p in range(npair):
                o, lse = _dil_normalise(*res[gi * npair + p])
                o_ref[0, p, rows, :] = o
                lse_ref[0, p, rows, :] = lse


def _dil_final_kernel(q_ref, k_ref, kh_ref, v_ref, vh_ref, o4_ref, lse4_ref, o16_ref, lse16_ref, o_ref, *, nblk):
    w = DIL_W
    npair = DIL_WD // LANES
    band_first = _dil_band(pl.program_id(1) == 0)
    band = _dil_band(False)

    for g0 in range(0, nblk, DIL_GROUP):
        blocks = range(g0, min(g0 + DIL_GROUP, nblk))
        items = []
        for i in blocks:
            rows = slice(i * w, (i + 1) * w)
            for p in range(npair):
                cols = slice(p * LANES, (p + 1) * LANES)
                if i == 0:
                    kcat = jnp.concatenate([kh_ref[0, :, cols], k_ref[0, rows, cols]], axis=0)
                    vcat = jnp.concatenate([vh_ref[0, :, cols], v_ref[0, rows, cols]], axis=0)
                else:
                    kcat = k_ref[0, (i - 1) * w:(i + 1) * w, cols]
                    vcat = v_ref[0, (i - 1) * w:(i + 1) * w, cols]
                items.append((q_ref[0, rows, cols], kcat, vcat))
        res = _dil_attend(items, lambda idx: band_first if blocks[idx // npair] == 0 else band)
        for gi, i in enumerate(blocks):
            rows = slice(i * w, (i + 1) * w)
            for p in range(npair):
                o1, lse1 = _dil_normalise(*res[gi * npair + p])
                lse4 = lse4_ref[0, p, rows, :]
                lse16 = lse16_ref[0, p, rows, :]
                top = jnp.maximum(jnp.maximum(lse1, lse4), lse16)
                w1, w4, w16 = jnp.exp2(lse1 - top), jnp.exp2(lse4 - top), jnp.exp2(lse16 - top)
                num = w1 * o1 + w4 * o4_ref[0, p, rows, :] + w16 * o16_ref[0, p, rows, :]
                o_ref[0, rows, p * LANES:(p + 1) * LANES] = (num / (w1 + w4 + w16)).astype(BF16)


def _dil_call(views, nblk=4):
    (q1, k1, v1) = views[0]
    b, s, w = q1.shape
    npair = w // LANES
    partial = []
    for r, (q, k, v) in zip(DIL_DILATIONS[1:], views[1:]):
        cur = pl.BlockSpec((1, DIL_W, r * w), lambda bb, n: (bb, n, 0))
        prev = pl.BlockSpec((1, DIL_W, r * w), lambda bb, n: (bb, jnp.maximum(n - 1, 0), 0))
        partial += pl.pallas_call(
            functools.partial(_dil_partial_kernel, r=r),
            out_shape=[jax.ShapeDtypeStruct((b, npair, s, LANES), F32)] * 2,
            grid=(b, s // (r * DIL_W)),
            in_specs=[cur, prev, cur, prev, cur],
            out_specs=[pl.BlockSpec((1, npair, r * DIL_W, LANES), lambda bb, n: (bb, 0, n, 0))] * 2,
            compiler_params=_cparams(("parallel", "arbitrary")),
            name=f"dilated_r{r}",
        )(q, k, k, v, v)
    tm = nblk * DIL_W
    tok = lambda width: pl.BlockSpec((1, tm, width), lambda bb, n: (bb, n, 0))
    halo = pl.BlockSpec((1, DIL_W, w), lambda bb, n: (bb, jnp.maximum(n * nblk - 1, 0), 0))
    accs = pl.BlockSpec((1, npair, tm, LANES), lambda bb, n: (bb, 0, n, 0))
    return pl.pallas_call(
        functools.partial(_dil_final_kernel, nblk=nblk),
        out_shape=jax.ShapeDtypeStruct((b, s, w), BF16),
        grid=(b, s // tm),
        in_specs=[tok(w), tok(w), halo, tok(w), halo, accs, accs, accs, accs],
        out_specs=tok(w),
        compiler_params=_cparams(("parallel", "arbitrary")),
        name="dilated_r1_merge",
    )(q1, k1, k1, v1, v1, *partial)


def _mla_kernel(q_ref, k_ref, v_ref, o_ref, m_ref, l_ref, acc_ref, *, t, sub):
    qi = pl.program_id(2)
    ki = pl.program_id(3)
    head_a = _lane_ids((1, LANES)) < MLA_V
    nsub = t // sub

    @pl.when(ki == 0)
    def _():
        m_ref[...] = jnp.full_like(m_ref, NEG_INF)
        l_ref[...] = jnp.zeros_like(l_ref)
        acc_ref[...] = jnp.zeros_like(acc_ref)

    def v_stack(c):
        v = v_ref[0, c * sub:(c + 1) * sub]
        zv = jnp.zeros_like(v)
        return jnp.concatenate([jnp.where(head_a, v, zv), jnp.where(head_a, zv, v)], axis=0)

    def column(c, rs, diag_r):
        vs = v_stack(c)
        scores = {}
        for r in rs:
            rows = slice(r * sub, (r + 1) * sub)
            for j in range(2):
                q = q_ref[0, rows, j * LANES:(j + 1) * LANES]
                k = k_ref[0, c * sub:(c + 1) * sub, j * LANES:(j + 1) * LANES]
                s = _dot_nt(q, k)
                if r == diag_r:
                    row = lax.broadcasted_iota(jnp.int32, (sub, sub), 0)
                    colk = lax.broadcasted_iota(jnp.int32, (sub, sub), 1)
                    s = jnp.where(colk <= row, s, NEG_INF)
                scores[(r, j)] = s
        probs = {}
        for r in rs:
            rows = slice(r * sub, (r + 1) * sub)
            ps, alphas = [], []
            for j in range(2):
                s = scores[(r, j)]
                m_old = m_ref[j, rows]
                m_new = jnp.maximum(m_old, jnp.max(s, axis=-1, keepdims=True))
                alpha = jnp.exp2(m_old - m_new)
                pe = jnp.exp2(s - jnp.concatenate([m_new] * (sub // LANES), axis=1))
                l_ref[j, rows] = alpha * l_ref[j, rows] + jnp.sum(pe, axis=-1, keepdims=True)
                m_ref[j, rows] = m_new
                ps.append(pe.astype(BF16))
                alphas.append(alpha)
            probs[r] = (jnp.concatenate(ps, axis=1), jnp.where(head_a, alphas[0], alphas[1]))
        for r in rs:
            rows = slice(r * sub, (r + 1) * sub)
            pcat, alpha = probs[r]
            acc_ref[rows] = acc_ref[rows] * alpha + _dot(pcat, vs)

    @pl.when(ki < qi)
    def _():
        for c in range(nsub):
            column(c, range(nsub), None)

    @pl.when(ki == qi)
    def _():
        for c in range(nsub):
            column(c, range(c, nsub), c)
        o_ref[0] = (acc_ref[...] / jnp.where(head_a, l_ref[0], l_ref[1])).astype(BF16)


def _mla_call(mq, mk, mv, t=2048, sub=512):
    b, s, _ = mq.shape
    n = s // t
    return pl.pallas_call(
        functools.partial(_mla_kernel, t=t, sub=sub),
        out_shape=jax.ShapeDtypeStruct((b, s, MLA_VW), BF16),
        grid=(b, MLA_HEADS // 2, n, n),
        in_specs=[pl.BlockSpec((1, t, 2 * LANES), lambda bb, p, i, j: (bb, i, p)),
                  pl.BlockSpec((1, t, 2 * LANES), lambda bb, p, i, j: (bb, jnp.minimum(j, i), p)),
                  pl.BlockSpec((1, t, LANES), lambda bb, p, i, j: (bb, jnp.minimum(j, i), p))],
        out_specs=pl.BlockSpec((1, t, LANES), lambda bb, p, i, j: (bb, i, p)),
        scratch_shapes=[pltpu.VMEM((2, t, LANES), F32), pltpu.VMEM((2, t, LANES), F32),
                        pltpu.VMEM((t, LANES), F32)],
        compiler_params=_cparams(("parallel", "parallel", "parallel", "arbitrary")),
        name="mla_attention",
    )(mq, mk, mv)


N_MIX = 8
ROUTER_ROWS = 128


def _mixer_residual(ya_ref, yb_ref, yc_ref, w_ref, x_ref, g_ref, lg_ref, lb_ref, row_groups=None):
    a, b = RET_W, RET_W + DIL_WD
    groups = [slice(None)] if row_groups is None else row_groups
    ys = [_dot(ya_ref[r, :], w_ref[0, 0:a, :]) + _dot(yb_ref[r, :], w_ref[0, a:b, :])
          + _dot(yc_ref[r, :], w_ref[0, b:, :]) for r in groups]
    out = [_layer_norm(ALPHA * x_ref[r, :] + (1.0 + g_ref[0]) * y, lg_ref[...], lb_ref[...])
           for r, y in zip(groups, ys)]
    return out[0] if row_groups is None else out


def _mixer_operands(ya, yb, yc, w_out, layer, x2d, g1, lg, lb, tiles_per_batch, tm):
    t, d = x2d.shape
    tok = lambda a: pl.BlockSpec((tm, a.shape[-1]), lambda i, *_: (i, 0))
    flat = lambda a: a.reshape(t, a.shape[-1])
    arrays = [flat(ya), flat(yb), flat(yc), w_out, x2d, g1, lg, lb]
    specs = [tok(ya), tok(yb), tok(yc),
             pl.BlockSpec((1,) + w_out.shape[1:], lambda i, *_: (layer, 0, 0)),
             tok(x2d), pl.BlockSpec((1, 1, d), lambda i, *_: (i // tiles_per_batch, 0, 0)),
             pl.BlockSpec((1, d), lambda i, *_: (0, 0)), pl.BlockSpec((1, d), lambda i, *_: (0, 0))]
    return arrays, specs


def _swiglu(h, w1_ref, w3_ref, w2_ref, chunk):
    ff = w1_ref.shape[1]
    y = None
    for a in range(0, ff, chunk):
        b = min(a + chunk, ff)
        mid = (_silu(_dot(h, w1_ref[:, a:b])) * _dot(h, w3_ref[:, a:b])).astype(BF16)
        part = _dot(mid, w2_ref[a:b, :])
        y = part if y is None else y + part
    return y


def _load_weights_bf16(w1_hbm, w3_hbm, w2_hbm, w1_ref, w3_ref, w2_ref, stage_cols, stage_rows, sem, chunk):
    ff = w1_ref.shape[1]
    jobs = []
    for a in range(0, ff, chunk):
        b = min(a + chunk, ff)
        for src, dst in ((w1_hbm, w1_ref), (w3_hbm, w3_ref)):
            jobs.append((src.at[:, a:b], lambda s, n=b - a: stage_cols.at[s, :, 0:n], dst.at[:, a:b]))
        jobs.append((w2_hbm.at[a:b, :], lambda s, n=b - a: stage_rows.at[s, 0:n, :], w2_ref.at[a:b, :]))

    def copy(k):
        src, stage, _ = jobs[k]
        return pltpu.make_async_copy(src, stage(k % 2), sem.at[k % 2])

    copy(0).start()
    for k in range(len(jobs)):
        if k + 1 < len(jobs):
            copy(k + 1).start()
        copy(k).wait()
        _, stage, dst = jobs[k]
        dst[...] = stage(k % 2)[...].astype(BF16)


def _ffn_kernel(*refs, layer, chunk):
    mix = refs[:N_MIX]
    (sc_ref, sh_ref, g_ref, w1_hbm, w3_hbm, w2_hbm, lg_ref, lb_ref, o_ref,
     w1_ref, w3_ref, w2_ref, stage_cols, stage_rows, sem) = refs[N_MIX:]

    @pl.when(pl.program_id(0) == 0)
    def _():
        _load_weights_bf16(w1_hbm.at[layer], w3_hbm.at[layer], w2_hbm.at[layer], w1_ref, w3_ref, w2_ref,
                           stage_cols, stage_rows, sem, chunk)

    tm = o_ref.shape[0]
    groups = [slice(k, k + ROUTER_ROWS) for k in range(0, tm, ROUTER_ROWS)]
    xs = _mixer_residual(*mix, row_groups=groups)
    h = jnp.concatenate([(x * (1.0 + sc_ref[0]) + sh_ref[0]).astype(BF16) for x in xs], axis=0)
    y = _swiglu(h, w1_ref, w3_ref, w2_ref, chunk)
    for r, x in zip(groups, xs):
        o_ref[r, :] = _layer_norm(ALPHA * x + (1.0 + g_ref[0]) * y[r, :], lg_ref[...], lb_ref[...])


def _weight_scratch(d, ff, chunk):
    return [pltpu.VMEM((d, ff), BF16), pltpu.VMEM((d, ff), BF16), pltpu.VMEM((ff, d), BF16),
            pltpu.VMEM((2, d, chunk), F32), pltpu.VMEM((2, chunk, d), F32), pltpu.SemaphoreType.DMA((2,))]


def _ffn_call(mixer, sc, sh, g2, w1, w3, w2, layer, lg, lb, tiles_per_batch, tm, chunk):
    mix_arrays, mix_specs = mixer
    t, d = mix_arrays[4].shape
    ff = w1.shape[2]
    per_b = pl.BlockSpec((1, 1, d), lambda i: (i // tiles_per_batch, 0, 0))
    vec = pl.BlockSpec((1, d), lambda i: (0, 0))
    hbm = pl.BlockSpec(memory_space=pl.ANY)
    return pl.pallas_call(
        functools.partial(_ffn_kernel, layer=layer, chunk=chunk),
        out_shape=jax.ShapeDtypeStruct((t, d), F32),
        grid=(t // tm,),
        in_specs=mix_specs + [per_b, per_b, per_b, hbm, hbm, hbm, vec, vec],
        out_specs=pl.BlockSpec((tm, d), lambda i: (i, 0)),
        scratch_shapes=_weight_scratch(d, ff, chunk),
        compiler_params=_cparams(("arbitrary",), FFN_VMEM_LIMIT),
        name="ffn_dense",
    )(*mix_arrays, sc, sh, g2, w1, w3, w2, lg, lb)


ROUTE_G_OFF = 2
ROUTE_RANK_OFF = 4


def _router_kernel(*refs):
    sc_ref, sh_ref, wr_ref, tri_ref, x1_ref, h_ref, r_ref, cnt_ref = refs[N_MIX:]
    tm = x1_ref.shape[0]
    groups = [slice(k, k + ROUTER_ROWS) for k in range(0, tm, ROUTER_ROWS)]
    x1s = _mixer_residual(*refs[:N_MIX], row_groups=groups)
    hs = [x1 * (1.0 + sc_ref[0]) + sh_ref[0] for x1 in x1s]
    for r, x1, h in zip(groups, x1s, hs):
        x1_ref[r, :] = x1
        h_ref[r, :] = h.astype(BF16)
    logits = jnp.concatenate(
        [jnp.dot(h, wr_ref[...], preferred_element_type=F32, precision=lax.Precision.HIGHEST) for h in hs], axis=0)
    lane = _lane_ids(logits.shape)
    lg = jnp.where(lane < N_EXPERTS, logits, NEG_INF)
    m1 = jnp.max(lg, axis=-1, keepdims=True)
    i1 = jnp.min(jnp.where(lg == m1, lane, LANES), axis=-1, keepdims=True)
    lg2 = jnp.where(lane == i1, NEG_INF, lg)
    m2 = jnp.max(lg2, axis=-1, keepdims=True)
    i2 = jnp.min(jnp.where(lg2 == m2, lane, LANES), axis=-1, keepdims=True)
    e2 = jnp.exp(m2 - m1)
    den = 1.0 + e2
    chosen = (lane == i1) | (lane == i2)
    cum = _dot(tri_ref[...], chosen.astype(BF16))
    rank1 = jnp.sum(jnp.where(lane == i1, cum, 0.0), axis=-1, keepdims=True) - 1.0
    rank2 = jnp.sum(jnp.where(lane == i2, cum, 0.0), axis=-1, keepdims=True) - 1.0
    cnt_ref[0] = cum[cum.shape[0] - 1:, :]
    out = jnp.where(lane == 0, i1.astype(F32), 0.0)
    out = jnp.where(lane == 1, i2.astype(F32), out)
    out = jnp.where(lane == ROUTE_G_OFF, 1.0 / den, out)
    out = jnp.where(lane == ROUTE_G_OFF + 1, e2 / den, out)
    out = jnp.where(lane == ROUTE_RANK_OFF, rank1, out)
    out = jnp.where(lane == ROUTE_RANK_OFF + 1, rank2, out)
    r_ref[...] = out


def _router_call(mixer, sc, sh, wr_p, tiles_per_batch, tm):
    mix_arrays, mix_specs = mixer
    t, d = mix_arrays[4].shape
    per_b = pl.BlockSpec((1, 1, d), lambda i: (i // tiles_per_batch, 0, 0))
    tri = (jnp.arange(tm)[:, None] >= jnp.arange(tm)[None, :]).astype(BF16)
    tok = pl.BlockSpec((tm, d), lambda i: (i, 0))
    return pl.pallas_call(
        _router_kernel,
        out_shape=[jax.ShapeDtypeStruct((t, d), F32), jax.ShapeDtypeStruct((t, d), BF16),
                   jax.ShapeDtypeStruct((t, LANES), F32), jax.ShapeDtypeStruct((t // tm, 1, LANES), F32)],
        grid=(t // tm,),
        in_specs=mix_specs + [per_b, per_b, pl.BlockSpec(wr_p.shape, lambda i: (0, 0)),
                              pl.BlockSpec((tm, tm), lambda i: (0, 0))],
        out_specs=[tok, tok, pl.BlockSpec((tm, LANES), lambda i: (i, 0)),
                   pl.BlockSpec((1, 1, LANES), lambda i: (i, 0, 0))],
        compiler_params=_cparams(("parallel",)),
        name="moe_router",
    )(*mix_arrays, sc, sh, wr_p, tri)


def _expert_kernel(te_ref, nu_ref, x_ref, w1_hbm, w3_hbm, w2_hbm, o_ref,
                   w1_ref, w3_ref, w2_ref, stage_cols, stage_rows, sem, *, layer, chunk):
    i = pl.program_id(0)
    used = i < nu_ref[0]
    e = te_ref[i]
    new_expert = (i == 0) | (e != te_ref[jnp.maximum(i - 1, 0)])

    @pl.when(used & new_expert)
    def _():
        _load_weights_bf16(w1_hbm.at[layer, e], w3_hbm.at[layer, e], w2_hbm.at[layer, e], w1_ref, w3_ref, w2_ref,
                           stage_cols, stage_rows, sem, chunk)

    @pl.when(used)
    def _():
        o_ref[...] = _swiglu(x_ref[...], w1_ref, w3_ref, w2_ref, chunk).astype(o_ref.dtype)

    @pl.when(jnp.logical_not(used))
    def _():
        o_ref[...] = jnp.zeros_like(o_ref)


def _expert_call(tile_expert, n_used, xs, w1, w3, w2, layer, tm, chunk):
    p, d = xs.shape
    ff = w1.shape[3]
    hbm = pl.BlockSpec(memory_space=pl.ANY)
    return pl.pallas_call(
        functools.partial(_expert_kernel, layer=layer, chunk=chunk),
        out_shape=jax.ShapeDtypeStruct((p, d), BF16),
        grid_spec=pltpu.PrefetchScalarGridSpec(
            num_scalar_prefetch=2,
            grid=(p // tm,),
            in_specs=[pl.BlockSpec((tm, d), lambda i, te, nu: (i, 0)), hbm, hbm, hbm],
            out_specs=pl.BlockSpec((tm, d), lambda i, te, nu: (i, 0)),
            scratch_shapes=_weight_scratch(d, ff, chunk)),
        compiler_params=_cparams(("arbitrary",), FFN_VMEM_LIMIT),
        name="moe_experts",
    )(tile_expert, n_used, xs, w1, w3, w2)


def _combine_kernel(x_ref, ya_ref, yb_ref, r_ref, g_ref, lg_ref, lb_ref, o_ref):
    r = r_ref[...]
    lane = _lane_ids(r.shape)
    ga = jnp.sum(jnp.where(lane == ROUTE_G_OFF, r, 0.0), axis=-1, keepdims=True)
    gb = jnp.sum(jnp.where(lane == ROUTE_G_OFF + 1, r, 0.0), axis=-1, keepdims=True)
    y = ga * ya_ref[...].astype(F32) + gb * yb_ref[...].astype(F32)
    v = ALPHA * x_ref[...] + (1.0 + g_ref[0]) * y
    o_ref[...] = _layer_norm(v, lg_ref[...], lb_ref[...])


def _combine_call(x2d, ya, yb, route, g2, lg, lb, tiles_per_batch, tm):
    t, d = x2d.shape
    tok = pl.BlockSpec((tm, d), lambda i: (i, 0))
    per_b = pl.BlockSpec((1, 1, d), lambda i: (i // tiles_per_batch, 0, 0))
    vec = pl.BlockSpec((1, d), lambda i: (0, 0))
    return pl.pallas_call(
        _combine_kernel,
        out_shape=jax.ShapeDtypeStruct((t, d), F32),
        grid=(t // tm,),
        in_specs=[tok, tok, tok, pl.BlockSpec((tm, LANES), lambda i: (i, 0)), per_b, vec, vec],
        out_specs=tok,
        compiler_params=_cparams(("parallel",)),
        name="moe_combine_ln",
    )(x2d, ya, yb, route, g2, lg, lb)


def _dispatch_plan(route, counts, tm, tme):
    t = route.shape[0]
    nt = t // tm
    n = counts[:, 0, :N_EXPERTS].astype(jnp.int32)
    gran = (n + ROW_GRANULE - 1) // ROW_GRANULE
    seg = gran * ROW_GRANULE
    local_off = jnp.cumsum(seg, axis=1) - seg
    region = ((jnp.sum(seg, axis=0) + tme - 1) // tme) * tme
    ends = jnp.cumsum(region)
    seg_row = (ends - region)[None, :] + jnp.cumsum(seg, axis=0) - seg
    n_rows = _sorted_rows(t, tm, tme)
    tile_start = jnp.arange(n_rows // tme, dtype=jnp.int32) * tme
    tile_expert = jnp.minimum(jnp.sum((tile_start[:, None] >= ends[None, :]).astype(jnp.int32), axis=1),
                              N_EXPERTS - 1)
    n_used = (ends[-1] // tme).astype(jnp.int32).reshape(1)
    tables = (seg_row.reshape(-1), local_off.reshape(-1), gran.reshape(-1))
    return tables, tile_expert, n_used


def _sorted_rows(t, tm, tme):
    bound = 2 * t + (t // tm) * N_EXPERTS * (ROW_GRANULE - 1) + N_EXPERTS * (tme - 1)
    return (bound + tme - 1) // tme * tme


def _dispatch_kernel(row_ref, off_ref, gran_ref, h_ref, r_ref, xs_in, xs_out, dest_ref, sorted_ref, sem, *, tm):
    del xs_in
    i = pl.program_id(0)
    rt = r_ref[...].T
    e1, e2 = rt[0:1, :], rt[1:2, :]
    rank1, rank2 = rt[ROUTE_RANK_OFF:ROUTE_RANK_OFF + 1, :], rt[ROUTE_RANK_OFF + 1:ROUTE_RANK_OFF + 2, :]
    slot1, slot2, dest1, dest2 = rank1, rank2, rank1, rank2
    for e in range(N_EXPERTS):
        off = off_ref[i * N_EXPERTS + e].astype(F32)
        row = row_ref[i * N_EXPERTS + e].astype(F32)
        slot1 = slot1 + jnp.where(e1 == e, off, 0.0)
        slot2 = slot2 + jnp.where(e2 == e, off, 0.0)
        dest1 = dest1 + jnp.where(e1 == e, row, 0.0)
        dest2 = dest2 + jnp.where(e2 == e, row, 0.0)
    sub = lax.broadcasted_iota(jnp.int32, (dest_ref.shape[1], tm), 0)
    dest_ref[0] = jnp.where(sub == 0, dest1, jnp.where(sub == 1, dest2, 0.0)).astype(jnp.int32)
    rows = lax.broadcasted_iota(jnp.int32, (sorted_ref.shape[0], tm), 0).astype(F32)
    perm = ((rows == slot1) | (rows == slot2)).astype(BF16)
    sorted_ref[...] = _dot(perm, h_ref[...]).astype(BF16)

    def copy(e, g):
        src = pl.multiple_of(off_ref[i * N_EXPERTS + e] + g * ROW_GRANULE, ROW_GRANULE)
        dst = pl.multiple_of(row_ref[i * N_EXPERTS + e] + g * ROW_GRANULE, ROW_GRANULE)
        return pltpu.make_async_copy(sorted_ref.at[pl.ds(src, ROW_GRANULE)], xs_out.at[pl.ds(dst, ROW_GRANULE)], sem)

    for e in range(N_EXPERTS):
        lax.fori_loop(0, gran_ref[i * N_EXPERTS + e], lambda g, c, e=e: (copy(e, g).start(), c)[1], 0)
    for e in range(N_EXPERTS):
        lax.fori_loop(0, gran_ref[i * N_EXPERTS + e], lambda g, c, e=e: (copy(e, g).wait(), c)[1], 0)


def _dispatch_call(tables, h, route, n_rows, tm):
    t, d = h.shape
    buf_rows = 2 * tm + N_EXPERTS * ROW_GRANULE
    hbm = pl.BlockSpec(memory_space=pl.ANY)
    sub = 8
    xs, dest = pl.pallas_call(
        functools.partial(_dispatch_kernel, tm=tm),
        out_shape=[jax.ShapeDtypeStruct((n_rows, d), BF16), jax.ShapeDtypeStruct((t // tm, sub, tm), jnp.int32)],
        grid_spec=pltpu.PrefetchScalarGridSpec(
            num_scalar_prefetch=3,
            grid=(t // tm,),
            in_specs=[pl.BlockSpec((tm, d), lambda i, *_: (i, 0)), pl.BlockSpec((tm, LANES), lambda i, *_: (i, 0)),
                      hbm],
            out_specs=[hbm, pl.BlockSpec((1, sub, tm), lambda i, *_: (i, 0, 0))],
            scratch_shapes=[pltpu.VMEM((buf_rows, d), BF16), pltpu.SemaphoreType.DMA(())]),
        input_output_aliases={5: 0},
        compiler_params=_cparams(("arbitrary",)),
        name="moe_dispatch",
    )(*tables, h, route, jnp.zeros((n_rows, d), BF16))
    return xs, dest[:, 0, :].reshape(t), dest[:, 1, :].reshape(t)


def _moe_layer(mixer, sc, sh, g2, wr_p, w1, w3, w2, layer, lg, lb, tiles_per_batch, tm, tme, tf):
    x2d, h, route, counts = _router_call(mixer, sc, sh, wr_p, tiles_per_batch, tm)
    tables, tile_expert, n_used = _dispatch_plan(route, counts, tm, tme)
    xs, dest1, dest2 = _dispatch_call(tables, h, route, _sorted_rows(x2d.shape[0], tm, tme), tm)
    ys = _expert_call(tile_expert, n_used, xs, w1, w3, w2, layer, tme, tf)
    ya = ys.at[dest1].get(mode="promise_in_bounds")
    yb = ys.at[dest2].get(mode="promise_in_bounds")
    return _combine_call(x2d, ya, yb, route, g2, lg, lb, tiles_per_batch, tm)


def _trig_kernel(a_ref, c_ref, s_ref):
    a = a_ref[...]
    c_ref[...] = jnp.cos(a)
    s_ref[...] = jnp.sin(a)


def _trig_call(ang2d):
    n = ang2d.shape[0]
    tr = math.gcd(n, 1024)
    spec = pl.BlockSpec((tr, LANES), lambda i: (i, 0))
    return pl.pallas_call(
        _trig_kernel,
        out_shape=[jax.ShapeDtypeStruct(ang2d.shape, F32)] * 2,
        grid=(n // tr,),
        in_specs=[spec],
        out_specs=[spec, spec],
        compiler_params=_cparams(("parallel",)),
        name="rope_trig",
    )(ang2d)


def _rope_tables(positions):
    pos = positions.astype(F32)[..., None]
    lane = jnp.arange(LANES)
    inv64 = ROPE_THETA ** (-jnp.arange(0, RET_D, 2, dtype=F32) / RET_D)
    invm = ROPE_THETA ** (-jnp.arange(0, MLA_ROPE, 2, dtype=F32) / MLA_ROPE)
    n64 = inv64.shape[0]
    ang = pos * jnp.concatenate([inv64, invm])
    cos, sin = _trig_call(ang.reshape(-1, LANES))
    cos, sin = cos.reshape(ang.shape), sin.reshape(ang.shape)
    cos64, sin64, cosm, sinm = cos[..., :n64], sin[..., :n64], cos[..., n64:], sin[..., n64:]
    sign64 = jnp.where((lane % 64) < 32, -1.0, 1.0)
    c64 = jnp.tile(cos64, (1, 1, LANES // 32))
    s64 = jnp.tile(sin64, (1, 1, LANES // 32)) * sign64
    in_rope = (lane >= KR_LANE) & (lane < KR_LANE + MLA_ROPE)
    signm = jnp.where(lane < KR_LANE + MLA_ROPE // 2, -1.0, 1.0)
    cm = jnp.where(in_rope, jnp.tile(cosm, (1, 1, LANES // 16)), 1.0)
    sm = jnp.where(in_rope, jnp.tile(sinm, (1, 1, LANES // 16)) * signm, 0.0)
    return c64, s64, cm, sm


def _prep_mixer_weights(w_in, w_uq, w_ukv):
    d = w_in.shape[0]
    kr_cols = jnp.zeros((d, LANES), w_in.dtype).at[:, KR_LANE:KR_LANE + MLA_ROPE].set(
        w_in[:, MLA_OFF + MLA_Q_RANK + MLA_KV_RANK:])
    w_in_p = jnp.concatenate([w_in[:, :MLA_OFF + MLA_Q_RANK + MLA_KV_RANK], kr_cols], axis=1).astype(BF16)
    uq = w_uq.reshape(MLA_Q_RANK, MLA_HEADS, MLA_NOPE + MLA_ROPE)
    wuq_p = jnp.pad(uq, ((0, 0), (0, 0), (0, LANES - MLA_NOPE - MLA_ROPE))).reshape(MLA_Q_RANK, MLA_PAD).astype(BF16)
    ukv = w_ukv.reshape(MLA_KV_RANK, MLA_HEADS, MLA_NOPE + MLA_V)
    wkn_p = jnp.pad(ukv[:, :, :MLA_NOPE], ((0, 0), (0, 0), (0, LANES - MLA_NOPE))).reshape(MLA_KV_RANK, MLA_PAD).astype(BF16)
    wv = ukv[:, :, MLA_NOPE:].reshape(MLA_KV_RANK, MLA_VW).astype(BF16)
    return w_in_p, wuq_p, wkn_p, wv


def kernel(x, c, positions, w_in, ret_gn_g, mla_qn_g, mla_kvn_g, w_uq, w_ukv, w_out, w_ada, b_ada, ln1_g, ln1_b, ln2_g, ln2_b, w1_dense, w3_dense, w2_dense, w_router, w1_moe, w3_moe, w2_moe):
    b, s, d = x.shape
    tabs = _rope_tables(positions)
    ret_tabs = _ret_tables()
    mod = _ada_call(c, w_ada, b_ada)
    tm_tok = 512
    tiles_per_batch = s // tm_tok
    w_out_b = _cast_call(w_out)
    for l in range(w_in.shape[0]):
        sh1, sc1, g1, sh2, sc2, g2 = [mod[l, :, j * d:(j + 1) * d].reshape(b, 1, d) for j in range(6)]
        w_in_p, wuq_p, wkn_p, wv = _prep_mixer_weights(w_in[l], w_uq[l], w_ukv[l])
        (rq, rk, rv, rg), dil_views, (mq, mk, mv) = _inproj_call(
            x, sc1, sh1, w_in_p, tabs, mla_qn_g[l].reshape(1, -1), mla_kvn_g[l].reshape(1, -1), wuq_p, wkn_p, wv)
        ya = _ret_call(rq, rk, rv, rg, ret_gn_g[l].reshape(1, -1), ret_tabs)
        yb = _dil_call(dil_views)
        yc = _mla_call(mq, mk, mv)
        mixer = _mixer_operands(ya, yb, yc, w_out_b, l, x.reshape(b * s, d), g1, ln1_g[l].reshape(1, d),
                                ln1_b[l].reshape(1, d), tiles_per_batch, tm_tok)
        lg, lb = ln2_g[l].reshape(1, d), ln2_b[l].reshape(1, d)
        j = l // 2
        if l % 2 == 0:
            x2d = _ffn_call(mixer, sc2, sh2, g2, w1_dense, w3_dense, w2_dense, j, lg, lb, tiles_per_batch, tm_tok,
                            FF_CHUNK)
        else:
            wr_p = jnp.pad(w_router[j], ((0, 0), (0, LANES - N_EXPERTS)))
            x2d = _moe_layer(mixer, sc2, sh2, g2, wr_p, w1_moe, w3_moe, w2_moe, j, lg, lb, tiles_per_batch, tm_tok,
                             EXPERT_TILE, FF_CHUNK)
        x = x2d.reshape(b, s, d)
    return x
```

```python
import functools
import math

import jax
import jax.numpy as jnp
from jax import lax
from jax.experimental import pallas as pl
from jax.experimental.pallas import tpu as pltpu

D_MODEL = 1024
DEPTH = 4
RET_HEADS = 4
RET_D = 64
RET_CHUNK = 128
DIL_HEADS = 6
DIL_D = 64
DIL_DILATIONS = (1, 4, 16)
DIL_W = 128
MLA_HEADS = 6
MLA_Q_RANK = 384
MLA_KV_RANK = 256
MLA_NOPE = 64
MLA_ROPE = 32
MLA_V = 64
N_EXPERTS = 8
ROPE_THETA = 10000.0
LN_EPS = 1e-5
RMS_EPS = 1e-6
ALPHA = (2.0 * DEPTH) ** 0.25

LANES = 128
RET_W = RET_HEADS * RET_D
DIL_WD = DIL_HEADS * DIL_D
MLA_PAD = MLA_HEADS * LANES
MLA_VW = MLA_HEADS * MLA_V
RET_IN = 4 * RET_W
DIL_IN = 3 * DIL_WD
MLA_OFF = RET_IN + DIL_IN
D_IN_PAD = MLA_OFF + MLA_Q_RANK + MLA_KV_RANK + LANES
KR_LANE = MLA_NOPE

VMEM_LIMIT = 48 * 1024 * 1024
FFN_VMEM_LIMIT = 56 * 1024 * 1024
FF_CHUNK = 512
EXPERT_TILE = 512
CAST_BLOCK_BYTES = 4 * 1024 * 1024
ROW_GRANULE = 16
BF16 = jnp.bfloat16
F32 = jnp.float32
NEG_INF = float("-inf")
LOG2E = math.log2(math.e)
MLA_Q_SCALE = (MLA_NOPE + MLA_ROPE) ** -0.5 * LOG2E


def _cparams(sem, vmem_limit=VMEM_LIMIT):
    return pltpu.CompilerParams(dimension_semantics=sem, vmem_limit_bytes=vmem_limit)


def _dot(a, b):
    return jnp.dot(a, b, preferred_element_type=F32)


def _dot_nt(a, b):
    return lax.dot_general(a, b, (((1,), (1,)), ((), ())), preferred_element_type=F32)


def _dot_tn(a, b):
    return lax.dot_general(a, b, (((0,), (0,)), ((), ())), preferred_element_type=F32)


def _silu(x):
    return x * (1.0 / (1.0 + jnp.exp(-x)))


def _layer_norm(v, g, b):
    mu = jnp.mean(v, axis=-1, keepdims=True)
    d = v - mu
    var = jnp.mean(d * d, axis=-1, keepdims=True)
    return d * lax.rsqrt(var + LN_EPS) * g + b


def _lane_ids(shape):
    return lax.broadcasted_iota(jnp.int32, shape, len(shape) - 1)


def _rope_group(x, cos, sin_signed, first_half, half):
    fwd = pltpu.roll(x, LANES - half, 1)
    bwd = pltpu.roll(x, half, 1)
    return x * cos + jnp.where(first_half, fwd, bwd) * sin_signed


def _cast_kernel(w_ref, o_ref):
    o_ref[...] = w_ref[...].astype(BF16)


def _cast_call(w):
    cols = w.shape[-1]
    w2d = w.reshape(-1, cols)
    rows = w2d.shape[0]
    pack = 16
    tr = max(pack, CAST_BLOCK_BYTES // (4 * cols) // pack * pack)
    while rows % tr:
        tr -= pack
    spec = pl.BlockSpec((tr, cols), lambda i: (i, 0))
    out = pl.pallas_call(
        _cast_kernel,
        out_shape=jax.ShapeDtypeStruct(w2d.shape, BF16),
        grid=(rows // tr,),
        in_specs=[spec],
        out_specs=spec,
        compiler_params=_cparams(("parallel",)),
        name="cast_bf16",
    )(w2d)
    return out.reshape(w.shape)


def _ada_kernel(c_ref, w_ref, b_ref, o_ref):
    cond = _silu(c_ref[...])
    o_ref[0] = jnp.dot(cond, w_ref[0], preferred_element_type=F32,
                       precision=lax.Precision.HIGHEST) + b_ref[0]


def _ada_call(c, w_ada, b_ada):
    nl, d, n = w_ada.shape
    b = c.shape[0]
    tn = 1536
    return pl.pallas_call(
        _ada_kernel,
        out_shape=jax.ShapeDtypeStruct((nl, b, n), F32),
        grid=(nl, n // tn),
        in_specs=[pl.BlockSpec((b, d), lambda l, j: (0, 0)),
                  pl.BlockSpec((1, d, tn), lambda l, j: (l, 0, j)),
                  pl.BlockSpec((1, 1, tn), lambda l, j: (l, 0, j))],
        out_specs=pl.BlockSpec((1, b, tn), lambda l, j: (l, 0, j)),
        compiler_params=_cparams(("arbitrary", "arbitrary")),
        name="ada_mod",
    )(c, w_ada, b_ada.reshape(nl, 1, n))


def _inproj_kernel(x_ref, sc_ref, sh_ref, w_ref, c64_ref, s64_ref, cm_ref, sm_ref,
                   qg_ref, kvg_ref, wuq_ref, wkn_ref, wv_ref,
                   rq_ref, rk_ref, rv_ref, rg_ref,
                   dq1_ref, dk1_ref, dv1_ref, dq4_ref, dk4_ref, dv4_ref, dq16_ref, dk16_ref, dv16_ref,
                   mq_ref, mk_ref, mv_ref, scr_ref, *, tm):
    h = (x_ref[0] * (1.0 + sc_ref[0]) + sh_ref[0]).astype(BF16)
    c64 = c64_ref[0]
    s64 = s64_ref[0]
    cm = cm_ref[0]
    sm = sm_ref[0]
    lane = _lane_ids((1, LANES))
    first64 = (lane % 64) < 32
    firstm = lane < (KR_LANE + MLA_ROPE // 2)

    def rope64(zc):
        return _rope_group(zc, c64, s64, first64, 32)

    def ropem(zc):
        return _rope_group(zc, cm, sm, firstm, MLA_ROPE // 2)

    zm = _dot(h, w_ref[:, MLA_OFF:D_IN_PAD])
    cq = zm[:, 0:MLA_Q_RANK]
    ckv = zm[:, MLA_Q_RANK:MLA_Q_RANK + MLA_KV_RANK]
    kr = ropem(zm[:, MLA_Q_RANK + MLA_KV_RANK:])
    cqn = (cq * lax.rsqrt(jnp.mean(cq * cq, axis=-1, keepdims=True) + RMS_EPS) * qg_ref[...]).astype(BF16)
    ckvn = (ckv * lax.rsqrt(jnp.mean(ckv * ckv, axis=-1, keepdims=True) + RMS_EPS) * kvg_ref[...]).astype(BF16)
    q = _dot(cqn, wuq_ref[...])
    kn = _dot(ckvn, wkn_ref[...])
    for hh in range(MLA_HEADS):
        sl = slice(hh * LANES, (hh + 1) * LANES)
        mq_ref[0, :, sl] = (ropem(q[:, hh * LANES:(hh + 1) * LANES]) * MLA_Q_SCALE).astype(BF16)
        mk_ref[0, :, sl] = (kn[:, hh * LANES:(hh + 1) * LANES] + kr).astype(BF16)
    mv_ref[0] = _dot(ckvn, wv_ref[...]).astype(BF16)

    za = _dot(h, w_ref[:, 0:RET_IN])
    for j in range(RET_W // LANES):
        sl = slice(j * LANES, (j + 1) * LANES)
        rq_ref[0, :, sl] = rope64(za[:, j * LANES:(j + 1) * LANES]).astype(BF16)
        kc = za[:, RET_W + j * LANES:RET_W + (j + 1) * LANES]
        rk_ref[0, :, sl] = (rope64(kc) * (RET_D ** -0.5)).astype(BF16)
    rv_ref[0] = za[:, 2 * RET_W:3 * RET_W].astype(BF16)
    rg_ref[0] = za[:, 3 * RET_W:4 * RET_W]

    zb = _dot(h, w_ref[:, RET_IN:MLA_OFF])
    q_scale = (DIL_D ** -0.5) * LOG2E
    views = ((dq1_ref, dq4_ref, dq16_ref), (dk1_ref, dk4_ref, dk16_ref), (dv1_ref, dv4_ref, dv16_ref))
    for a, (n1, n4, n16) in enumerate(views):
        for j in range(DIL_WD // LANES):
            zc = zb[:, a * DIL_WD + j * LANES:a * DIL_WD + (j + 1) * LANES]
            if a == 0:
                zc = rope64(zc) * q_scale
            elif a == 1:
                zc = rope64(zc)
            scr_ref[j] = zc
            n1[0, :, j * LANES:(j + 1) * LANES] = zc.astype(BF16)
        for r, ref in ((4, n4), (16, n16)):
            for rho in range(r):
                for j in range(DIL_WD // LANES):
                    ref[0, :, rho * DIL_WD + j * LANES:rho * DIL_WD + (j + 1) * LANES] = (
                        scr_ref[j, pl.ds(rho, tm // r, stride=r), :].astype(BF16))

def _inproj_call(x, sc, sh, w_in_p, tabs, qg, kvg, wuq_p, wkn_p, wv, tm=512):
    b, s, d = x.shape
    c64, s64, cm, sm = tabs
    tok = lambda w: pl.BlockSpec((1, tm, w), lambda i, j: (i, j, 0))
    per_b = pl.BlockSpec((1, 1, d), lambda i, j: (i, 0, 0))
    full = lambda a: pl.BlockSpec(a.shape, lambda i, j: (0,) * a.ndim)
    outs = [(1, RET_W, BF16), (1, RET_W, BF16), (1, RET_W, BF16), (1, RET_W, F32)]
    outs += [(r, DIL_WD, BF16) for r in DIL_DILATIONS for _ in range(3)]
    outs += [(1, MLA_PAD, BF16), (1, MLA_PAD, BF16), (1, MLA_VW, BF16)]
    res = pl.pallas_call(
        functools.partial(_inproj_kernel, tm=tm),
        out_shape=[jax.ShapeDtypeStruct((b, s // r, r * w), dt) for r, w, dt in outs],
        grid=(b, s // tm),
        in_specs=[tok(d), per_b, per_b, full(w_in_p), tok(LANES), tok(LANES), tok(LANES), tok(LANES),
                  full(qg), full(kvg), full(wuq_p), full(wkn_p), full(wv)],
        out_specs=[pl.BlockSpec((1, tm // r, r * w), lambda i, j: (i, j, 0)) for r, w, _ in outs],
        scratch_shapes=[pltpu.VMEM((DIL_WD // LANES, tm, LANES), F32)],
        compiler_params=_cparams(("parallel", "parallel")),
        name="in_proj",
    )(x, sc, sh, w_in_p, c64, s64, cm, sm, qg, kvg, wuq_p, wkn_p, wv)
    ret = res[0:4]
    dil = [res[4 + 3 * i:7 + 3 * i] for i in range(len(DIL_DILATIONS))]
    mla = res[4 + 3 * len(DIL_DILATIONS):]
    return ret, dil, mla


def _ret_kernel(q_ref, k_ref, v_ref, g_ref, gn_ref, intra_ref, xi_ref, zeta_ref, dmat_ref, bmask_ref,
                o_ref, state_ref, *, chunks):
    @pl.when(pl.program_id(1) == 0)
    def _():
        state_ref[...] = jnp.zeros_like(state_ref)

    lane = _lane_ids((1, LANES))
    head_a = lane < RET_D
    c = RET_CHUNK
    npair = RET_W // LANES
    items = [(ci, p) for ci in range(chunks) for p in range(npair)]
    view = lambda ref, ci, p: ref[0, ci * c:(ci + 1) * c, p * LANES:(p + 1) * LANES]

    inner, kv_inc = {}, {}
    for ci, p in items:
        cols = slice(p * LANES, (p + 1) * LANES)
        q, k, v = view(q_ref, ci, p), view(k_ref, ci, p), view(v_ref, ci, p)
        zero = jnp.zeros_like(q)
        s_a = _dot_nt(jnp.where(head_a, q, zero), k) * intra_ref[2 * p]
        s_b = _dot_nt(jnp.where(head_a, zero, q), k) * intra_ref[2 * p + 1]
        s_cat = jnp.concatenate([s_a, s_b], axis=1).astype(BF16)
        v_stack = jnp.concatenate([jnp.where(head_a, v, zero), jnp.where(head_a, zero, v)], axis=0)
        inner[(ci, p)] = _dot(s_cat, v_stack)
        kz = (k.astype(F32) * zeta_ref[:, cols]).astype(BF16)
        kv_inc[(ci, p)] = _dot_tn(kz, v) * bmask_ref[...]

    outs = {}
    for p in range(npair):
        cols = slice(p * LANES, (p + 1) * LANES)
        state = state_ref[p]
        for ci in range(chunks):
            cross = _dot(view(q_ref, ci, p), state.astype(BF16)) * xi_ref[:, cols]
            outs[(ci, p)] = inner[(ci, p)] + cross
            state = state * dmat_ref[p] + kv_inc[(ci, p)]
        state_ref[p] = state

    inv_n = 1.0 / RET_D
    for ci, p in items:
        cols = slice(p * LANES, (p + 1) * LANES)
        o = outs[(ci, p)]
        mu = jnp.where(head_a,
                       jnp.sum(jnp.where(head_a, o, 0.0), axis=-1, keepdims=True),
                       jnp.sum(jnp.where(head_a, 0.0, o), axis=-1, keepdims=True)) * inv_n
        d = o - mu
        dd = d * d
        var = jnp.where(head_a,
                        jnp.sum(jnp.where(head_a, dd, 0.0), axis=-1, keepdims=True),
                        jnp.sum(jnp.where(head_a, 0.0, dd), axis=-1, keepdims=True)) * inv_n
        on = d * lax.rsqrt(var + LN_EPS) * gn_ref[:, cols]
        o_ref[0, ci * c:(ci + 1) * c, cols] = (_silu(view(g_ref, ci, p)) * on).astype(BF16)


def _ret_tables():
    c = RET_CHUNK
    log_g = jnp.log(1.0 - 2.0 ** (-5.0 - jnp.arange(RET_HEADS, dtype=F32)))
    idx = jnp.arange(c, dtype=F32)
    diff = idx[:, None] - idx[None, :]
    intra = jnp.where(diff >= 0, jnp.exp(jnp.maximum(diff, 0.0) * log_g[:, None, None]), 0.0)
    xi = jnp.exp((idx[:, None] + 1.0) * log_g[None, :])
    zeta = jnp.exp((c - 1.0 - idx[:, None]) * log_g[None, :])
    decay = jnp.exp(c * log_g)
    xi_l = jnp.repeat(xi, RET_D, axis=1)
    zeta_l = jnp.repeat(zeta, RET_D, axis=1)
    r = jnp.arange(LANES)
    same = (r[:, None] // RET_D) == (r[None, :] // RET_D)
    bmask = same.astype(F32)
    dec_l = jnp.repeat(decay, RET_D).reshape(RET_W // LANES, 1, LANES)
    dmat = bmask[None] * dec_l
    return intra, xi_l, zeta_l, dmat, bmask


def _ret_call(rq, rk, rv, rg, gn, tables, chunks=8):
    b, s, w = rq.shape
    tm = chunks * RET_CHUNK
    intra, xi_l, zeta_l, dmat, bmask = tables
    tok = pl.BlockSpec((1, tm, w), lambda i, j: (i, j, 0))
    full = lambda a: pl.BlockSpec(a.shape, lambda i, j: (0,) * a.ndim)
    return pl.pallas_call(
        functools.partial(_ret_kernel, chunks=chunks),
        out_shape=jax.ShapeDtypeStruct((b, s, w), BF16),
        grid=(b, s // tm),
        in_specs=[tok, tok, tok, tok, full(gn), full(intra), full(xi_l), full(zeta_l), full(dmat), full(bmask)],
        out_specs=tok,
        scratch_shapes=[pltpu.VMEM((w // LANES, LANES, LANES), F32)],
        compiler_params=_cparams(("parallel", "arbitrary")),
        name="retention",
    )(rq, rk, rv, rg, gn, intra, xi_l, zeta_l, dmat, bmask)


def _dil_band(first_block):
    w = DIL_W
    qi = lax.broadcasted_iota(jnp.int32, (w, 2 * w), 0)
    kj = lax.broadcasted_iota(jnp.int32, (w, 2 * w), 1)
    dist = w + qi - kj
    band = (dist >= 0) & (dist <= w)
    if first_block is False:
        return band
    return band & (jnp.logical_not(first_block) | (kj >= w))


DIL_GROUP = 2


def _dil_attend(items, valid_of):
    head_a = _lane_ids((1, LANES)) < DIL_D
    scores = []
    for idx, (q, kcat, _) in enumerate(items):
        zero = jnp.zeros_like(q)
        valid = valid_of(idx)
        scores.append([jnp.where(valid, _dot_nt(qh, kcat), NEG_INF)
                       for qh in (jnp.where(head_a, q, zero), jnp.where(head_a, zero, q))])
    soft = []
    for scs in scores:
        ms, ls, ps = [], [], []
        for sc in scs:
            m = jnp.max(sc, axis=-1, keepdims=True)
            pe = jnp.exp2(sc - m)
            ms.append(m)
            ls.append(jnp.sum(pe, axis=-1, keepdims=True))
            ps.append(pe.astype(BF16))
        soft.append((ms, ls, jnp.concatenate(ps, axis=1)))
    out = []
    for (_, _, vcat), (ms, ls, pcat) in zip(items, soft):
        zv = jnp.zeros_like(vcat)
        v_stack = jnp.concatenate([jnp.where(head_a, vcat, zv), jnp.where(head_a, zv, vcat)], axis=0)
        out.append((_dot(pcat, v_stack), ms, ls))
    return out


def _dil_normalise(acc, ms, ls):
    head_a = _lane_ids((1, LANES)) < DIL_D
    o = acc * jnp.where(head_a, 1.0 / ls[0], 1.0 / ls[1])
    lse = jnp.where(head_a, ms[0] + jnp.log2(ls[0]), ms[1] + jnp.log2(ls[1]))
    return o, lse


def _dil_partial_kernel(q_ref, kp_ref, kc_ref, vp_ref, vc_ref, o_ref, lse_ref, *, r):
    valid = _dil_band(pl.program_id(1) == 0)
    npair = DIL_WD // LANES
    for g0 in range(0, r, DIL_GROUP):
        rhos = range(g0, min(g0 + DIL_GROUP, r))
        items = []
        for rho in rhos:
            for p in range(npair):
                cols = slice(rho * DIL_WD + p * LANES, rho * DIL_WD + (p + 1) * LANES)
                items.append((q_ref[0, :, cols],
                              jnp.concatenate([kp_ref[0, :, cols], kc_ref[0, :, cols]], axis=0),
                              jnp.concatenate([vp_ref[0, :, cols], vc_ref[0, :, cols]], axis=0)))
        res = _dil_attend(items, lambda idx: valid)
        for gi, rho in enumerate(rhos):
            rows = pl.ds(rho, DIL_W, stride=r)
            for p in range(npair):
                o, lse = _dil_normalise(*res[gi * npair + p])
                o_ref[0, p, rows, :] = o
                lse_ref[0, p, rows, :] = lse


def _dil_final_kernel(q_ref, k_ref, kh_ref, v_ref, vh_ref, o4_ref, lse4_ref, o16_ref, lse16_ref, o_ref, *, nblk):
    w = DIL_W
    npair = DIL_WD // LANES
    band_first = _dil_band(pl.program_id(1) == 0)
    band = _dil_band(False)

    for g0 in range(0, nblk, DIL_GROUP):
        blocks = range(g0, min(g0 + DIL_GROUP, nblk))
        items = []
        for i in blocks:
            rows = slice(i * w, (i + 1) * w)
            for p in range(npair):
                cols = slice(p * LANES, (p + 1) * LANES)
                if i == 0:
                    kcat = jnp.concatenate([kh_ref[0, :, cols], k_ref[0, rows, cols]], axis=0)
                    vcat = jnp.concatenate([vh_ref[0, :, cols], v_ref[0, rows, cols]], axis=0)
                else:
                    kcat = k_ref[0, (i - 1) * w:(i + 1) * w, cols]
                    vcat = v_ref[0, (i - 1) * w:(i + 1) * w, cols]
                items.append((q_ref[0, rows, cols], kcat, vcat))
        res = _dil_attend(items, lambda idx: band_first if blocks[idx // npair] == 0 else band)
        for gi, i in enumerate(blocks):
            rows = slice(i * w, (i + 1) * w)
            for p in range(npair):
                o1, lse1 = _dil_normalise(*res[gi * npair + p])
                lse4 = lse4_ref[0, p, rows, :]
                lse16 = lse16_ref[0, p, rows, :]
                top = jnp.maximum(jnp.maximum(lse1, lse4), lse16)
                w1, w4, w16 = jnp.exp2(lse1 - top), jnp.exp2(lse4 - top), jnp.exp2(lse16 - top)
                num = w1 * o1 + w4 * o4_ref[0, p, rows, :] + w16 * o16_ref[0, p, rows, :]
                o_ref[0, rows, p * LANES:(p + 1) * LANES] = (num / (w1 + w4 + w16)).astype(BF16)


def _dil_call(views, nblk=4):
    (q1, k1, v1) = views[0]
    b, s, w = q1.shape
    npair = w // LANES
    partial = []
    for r, (q, k, v) in zip(DIL_DILATIONS[1:], views[1:]):
        cur = pl.BlockSpec((1, DIL_W, r * w), lambda bb, n: (bb, n, 0))
        prev = pl.BlockSpec((1, DIL_W, r * w), lambda bb, n: (bb, jnp.maximum(n - 1, 0), 0))
        partial += pl.pallas_call(
            functools.partial(_dil_partial_kernel, r=r),
            out_shape=[jax.ShapeDtypeStruct((b, npair, s, LANES), F32)] * 2,
            grid=(b, s // (r * DIL_W)),
            in_specs=[cur, prev, cur, prev, cur],
            out_specs=[pl.BlockSpec((1, npair, r * DIL_W, LANES), lambda bb, n: (bb, 0, n, 0))] * 2,
            compiler_params=_cparams(("parallel", "arbitrary")),
            name=f"dilated_r{r}",
        )(q, k, k, v, v)
    tm = nblk * DIL_W
    tok = lambda width: pl.BlockSpec((1, tm, width), lambda bb, n: (bb, n, 0))
    halo = pl.BlockSpec((1, DIL_W, w), lambda bb, n: (bb, jnp.maximum(n * nblk - 1, 0), 0))
    accs = pl.BlockSpec((1, npair, tm, LANES), lambda bb, n: (bb, 0, n, 0))
    return pl.pallas_call(
        functools.partial(_dil_final_kernel, nblk=nblk),
        out_shape=jax.ShapeDtypeStruct((b, s, w), BF16),
        grid=(b, s // tm),
        in_specs=[tok(w), tok(w), halo, tok(w), halo, accs, accs, accs, accs],
        out_specs=tok(w),
        compiler_params=_cparams(("parallel", "arbitrary")),
        name="dilated_r1_merge",
    )(q1, k1, k1, v1, v1, *partial)


def _mla_kernel(q_ref, k_ref, v_ref, o_ref, m_ref, l_ref, acc_ref, *, t, sub):
    qi = pl.program_id(2)
    ki = pl.program_id(3)
    head_a = _lane_ids((1, LANES)) < MLA_V
    nsub = t // sub

    @pl.when(ki == 0)
    def _():
        m_ref[...] = jnp.full_like(m_ref, NEG_INF)
        l_ref[...] = jnp.zeros_like(l_ref)
        acc_ref[...] = jnp.zeros_like(acc_ref)

    def v_stack(c):
        v = v_ref[0, c * sub:(c + 1) * sub]
        zv = jnp.zeros_like(v)
        return jnp.concatenate([jnp.where(head_a, v, zv), jnp.where(head_a, zv, v)], axis=0)

    def column(c, rs, diag_r):
        vs = v_stack(c)
        scores = {}
        for r in rs:
            rows = slice(r * sub, (r + 1) * sub)
            for j in range(2):
                q = q_ref[0, rows, j * LANES:(j + 1) * LANES]
                k = k_ref[0, c * sub:(c + 1) * sub, j * LANES:(j + 1) * LANES]
                s = _dot_nt(q, k)
                if r == diag_r:
                    row = lax.broadcasted_iota(jnp.int32, (sub, sub), 0)
                    colk = lax.broadcasted_iota(jnp.int32, (sub, sub), 1)
                    s = jnp.where(colk <= row, s, NEG_INF)
                scores[(r, j)] = s
        probs = {}
        for r in rs:
            rows = slice(r * sub, (r + 1) * sub)
            ps, alphas = [], []
            for j in range(2):
                s = scores[(r, j)]
                m_old = m_ref[j, rows]
                m_new = jnp.maximum(m_old, jnp.max(s, axis=-1, keepdims=True))
                alpha = jnp.exp2(m_old - m_new)
                pe = jnp.exp2(s - jnp.concatenate([m_new] * (sub // LANES), axis=1))
                l_ref[j, rows] = alpha * l_ref[j, rows] + jnp.sum(pe, axis=-1, keepdims=True)
                m_ref[j, rows] = m_new
                ps.append(pe.astype(BF16))
                alphas.append(alpha)
            probs[r] = (jnp.concatenate(ps, axis=1), jnp.where(head_a, alphas[0], alphas[1]))
        for r in rs:
            rows = slice(r * sub, (r + 1) * sub)
            pcat, alpha = probs[r]
            acc_ref[rows] = acc_ref[rows] * alpha + _dot(pcat, vs)

    @pl.when(ki < qi)
    def _():
        for c in range(nsub):
            column(c, range(nsub), None)

    @pl.when(ki == qi)
    def _():
        for c in range(nsub):
            column(c, range(c, nsub), c)
        o_ref[0] = (acc_ref[...] / jnp.where(head_a, l_ref[0], l_ref[1])).astype(BF16)


def _mla_call(mq, mk, mv, t=2048, sub=512):
    b, s, _ = mq.shape
    n = s // t
    return pl.pallas_call(
        functools.partial(_mla_kernel, t=t, sub=sub),
        out_shape=jax.ShapeDtypeStruct((b, s, MLA_VW), BF16),
        grid=(b, MLA_HEADS // 2, n, n),
        in_specs=[pl.BlockSpec((1, t, 2 * LANES), lambda bb, p, i, j: (bb, i, p)),
                  pl.BlockSpec((1, t, 2 * LANES), lambda bb, p, i, j: (bb, jnp.minimum(j, i), p)),
                  pl.BlockSpec((1, t, LANES), lambda bb, p, i, j: (bb, jnp.minimum(j, i), p))],
        out_specs=pl.BlockSpec((1, t, LANES), lambda bb, p, i, j: (bb, i, p)),
        scratch_shapes=[pltpu.VMEM((2, t, LANES), F32), pltpu.VMEM((2, t, LANES), F32),
                        pltpu.VMEM((t, LANES), F32)],
        compiler_params=_cparams(("parallel", "parallel", "parallel", "arbitrary")),
        name="mla_attention",
    )(mq, mk, mv)


N_MIX = 8
ROUTER_ROWS = 128


def _mixer_residual(ya_ref, yb_ref, yc_ref, w_ref, x_ref, g_ref, lg_ref, lb_ref, row_groups=None):
    a, b = RET_W, RET_W + DIL_WD
    groups = [slice(None)] if row_groups is None else row_groups
    ys = [_dot(ya_ref[r, :], w_ref[0, 0:a, :]) + _dot(yb_ref[r, :], w_ref[0, a:b, :])
          + _dot(yc_ref[r, :], w_ref[0, b:, :]) for r in groups]
    out = [_layer_norm(ALPHA * x_ref[r, :] + (1.0 + g_ref[0]) * y, lg_ref[...], lb_ref[...])
           for r, y in zip(groups, ys)]
    return out[0] if row_groups is None else out


def _mixer_operands(ya, yb, yc, w_out, layer, x2d, g1, lg, lb, tiles_per_batch, tm):
    t, d = x2d.shape
    tok = lambda a: pl.BlockSpec((tm, a.shape[-1]), lambda i, *_: (i, 0))
    flat = lambda a: a.reshape(t, a.shape[-1])
    arrays = [flat(ya), flat(yb), flat(yc), w_out, x2d, g1, lg, lb]
    specs = [tok(ya), tok(yb), tok(yc),
             pl.BlockSpec((1,) + w_out.shape[1:], lambda i, *_: (layer, 0, 0)),
             tok(x2d), pl.BlockSpec((1, 1, d), lambda i, *_: (i // tiles_per_batch, 0, 0)),
             pl.BlockSpec((1, d), lambda i, *_: (0, 0)), pl.BlockSpec((1, d), lambda i, *_: (0, 0))]
    return arrays, specs


def _swiglu(h, w1_ref, w3_ref, w2_ref, chunk):
    ff = w1_ref.shape[1]
    y = None
    for a in range(0, ff, chunk):
        b = min(a + chunk, ff)
        mid = (_silu(_dot(h, w1_ref[:, a:b])) * _dot(h, w3_ref[:, a:b])).astype(BF16)
        part = _dot(mid, w2_ref[a:b, :])
        y = part if y is None else y + part
    return y


def _load_weights_bf16(w1_hbm, w3_hbm, w2_hbm, w1_ref, w3_ref, w2_ref, stage_cols, stage_rows, sem, chunk):
    ff = w1_ref.shape[1]
    jobs = []
    for a in range(0, ff, chunk):
        b = min(a + chunk, ff)
        for src, dst in ((w1_hbm, w1_ref), (w3_hbm, w3_ref)):
            jobs.append((src.at[:, a:b], lambda s, n=b - a: stage_cols.at[s, :, 0:n], dst.at[:, a:b]))
        jobs.append((w2_hbm.at[a:b, :], lambda s, n=b - a: stage_rows.at[s, 0:n, :], w2_ref.at[a:b, :]))

    def copy(k):
        src, stage, _ = jobs[k]
        return pltpu.make_async_copy(src, stage(k % 2), sem.at[k % 2])

    copy(0).start()
    for k in range(len(jobs)):
        if k + 1 < len(jobs):
            copy(k + 1).start()
        copy(k).wait()
        _, stage, dst = jobs[k]
        dst[...] = stage(k % 2)[...].astype(BF16)


def _ffn_kernel(*refs, layer, chunk):
    mix = refs[:N_MIX]
    (sc_ref, sh_ref, g_ref, w1_hbm, w3_hbm, w2_hbm, lg_ref, lb_ref, o_ref,
     w1_ref, w3_ref, w2_ref, stage_cols, stage_rows, sem) = refs[N_MIX:]

    @pl.when(pl.program_id(0) == 0)
    def _():
        _load_weights_bf16(w1_hbm.at[layer], w3_hbm.at[layer], w2_hbm.at[layer], w1_ref, w3_ref, w2_ref,
                           stage_cols, stage_rows, sem, chunk)

    tm = o_ref.shape[0]
    groups = [slice(k, k + ROUTER_ROWS) for k in range(0, tm, ROUTER_ROWS)]
    xs = _mixer_residual(*mix, row_groups=groups)
    h = jnp.concatenate([(x * (1.0 + sc_ref[0]) + sh_ref[0]).astype(BF16) for x in xs], axis=0)
    y = _swiglu(h, w1_ref, w3_ref, w2_ref, chunk)
    for r, x in zip(groups, xs):
        o_ref[r, :] = _layer_norm(ALPHA * x + (1.0 + g_ref[0]) * y[r, :], lg_ref[...], lb_ref[...])


def _weight_scratch(d, ff, chunk):
    return [pltpu.VMEM((d, ff), BF16), pltpu.VMEM((d, ff), BF16), pltpu.VMEM((ff, d), BF16),
            pltpu.VMEM((2, d, chunk), F32), pltpu.VMEM((2, chunk, d), F32), pltpu.SemaphoreType.DMA((2,))]


def _ffn_call(mixer, sc, sh, g2, w1, w3, w2, layer, lg, lb, tiles_per_batch, tm, chunk):
    mix_arrays, mix_specs = mixer
    t, d = mix_arrays[4].shape
    ff = w1.shape[2]
    per_b = pl.BlockSpec((1, 1, d), lambda i: (i // tiles_per_batch, 0, 0))
    vec = pl.BlockSpec((1, d), lambda i: (0, 0))
    hbm = pl.BlockSpec(memory_space=pl.ANY)
    return pl.pallas_call(
        functools.partial(_ffn_kernel, layer=layer, chunk=chunk),
        out_shape=jax.ShapeDtypeStruct((t, d), F32),
        grid=(t // tm,),
        in_specs=mix_specs + [per_b, per_b, per_b, hbm, hbm, hbm, vec, vec],
        out_specs=pl.BlockSpec((tm, d), lambda i: (i, 0)),
        scratch_shapes=_weight_scratch(d, ff, chunk),
        compiler_params=_cparams(("arbitrary",), FFN_VMEM_LIMIT),
        name="ffn_dense",
    )(*mix_arrays, sc, sh, g2, w1, w3, w2, lg, lb)


ROUTE_G_OFF = 2
ROUTE_RANK_OFF = 4


def _router_kernel(*refs):
    sc_ref, sh_ref, wrh_ref, wrl_ref, tri_ref, x1_ref, h_ref, r_ref, cnt_ref = refs[N_MIX:]
    tm = x1_ref.shape[0]
    groups = [slice(k, k + ROUTER_ROWS) for k in range(0, tm, ROUTER_ROWS)]
    x1s = _mixer_residual(*refs[:N_MIX], row_groups=groups)
    hs = [x1 * (1.0 + sc_ref[0]) + sh_ref[0] for x1 in x1s]
    parts = []
    for r, x1, h in zip(groups, x1s, hs):
        x1_ref[r, :] = x1
        h_hi = h.astype(BF16)
        h_ref[r, :] = h_hi
        h_lo = (h - h_hi.astype(F32)).astype(BF16)
        parts.append(_dot(h_hi, wrh_ref[...]) + (_dot(h_lo, wrh_ref[...]) + _dot(h_hi, wrl_ref[...])))
    logits = jnp.concatenate(parts, axis=0)
    lane = _lane_ids(logits.shape)
    lg = jnp.where(lane < N_EXPERTS, logits, NEG_INF)
    m1 = jnp.max(lg, axis=-1, keepdims=True)
    i1 = jnp.min(jnp.where(lg == m1, lane, LANES), axis=-1, keepdims=True)
    lg2 = jnp.where(lane == i1, NEG_INF, lg)
    m2 = jnp.max(lg2, axis=-1, keepdims=True)
    i2 = jnp.min(jnp.where(lg2 == m2, lane, LANES), axis=-1, keepdims=True)
    e2 = jnp.exp(m2 - m1)
    den = 1.0 + e2
    chosen = (lane == i1) | (lane == i2)
    cum = _dot(tri_ref[...], chosen.astype(BF16))
    rank1 = jnp.sum(jnp.where(lane == i1, cum, 0.0), axis=-1, keepdims=True) - 1.0
    rank2 = jnp.sum(jnp.where(lane == i2, cum, 0.0), axis=-1, keepdims=True) - 1.0
    cnt_ref[0] = cum[cum.shape[0] - 1:, :]
    out = jnp.where(lane == 0, i1.astype(F32), 0.0)
    out = jnp.where(lane == 1, i2.astype(F32), out)
    out = jnp.where(lane == ROUTE_G_OFF, 1.0 / den, out)
    out = jnp.where(lane == ROUTE_G_OFF + 1, e2 / den, out)
    out = jnp.where(lane == ROUTE_RANK_OFF, rank1, out)
    out = jnp.where(lane == ROUTE_RANK_OFF + 1, rank2, out)
    r_ref[...] = out


def _router_call(mixer, sc, sh, wr_p, tiles_per_batch, tm):
    mix_arrays, mix_specs = mixer
    t, d = mix_arrays[4].shape
    per_b = pl.BlockSpec((1, 1, d), lambda i: (i // tiles_per_batch, 0, 0))
    tri = (jnp.arange(tm)[:, None] >= jnp.arange(tm)[None, :]).astype(BF16)
    wr_hi = wr_p.astype(BF16)
    wr_lo = (wr_p - wr_hi.astype(F32)).astype(BF16)
    wspec = pl.BlockSpec(wr_p.shape, lambda i: (0, 0))
    tok = pl.BlockSpec((tm, d), lambda i: (i, 0))
    return pl.pallas_call(
        _router_kernel,
        out_shape=[jax.ShapeDtypeStruct((t, d), F32), jax.ShapeDtypeStruct((t, d), BF16),
                   jax.ShapeDtypeStruct((t, LANES), F32), jax.ShapeDtypeStruct((t // tm, 1, LANES), F32)],
        grid=(t // tm,),
        in_specs=mix_specs + [per_b, per_b, wspec, wspec, pl.BlockSpec((tm, tm), lambda i: (0, 0))],
        out_specs=[tok, tok, pl.BlockSpec((tm, LANES), lambda i: (i, 0)),
                   pl.BlockSpec((1, 1, LANES), lambda i: (i, 0, 0))],
        compiler_params=_cparams(("parallel",)),
        name="moe_router",
    )(*mix_arrays, sc, sh, wr_hi, wr_lo, tri)


def _expert_kernel(te_ref, nu_ref, x_ref, w1_hbm, w3_hbm, w2_hbm, o_ref,
                   w1_ref, w3_ref, w2_ref, stage_cols, stage_rows, sem, *, layer, chunk):
    i = pl.program_id(0)
    used = i < nu_ref[0]
    e = te_ref[i]
    new_expert = (i == 0) | (e != te_ref[jnp.maximum(i - 1, 0)])

    @pl.when(used & new_expert)
    def _():
        _load_weights_bf16(w1_hbm.at[layer, e], w3_hbm.at[layer, e], w2_hbm.at[layer, e], w1_ref, w3_ref, w2_ref,
                           stage_cols, stage_rows, sem, chunk)

    @pl.when(used)
    def _():
        o_ref[...] = _swiglu(x_ref[...], w1_ref, w3_ref, w2_ref, chunk).astype(o_ref.dtype)

    @pl.when(jnp.logical_not(used))
    def _():
        o_ref[...] = jnp.zeros_like(o_ref)


def _expert_call(tile_expert, n_used, xs, w1, w3, w2, layer, tm, chunk):
    p, d = xs.shape
    ff = w1.shape[3]
    hbm = pl.BlockSpec(memory_space=pl.ANY)
    return pl.pallas_call(
        functools.partial(_expert_kernel, layer=layer, chunk=chunk),
        out_shape=jax.ShapeDtypeStruct((p, d), BF16),
        grid_spec=pltpu.PrefetchScalarGridSpec(
            num_scalar_prefetch=2,
            grid=(p // tm,),
            in_specs=[pl.BlockSpec((tm, d), lambda i, te, nu: (i, 0)), hbm, hbm, hbm],
            out_specs=pl.BlockSpec((tm, d), lambda i, te, nu: (i, 0)),
            scratch_shapes=_weight_scratch(d, ff, chunk)),
        compiler_params=_cparams(("arbitrary",), FFN_VMEM_LIMIT),
        name="moe_experts",
    )(tile_expert, n_used, xs, w1, w3, w2)


def _combine_kernel(x_ref, ya_ref, yb_ref, r_ref, g_ref, lg_ref, lb_ref, o_ref):
    r = r_ref[...]
    lane = _lane_ids(r.shape)
    ga = jnp.sum(jnp.where(lane == ROUTE_G_OFF, r, 0.0), axis=-1, keepdims=True)
    gb = jnp.sum(jnp.where(lane == ROUTE_G_OFF + 1, r, 0.0), axis=-1, keepdims=True)
    y = ga * ya_ref[...].astype(F32) + gb * yb_ref[...].astype(F32)
    v = ALPHA * x_ref[...] + (1.0 + g_ref[0]) * y
    o_ref[...] = _layer_norm(v, lg_ref[...], lb_ref[...])


def _combine_call(x2d, ya, yb, route, g2, lg, lb, tiles_per_batch, tm):
    t, d = x2d.shape
    tok = pl.BlockSpec((tm, d), lambda i: (i, 0))
    per_b = pl.BlockSpec((1, 1, d), lambda i: (i // tiles_per_batch, 0, 0))
    vec = pl.BlockSpec((1, d), lambda i: (0, 0))
    return pl.pallas_call(
        _combine_kernel,
        out_shape=jax.ShapeDtypeStruct((t, d), F32),
        grid=(t // tm,),
        in_specs=[tok, tok, tok, pl.BlockSpec((tm, LANES), lambda i: (i, 0)), per_b, vec, vec],
        out_specs=tok,
        compiler_params=_cparams(("parallel",)),
        name="moe_combine_ln",
    )(x2d, ya, yb, route, g2, lg, lb)


def _dispatch_plan(route, counts, tm, tme):
    t = route.shape[0]
    nt = t // tm
    n = counts[:, 0, :N_EXPERTS].astype(jnp.int32)
    gran = (n + ROW_GRANULE - 1) // ROW_GRANULE
    seg = gran * ROW_GRANULE
    local_off = jnp.cumsum(seg, axis=1) - seg
    region = ((jnp.sum(seg, axis=0) + tme - 1) // tme) * tme
    ends = jnp.cumsum(region)
    seg_row = (ends - region)[None, :] + jnp.cumsum(seg, axis=0) - seg
    n_rows = _sorted_rows(t, tm, tme)
    tile_start = jnp.arange(n_rows // tme, dtype=jnp.int32) * tme
    tile_expert = jnp.minimum(jnp.sum((tile_start[:, None] >= ends[None, :]).astype(jnp.int32), axis=1),
                              N_EXPERTS - 1)
    n_used = (ends[-1] // tme).astype(jnp.int32).reshape(1)
    tables = (seg_row.reshape(-1), local_off.reshape(-1), gran.reshape(-1))
    return tables, tile_expert, n_used


def _sorted_rows(t, tm, tme):
    bound = 2 * t + (t // tm) * N_EXPERTS * (ROW_GRANULE - 1) + N_EXPERTS * (tme - 1)
    return (bound + tme - 1) // tme * tme


def _dispatch_kernel(row_ref, off_ref, gran_ref, h_ref, r_ref, xs_in, xs_out, dest_ref, sorted_ref, sem, *, tm):
    del xs_in
    i = pl.program_id(0)
    rt = r_ref[...].T
    e1, e2 = rt[0:1, :], rt[1:2, :]
    rank1, rank2 = rt[ROUTE_RANK_OFF:ROUTE_RANK_OFF + 1, :], rt[ROUTE_RANK_OFF + 1:ROUTE_RANK_OFF + 2, :]
    slot1, slot2, dest1, dest2 = rank1, rank2, rank1, rank2
    for e in range(N_EXPERTS):
        off = off_ref[i * N_EXPERTS + e].astype(F32)
        row = row_ref[i * N_EXPERTS + e].astype(F32)
        slot1 = slot1 + jnp.where(e1 == e, off, 0.0)
        slot2 = slot2 + jnp.where(e2 == e, off, 0.0)
        dest1 = dest1 + jnp.where(e1 == e, row, 0.0)
        dest2 = dest2 + jnp.where(e2 == e, row, 0.0)
    sub = lax.broadcasted_iota(jnp.int32, (dest_ref.shape[1], tm), 0)
    dest_ref[0] = jnp.where(sub == 0, dest1, jnp.where(sub == 1, dest2, 0.0)).astype(jnp.int32)
    rows = lax.broadcasted_iota(jnp.int32, (sorted_ref.shape[0], tm), 0).astype(F32)
    perm = ((rows == slot1) | (rows == slot2)).astype(BF16)
    sorted_ref[...] = _dot(perm, h_ref[...]).astype(BF16)

    def copy(e, g):
        src = pl.multiple_of(off_ref[i * N_EXPERTS + e] + g * ROW_GRANULE, ROW_GRANULE)
        dst = pl.multiple_of(row_ref[i * N_EXPERTS + e] + g * ROW_GRANULE, ROW_GRANULE)
        return pltpu.make_async_copy(sorted_ref.at[pl.ds(src, ROW_GRANULE)], xs_out.at[pl.ds(dst, ROW_GRANULE)], sem)

    for e in range(N_EXPERTS):
        lax.fori_loop(0, gran_ref[i * N_EXPERTS + e], lambda g, c, e=e: (copy(e, g).start(), c)[1], 0)
    for e in range(N_EXPERTS):
        lax.fori_loop(0, gran_ref[i * N_EXPERTS + e], lambda g, c, e=e: (copy(e, g).wait(), c)[1], 0)


def _dispatch_call(tables, h, route, n_rows, tm):
    t, d = h.shape
    buf_rows = 2 * tm + N_EXPERTS * ROW_GRANULE
    hbm = pl.BlockSpec(memory_space=pl.ANY)
    sub = 8
    xs, dest = pl.pallas_call(
        functools.partial(_dispatch_kernel, tm=tm),
        out_shape=[jax.ShapeDtypeStruct((n_rows, d), BF16), jax.ShapeDtypeStruct((t // tm, sub, tm), jnp.int32)],
        grid_spec=pltpu.PrefetchScalarGridSpec(
            num_scalar_prefetch=3,
            grid=(t // tm,),
            in_specs=[pl.BlockSpec((tm, d), lambda i, *_: (i, 0)), pl.BlockSpec((tm, LANES), lambda i, *_: (i, 0)),
                      hbm],
            out_specs=[hbm, pl.BlockSpec((1, sub, tm), lambda i, *_: (i, 0, 0))],
            scratch_shapes=[pltpu.VMEM((buf_rows, d), BF16), pltpu.SemaphoreType.DMA(())]),
        input_output_aliases={5: 0},
        compiler_params=_cparams(("arbitrary",)),
        name="moe_dispatch",
    )(*tables, h, route, jnp.zeros((n_rows, d), BF16))
    return xs, dest[:, 0, :].reshape(t), dest[:, 1, :].reshape(t)


def _moe_layer(mixer, sc, sh, g2, wr_p, w1, w3, w2, layer, lg, lb, tiles_per_batch, tm, tme, tf):
    x2d, h, route, counts = _router_call(mixer, sc, sh, wr_p, tiles_per_batch, tm)
    tables, tile_expert, n_used = _dispatch_plan(route, counts, tm, tme)
    xs, dest1, dest2 = _dispatch_call(tables, h, route, _sorted_rows(x2d.shape[0], tm, tme), tm)
    ys = _expert_call(tile_expert, n_used, xs, w1, w3, w2, layer, tme, tf)
    ya = ys.at[dest1].get(mode="promise_in_bounds")
    yb = ys.at[dest2].get(mode="promise_in_bounds")
    return _combine_call(x2d, ya, yb, route, g2, lg, lb, tiles_per_batch, tm)


def _trig_kernel(a_ref, c_ref, s_ref):
    a = a_ref[...]
    c_ref[...] = jnp.cos(a)
    s_ref[...] = jnp.sin(a)


def _trig_call(ang2d):
    n = ang2d.shape[0]
    tr = math.gcd(n, 1024)
    spec = pl.BlockSpec((tr, LANES), lambda i: (i, 0))
    return pl.pallas_call(
        _trig_kernel,
        out_shape=[jax.ShapeDtypeStruct(ang2d.shape, F32)] * 2,
        grid=(n // tr,),
        in_specs=[spec],
        out_specs=[spec, spec],
        compiler_params=_cparams(("parallel",)),
        name="rope_trig",
    )(ang2d)


def _rope_tables(positions):
    pos = positions.astype(F32)[..., None]
    lane = jnp.arange(LANES)
    inv64 = ROPE_THETA ** (-jnp.arange(0, RET_D, 2, dtype=F32) / RET_D)
    invm = ROPE_THETA ** (-jnp.arange(0, MLA_ROPE, 2, dtype=F32) / MLA_ROPE)
    n64 = inv64.shape[0]
    ang = pos * jnp.concatenate([inv64, invm])
    cos, sin = _trig_call(ang.reshape(-1, LANES))
    cos, sin = cos.reshape(ang.shape), sin.reshape(ang.shape)
    cos64, sin64, cosm, sinm = cos[..., :n64], sin[..., :n64], cos[..., n64:], sin[..., n64:]
    sign64 = jnp.where((lane % 64) < 32, -1.0, 1.0)
    c64 = jnp.tile(cos64, (1, 1, LANES // 32))
    s64 = jnp.tile(sin64, (1, 1, LANES // 32)) * sign64
    in_rope = (lane >= KR_LANE) & (lane < KR_LANE + MLA_ROPE)
    signm = jnp.where(lane < KR_LANE + MLA_ROPE // 2, -1.0, 1.0)
    cm = jnp.where(in_rope, jnp.tile(cosm, (1, 1, LANES // 16)), 1.0)
    sm = jnp.where(in_rope, jnp.tile(sinm, (1, 1, LANES // 16)) * signm, 0.0)
    return c64, s64, cm, sm


def _prep_mixer_weights(w_in, w_uq, w_ukv):
    d = w_in.shape[0]
    kr_cols = jnp.zeros((d, LANES), w_in.dtype).at[:, KR_LANE:KR_LANE + MLA_ROPE].set(
        w_in[:, MLA_OFF + MLA_Q_RANK + MLA_KV_RANK:])
    w_in_p = jnp.concatenate([w_in[:, :MLA_OFF + MLA_Q_RANK + MLA_KV_RANK], kr_cols], axis=1).astype(BF16)
    uq = w_uq.reshape(MLA_Q_RANK, MLA_HEADS, MLA_NOPE + MLA_ROPE)
    wuq_p = jnp.pad(uq, ((0, 0), (0, 0), (0, LANES - MLA_NOPE - MLA_ROPE))).reshape(MLA_Q_RANK, MLA_PAD).astype(BF16)
    ukv = w_ukv.reshape(MLA_KV_RANK, MLA_HEADS, MLA_NOPE + MLA_V)
    wkn_p = jnp.pad(ukv[:, :, :MLA_NOPE], ((0, 0), (0, 0), (0, LANES - MLA_NOPE))).reshape(MLA_KV_RANK, MLA_PAD).astype(BF16)
    wv = ukv[:, :, MLA_NOPE:].reshape(MLA_KV_RANK, MLA_VW).astype(BF16)
    return w_in_p, wuq_p, wkn_p, wv


def kernel(x, c, positions, w_in, ret_gn_g, mla_qn_g, mla_kvn_g, w_uq, w_ukv, w_out, w_ada, b_ada, ln1_g, ln1_b, ln2_g, ln2_b, w1_dense, w3_dense, w2_dense, w_router, w1_moe, w3_moe, w2_moe):
    b, s, d = x.shape
    tabs = _rope_tables(positions)
    ret_tabs = _ret_tables()
    mod = _ada_call(c, w_ada, b_ada)
    tm_tok = 512
    tiles_per_batch = s // tm_tok
    w_out_b = _cast_call(w_out)
    for l in range(w_in.shape[0]):
        sh1, sc1, g1, sh2, sc2, g2 = [mod[l, :, j * d:(j + 1) * d].reshape(b, 1, d) for j in range(6)]
        w_in_p, wuq_p, wkn_p, wv = _prep_mixer_weights(w_in[l], w_uq[l], w_ukv[l])
        (rq, rk, rv, rg), dil_views, (mq, mk, mv) = _inproj_call(
            x, sc1, sh1, w_in_p, tabs, mla_qn_g[l].reshape(1, -1), mla_kvn_g[l].reshape(1, -1), wuq_p, wkn_p, wv)
        ya = _ret_call(rq, rk, rv, rg, ret_gn_g[l].reshape(1, -1), ret_tabs)
        yb = _dil_call(dil_views)
        yc = _mla_call(mq, mk, mv)
        mixer = _mixer_operands(ya, yb, yc, w_out_b, l, x.reshape(b * s, d), g1, ln1_g[l].reshape(1, d),
                                ln1_b[l].reshape(1, d), tiles_per_batch, tm_tok)
        lg, lb = ln2_g[l].reshape(1, d), ln2_b[l].reshape(1, d)
        j = l // 2
        if l % 2 == 0:
            x2d = _ffn_call(mixer, sc2, sh2, g2, w1_dense, w3_dense, w2_dense, j, lg, lb, tiles_per_batch, tm_tok,
                            FF_CHUNK)
        else:
            wr_p = jnp.pad(w_router[j], ((0, 0), (0, LANES - N_EXPERTS)))
            x2d = _moe_layer(mixer, sc2, sh2, g2, wr_p, w1_moe, w3_moe, w2_moe, j, lg, lb, tiles_per_batch, tm_tok,
                             EXPERT_TILE, FF_CHUNK)
        x = x2d.reshape(b, s, d)
    return x
```

```python
import functools
import math

import jax
import jax.numpy as jnp
from jax import lax
from jax.experimental import pallas as pl
from jax.experimental.pallas import tpu as pltpu

D_MODEL = 1024
DEPTH = 4
RET_HEADS = 4
RET_D = 64
RET_CHUNK = 128
DIL_HEADS = 6
DIL_D = 64
DIL_DILATIONS = (1, 4, 16)
DIL_W = 128
MLA_HEADS = 6
MLA_Q_RANK = 384
MLA_KV_RANK = 256
MLA_NOPE = 64
MLA_ROPE = 32
MLA_V = 64
N_EXPERTS = 8
ROPE_THETA = 10000.0
LN_EPS = 1e-5
RMS_EPS = 1e-6
ALPHA = (2.0 * DEPTH) ** 0.25

LANES = 128
RET_W = RET_HEADS * RET_D
DIL_WD = DIL_HEADS * DIL_D
MLA_PAD = MLA_HEADS * LANES
MLA_VW = MLA_HEADS * MLA_V
RET_IN = 4 * RET_W
DIL_IN = 3 * DIL_WD
MLA_OFF = RET_IN + DIL_IN
D_IN_PAD = MLA_OFF + MLA_Q_RANK + MLA_KV_RANK + LANES
KR_LANE = MLA_NOPE

VMEM_LIMIT = 48 * 1024 * 1024
FFN_VMEM_LIMIT = 56 * 1024 * 1024
FF_CHUNK = 512
EXPERT_TILE = 512
CAST_BLOCK_BYTES = 4 * 1024 * 1024
ROW_GRANULE = 16
BF16 = jnp.bfloat16
F32 = jnp.float32
NEG_INF = float("-inf")
LOG2E = math.log2(math.e)
MLA_Q_SCALE = (MLA_NOPE + MLA_ROPE) ** -0.5 * LOG2E


def _cparams(sem, vmem_limit=VMEM_LIMIT):
    return pltpu.CompilerParams(dimension_semantics=sem, vmem_limit_bytes=vmem_limit)


def _dot(a, b):
    return jnp.dot(a, b, preferred_element_type=F32)


def _dot_nt(a, b):
    return lax.dot_general(a, b, (((1,), (1,)), ((), ())), preferred_element_type=F32)


def _dot_tn(a, b):
    return lax.dot_general(a, b, (((0,), (0,)), ((), ())), preferred_element_type=F32)


def _silu(x):
    return x * (1.0 / (1.0 + jnp.exp(-x)))


def _layer_norm(v, g, b):
    mu = jnp.mean(v, axis=-1, keepdims=True)
    d = v - mu
    var = jnp.mean(d * d, axis=-1, keepdims=True)
    return d * lax.rsqrt(var + LN_EPS) * g + b


def _lane_ids(shape):
    return lax.broadcasted_iota(jnp.int32, shape, len(shape) - 1)


def _rope_group(x, cos, sin_signed, first_half, half):
    fwd = pltpu.roll(x, LANES - half, 1)
    bwd = pltpu.roll(x, half, 1)
    return x * cos + jnp.where(first_half, fwd, bwd) * sin_signed


def _cast_kernel(w_ref, o_ref):
    o_ref[...] = w_ref[...].astype(BF16)


def _cast_call(w):
    cols = w.shape[-1]
    w2d = w.reshape(-1, cols)
    rows = w2d.shape[0]
    pack = 16
    tr = max(pack, CAST_BLOCK_BYTES // (4 * cols) // pack * pack)
    while rows % tr:
        tr -= pack
    spec = pl.BlockSpec((tr, cols), lambda i: (i, 0))
    out = pl.pallas_call(
        _cast_kernel,
        out_shape=jax.ShapeDtypeStruct(w2d.shape, BF16),
        grid=(rows // tr,),
        in_specs=[spec],
        out_specs=spec,
        compiler_params=_cparams(("parallel",)),
        name="cast_bf16",
    )(w2d)
    return out.reshape(w.shape)


def _ada_kernel(c_ref, w_ref, b_ref, o_ref):
    cond = _silu(c_ref[...])
    o_ref[0] = jnp.dot(cond, w_ref[0], preferred_element_type=F32,
                       precision=lax.Precision.HIGHEST) + b_ref[0]


def _ada_call(c, w_ada, b_ada):
    nl, d, n = w_ada.shape
    b = c.shape[0]
    tn = 1536
    return pl.pallas_call(
        _ada_kernel,
        out_shape=jax.ShapeDtypeStruct((nl, b, n), F32),
        grid=(nl, n // tn),
        in_specs=[pl.BlockSpec((b, d), lambda l, j: (0, 0)),
                  pl.BlockSpec((1, d, tn), lambda l, j: (l, 0, j)),
                  pl.BlockSpec((1, 1, tn), lambda l, j: (l, 0, j))],
        out_specs=pl.BlockSpec((1, b, tn), lambda l, j: (l, 0, j)),
        compiler_params=_cparams(("arbitrary", "arbitrary")),
        name="ada_mod",
    )(c, w_ada, b_ada.reshape(nl, 1, n))


def _inproj_kernel(x_ref, sc_ref, sh_ref, w_ref, c64_ref, s64_ref, cm_ref, sm_ref,
                   qg_ref, kvg_ref, wuq_ref, wkn_ref, wv_ref,
                   rq_ref, rk_ref, rv_ref, rg_ref,
                   dq1_ref, dk1_ref, dv1_ref, dq4_ref, dk4_ref, dv4_ref, dq16_ref, dk16_ref, dv16_ref,
                   mq_ref, mk_ref, mv_ref, scr_ref, *, tm):
    h = (x_ref[0] * (1.0 + sc_ref[0]) + sh_ref[0]).astype(BF16)
    c64 = c64_ref[0]
    s64 = s64_ref[0]
    cm = cm_ref[0]
    sm = sm_ref[0]
    lane = _lane_ids((1, LANES))
    first64 = (lane % 64) < 32
    firstm = lane < (KR_LANE + MLA_ROPE // 2)

    def rope64(zc):
        return _rope_group(zc, c64, s64, first64, 32)

    def ropem(zc):
        return _rope_group(zc, cm, sm, firstm, MLA_ROPE // 2)

    zm = _dot(h, w_ref[:, MLA_OFF:D_IN_PAD])
    cq = zm[:, 0:MLA_Q_RANK]
    ckv = zm[:, MLA_Q_RANK:MLA_Q_RANK + MLA_KV_RANK]
    kr = ropem(zm[:, MLA_Q_RANK + MLA_KV_RANK:])
    cqn = (cq * lax.rsqrt(jnp.mean(cq * cq, axis=-1, keepdims=True) + RMS_EPS) * qg_ref[...]).astype(BF16)
    ckvn = (ckv * lax.rsqrt(jnp.mean(ckv * ckv, axis=-1, keepdims=True) + RMS_EPS) * kvg_ref[...]).astype(BF16)
    q = _dot(cqn, wuq_ref[...])
    kn = _dot(ckvn, wkn_ref[...])
    for hh in range(MLA_HEADS):
        sl = slice(hh * LANES, (hh + 1) * LANES)
        mq_ref[0, :, sl] = (ropem(q[:, hh * LANES:(hh + 1) * LANES]) * MLA_Q_SCALE).astype(BF16)
        mk_ref[0, :, sl] = (kn[:, hh * LANES:(hh + 1) * LANES] + kr).astype(BF16)
    mv_ref[0] = _dot(ckvn, wv_ref[...]).astype(BF16)

    za = _dot(h, w_ref[:, 0:RET_IN])
    for j in range(RET_W // LANES):
        sl = slice(j * LANES, (j + 1) * LANES)
        rq_ref[0, :, sl] = rope64(za[:, j * LANES:(j + 1) * LANES]).astype(BF16)
        kc = za[:, RET_W + j * LANES:RET_W + (j + 1) * LANES]
        rk_ref[0, :, sl] = (rope64(kc) * (RET_D ** -0.5)).astype(BF16)
    rv_ref[0] = za[:, 2 * RET_W:3 * RET_W].astype(BF16)
    rg_ref[0] = za[:, 3 * RET_W:4 * RET_W]

    zb = _dot(h, w_ref[:, RET_IN:MLA_OFF])
    q_scale = (DIL_D ** -0.5) * LOG2E
    views = ((dq1_ref, dq4_ref, dq16_ref), (dk1_ref, dk4_ref, dk16_ref), (dv1_ref, dv4_ref, dv16_ref))
    for a, (n1, n4, n16) in enumerate(views):
        for j in range(DIL_WD // LANES):
            zc = zb[:, a * DIL_WD + j * LANES:a * DIL_WD + (j + 1) * LANES]
            if a == 0:
                zc = rope64(zc) * q_scale
            elif a == 1:
                zc = rope64(zc)
            scr_ref[j] = zc
            n1[0, :, j * LANES:(j + 1) * LANES] = zc.astype(BF16)
        for r, ref in ((4, n4), (16, n16)):
            for rho in range(r):
                for j in range(DIL_WD // LANES):
                    ref[0, :, rho * DIL_WD + j * LANES:rho * DIL_WD + (j + 1) * LANES] = (
                        scr_ref[j, pl.ds(rho, tm // r, stride=r), :].astype(BF16))

def _inproj_call(x, sc, sh, w_in_p, tabs, qg, kvg, wuq_p, wkn_p, wv, tm=512):
    b, s, d = x.shape
    c64, s64, cm, sm = tabs
    tok = lambda w: pl.BlockSpec((1, tm, w), lambda i, j: (i, j, 0))
    per_b = pl.BlockSpec((1, 1, d), lambda i, j: (i, 0, 0))
    full = lambda a: pl.BlockSpec(a.shape, lambda i, j: (0,) * a.ndim)
    outs = [(1, RET_W, BF16), (1, RET_W, BF16), (1, RET_W, BF16), (1, RET_W, F32)]
    outs += [(r, DIL_WD, BF16) for r in DIL_DILATIONS for _ in range(3)]
    outs += [(1, MLA_PAD, BF16), (1, MLA_PAD, BF16), (1, MLA_VW, BF16)]
    res = pl.pallas_call(
        functools.partial(_inproj_kernel, tm=tm),
        out_shape=[jax.ShapeDtypeStruct((b, s // r, r * w), dt) for r, w, dt in outs],
        grid=(b, s // tm),
        in_specs=[tok(d), per_b, per_b, full(w_in_p), tok(LANES), tok(LANES), tok(LANES), tok(LANES),
                  full(qg), full(kvg), full(wuq_p), full(wkn_p), full(wv)],
        out_specs=[pl.BlockSpec((1, tm // r, r * w), lambda i, j: (i, j, 0)) for r, w, _ in outs],
        scratch_shapes=[pltpu.VMEM((DIL_WD // LANES, tm, LANES), F32)],
        compiler_params=_cparams(("parallel", "parallel")),
        name="in_proj",
    )(x, sc, sh, w_in_p, c64, s64, cm, sm, qg, kvg, wuq_p, wkn_p, wv)
    ret = res[0:4]
    dil = [res[4 + 3 * i:7 + 3 * i] for i in range(len(DIL_DILATIONS))]
    mla = res[4 + 3 * len(DIL_DILATIONS):]
    return ret, dil, mla


def _ret_kernel(q_ref, k_ref, v_ref, g_ref, gn_ref, intra_ref, xi_ref, zeta_ref, dmat_ref, bmask_ref,
                o_ref, state_ref, *, chunks):
    @pl.when(pl.program_id(1) == 0)
    def _():
        state_ref[...] = jnp.zeros_like(state_ref)

    lane = _lane_ids((1, LANES))
    head_a = lane < RET_D
    c = RET_CHUNK
    npair = RET_W // LANES
    items = [(ci, p) for ci in range(chunks) for p in range(npair)]
    view = lambda ref, ci, p: ref[0, ci * c:(ci + 1) * c, p * LANES:(p + 1) * LANES]

    inner, kv_inc = {}, {}
    for ci, p in items:
        cols = slice(p * LANES, (p + 1) * LANES)
        q, k, v = view(q_ref, ci, p), view(k_ref, ci, p), view(v_ref, ci, p)
        zero = jnp.zeros_like(q)
        s_a = _dot_nt(jnp.where(head_a, q, zero), k) * intra_ref[2 * p]
        s_b = _dot_nt(jnp.where(head_a, zero, q), k) * intra_ref[2 * p + 1]
        s_cat = jnp.concatenate([s_a, s_b], axis=1).astype(BF16)
        v_stack = jnp.concatenate([jnp.where(head_a, v, zero), jnp.where(head_a, zero, v)], axis=0)
        inner[(ci, p)] = _dot(s_cat, v_stack)
        kz = (k.astype(F32) * zeta_ref[:, cols]).astype(BF16)
        kv_inc[(ci, p)] = _dot_tn(kz, v) * bmask_ref[...]

    outs = {}
    for p in range(npair):
        cols = slice(p * LANES, (p + 1) * LANES)
        state = state_ref[p]
        for ci in range(chunks):
            cross = _dot(view(q_ref, ci, p), state.astype(BF16)) * xi_ref[:, cols]
            outs[(ci, p)] = inner[(ci, p)] + cross
            state = state * dmat_ref[p] + kv_inc[(ci, p)]
        state_ref[p] = state

    inv_n = 1.0 / RET_D
    for ci, p in items:
        cols = slice(p * LANES, (p + 1) * LANES)
        o = outs[(ci, p)]
        mu = jnp.where(head_a,
                       jnp.sum(jnp.where(head_a, o, 0.0), axis=-1, keepdims=True),
                       jnp.sum(jnp.where(head_a, 0.0, o), axis=-1, keepdims=True)) * inv_n
        d = o - mu
        dd = d * d
        var = jnp.where(head_a,
                        jnp.sum(jnp.where(head_a, dd, 0.0), axis=-1, keepdims=True),
                        jnp.sum(jnp.where(head_a, 0.0, dd), axis=-1, keepdims=True)) * inv_n
        on = d * lax.rsqrt(var + LN_EPS) * gn_ref[:, cols]
        o_ref[0, ci * c:(ci + 1) * c, cols] = (_silu(view(g_ref, ci, p)) * on).astype(BF16)


def _ret_tables():
    c = RET_CHUNK
    log_g = jnp.log(1.0 - 2.0 ** (-5.0 - jnp.arange(RET_HEADS, dtype=F32)))
    idx = jnp.arange(c, dtype=F32)
    diff = idx[:, None] - idx[None, :]
    intra = jnp.where(diff >= 0, jnp.exp(jnp.maximum(diff, 0.0) * log_g[:, None, None]), 0.0)
    xi = jnp.exp((idx[:, None] + 1.0) * log_g[None, :])
    zeta = jnp.exp((c - 1.0 - idx[:, None]) * log_g[None, :])
    decay = jnp.exp(c * log_g)
    xi_l = jnp.repeat(xi, RET_D, axis=1)
    zeta_l = jnp.repeat(zeta, RET_D, axis=1)
    r = jnp.arange(LANES)
    same = (r[:, None] // RET_D) == (r[None, :] // RET_D)
    bmask = same.astype(F32)
    dec_l = jnp.repeat(decay, RET_D).reshape(RET_W // LANES, 1, LANES)
    dmat = bmask[None] * dec_l
    return intra, xi_l, zeta_l, dmat, bmask


def _ret_call(rq, rk, rv, rg, gn, tables, chunks=8):
    b, s, w = rq.shape
    tm = chunks * RET_CHUNK
    intra, xi_l, zeta_l, dmat, bmask = tables
    tok = pl.BlockSpec((1, tm, w), lambda i, j: (i, j, 0))
    full = lambda a: pl.BlockSpec(a.shape, lambda i, j: (0,) * a.ndim)
    return pl.pallas_call(
        functools.partial(_ret_kernel, chunks=chunks),
        out_shape=jax.ShapeDtypeStruct((b, s, w), BF16),
        grid=(b, s // tm),
        in_specs=[tok, tok, tok, tok, full(gn), full(intra), full(xi_l), full(zeta_l), full(dmat), full(bmask)],
        out_specs=tok,
        scratch_shapes=[pltpu.VMEM((w // LANES, LANES, LANES), F32)],
        compiler_params=_cparams(("parallel", "arbitrary")),
        name="retention",
    )(rq, rk, rv, rg, gn, intra, xi_l, zeta_l, dmat, bmask)


def _dil_band(first_block):
    w = DIL_W
    qi = lax.broadcasted_iota(jnp.int32, (w, 2 * w), 0)
    kj = lax.broadcasted_iota(jnp.int32, (w, 2 * w), 1)
    dist = w + qi - kj
    band = (dist >= 0) & (dist <= w)
    if first_block is False:
        return band
    return band & (jnp.logical_not(first_block) | (kj >= w))


DIL_GROUP = 2


def _dil_attend(items, valid_of):
    head_a = _lane_ids((1, LANES)) < DIL_D
    scores = []
    for idx, (q, kcat, _) in enumerate(items):
        zero = jnp.zeros_like(q)
        valid = valid_of(idx)
        scores.append([jnp.where(valid, _dot_nt(qh, kcat), NEG_INF)
                       for qh in (jnp.where(head_a, q, zero), jnp.where(head_a, zero, q))])
    soft = []
    for scs in scores:
        ms, ls, ps = [], [], []
        for sc in scs:
            m = jnp.max(sc, axis=-1, keepdims=True)
            pe = jnp.exp2(sc - m)
            ms.append(m)
            ls.append(jnp.sum(pe, axis=-1, keepdims=True))
            ps.append(pe.astype(BF16))
        soft.append((ms, ls, jnp.concatenate(ps, axis=1)))
    out = []
    for (_, _, vcat), (ms, ls, pcat) in zip(items, soft):
        zv = jnp.zeros_like(vcat)
        v_stack = jnp.concatenate([jnp.where(head_a, vcat, zv), jnp.where(head_a, zv, vcat)], axis=0)
        out.append((_dot(pcat, v_stack), ms, ls))
    return out


def _dil_normalise(acc, ms, ls):
    head_a = _lane_ids((1, LANES)) < DIL_D
    o = acc * jnp.where(head_a, 1.0 / ls[0], 1.0 / ls[1])
    lse = jnp.where(head_a, ms[0] + jnp.log2(ls[0]), ms[1] + jnp.log2(ls[1]))
    return o, lse


def _dil_partial_kernel(q_ref, kp_ref, kc_ref, vp_ref, vc_ref, o_ref, lse_ref, *, r):
    valid = _dil_band(pl.program_id(1) == 0)
    npair = DIL_WD // LANES
    for g0 in range(0, r, DIL_GROUP):
        rhos = range(g0, min(g0 + DIL_GROUP, r))
        items = []
        for rho in rhos:
            for p in range(npair):
                cols = slice(rho * DIL_WD + p * LANES, rho * DIL_WD + (p + 1) * LANES)
                items.append((q_ref[0, :, cols],
                              jnp.concatenate([kp_ref[0, :, cols], kc_ref[0, :, cols]], axis=0),
                              jnp.concatenate([vp_ref[0, :, cols], vc_ref[0, :, cols]], axis=0)))
        res = _dil_attend(items, lambda idx: valid)
        for gi, rho in enumerate(rhos):
            rows = pl.ds(rho, DIL_W, stride=r)
            for p in range(npair):
                o, lse = _dil_normalise(*res[gi * npair + p])
                o_ref[0, p, rows, :] = o
                lse_ref[0, p, rows, :] = lse


def _dil_final_kernel(q_ref, k_ref, kh_ref, v_ref, vh_ref, o4_ref, lse4_ref, o16_ref, lse16_ref, o_ref, *, nblk):
    w = DIL_W
    npair = DIL_WD // LANES
    band_first = _dil_band(pl.program_id(1) == 0)
    band = _dil_band(False)

    for g0 in range(0, nblk, DIL_GROUP):
        blocks = range(g0, min(g0 + DIL_GROUP, nblk))
        items = []
        for i in blocks:
            rows = slice(i * w, (i + 1) * w)
            for p in range(npair):
                cols = slice(p * LANES, (p + 1) * LANES)
                if i == 0:
                    kcat = jnp.concatenate([kh_ref[0, :, cols], k_ref[0, rows, cols]], axis=0)
                    vcat = jnp.concatenate([vh_ref[0, :, cols], v_ref[0, rows, cols]], axis=0)
                else:
                    kcat = k_ref[0, (i - 1) * w:(i + 1) * w, cols]
                    vcat = v_ref[0, (i - 1) * w:(i + 1) * w, cols]
                items.append((q_ref[0, rows, cols], kcat, vcat))
        res = _dil_attend(items, lambda idx: band_first if blocks[idx // npair] == 0 else band)
        for gi, i in enumerate(blocks):
            rows = slice(i * w, (i + 1) * w)
            for p in range(npair):
                o1, lse1 = _dil_normalise(*res[gi * npair + p])
                lse4 = lse4_ref[0, p, rows, :]
                lse16 = lse16_ref[0, p, rows, :]
                top = jnp.maximum(jnp.maximum(lse1, lse4), lse16)
                w1, w4, w16 = jnp.exp2(lse1 - top), jnp.exp2(lse4 - top), jnp.exp2(lse16 - top)
                num = w1 * o1 + w4 * o4_ref[0, p, rows, :] + w16 * o16_ref[0, p, rows, :]
                o_ref[0, rows, p * LANES:(p + 1) * LANES] = (num / (w1 + w4 + w16)).astype(BF16)


def _dil_call(views, nblk=4):
    (q1, k1, v1) = views[0]
    b, s, w = q1.shape
    npair = w // LANES
    partial = []
    for r, (q, k, v) in zip(DIL_DILATIONS[1:], views[1:]):
        cur = pl.BlockSpec((1, DIL_W, r * w), lambda bb, n: (bb, n, 0))
        prev = pl.BlockSpec((1, DIL_W, r * w), lambda bb, n: (bb, jnp.maximum(n - 1, 0), 0))
        partial += pl.pallas_call(
            functools.partial(_dil_partial_kernel, r=r),
            out_shape=[jax.ShapeDtypeStruct((b, npair, s, LANES), F32)] * 2,
            grid=(b, s // (r * DIL_W)),
            in_specs=[cur, prev, cur, prev, cur],
            out_specs=[pl.BlockSpec((1, npair, r * DIL_W, LANES), lambda bb, n: (bb, 0, n, 0))] * 2,
            compiler_params=_cparams(("parallel", "arbitrary")),
            name=f"dilated_r{r}",
        )(q, k, k, v, v)
    tm = nblk * DIL_W
    tok = lambda width: pl.BlockSpec((1, tm, width), lambda bb, n: (bb, n, 0))
    halo = pl.BlockSpec((1, DIL_W, w), lambda bb, n: (bb, jnp.maximum(n * nblk - 1, 0), 0))
    accs = pl.BlockSpec((1, npair, tm, LANES), lambda bb, n: (bb, 0, n, 0))
    return pl.pallas_call(
        functools.partial(_dil_final_kernel, nblk=nblk),
        out_shape=jax.ShapeDtypeStruct((b, s, w), BF16),
        grid=(b, s // tm),
        in_specs=[tok(w), tok(w), halo, tok(w), halo, accs, accs, accs, accs],
        out_specs=tok(w),
        compiler_params=_cparams(("parallel", "arbitrary")),
        name="dilated_r1_merge",
    )(q1, k1, k1, v1, v1, *partial)


def _mla_kernel(q_ref, k_ref, v_ref, o_ref, m_ref, l_ref, acc_ref, *, t, sub):
    qi = pl.program_id(2)
    ki = pl.program_id(3)
    head_a = _lane_ids((1, LANES)) < MLA_V
    nsub = t // sub

    @pl.when(ki == 0)
    def _():
        m_ref[...] = jnp.full_like(m_ref, NEG_INF)
        l_ref[...] = jnp.zeros_like(l_ref)
        acc_ref[...] = jnp.zeros_like(acc_ref)

    def v_stack(c):
        v = v_ref[0, c * sub:(c + 1) * sub]
        zv = jnp.zeros_like(v)
        return jnp.concatenate([jnp.where(head_a, v, zv), jnp.where(head_a, zv, v)], axis=0)

    def column(c, rs, diag_r):
        vs = v_stack(c)
        scores = {}
        for r in rs:
            rows = slice(r * sub, (r + 1) * sub)
            for j in range(2):
                q = q_ref[0, rows, j * LANES:(j + 1) * LANES]
                k = k_ref[0, c * sub:(c + 1) * sub, j * LANES:(j + 1) * LANES]
                s = _dot_nt(q, k)
                if r == diag_r:
                    row = lax.broadcasted_iota(jnp.int32, (sub, sub), 0)
                    colk = lax.broadcasted_iota(jnp.int32, (sub, sub), 1)
                    s = jnp.where(colk <= row, s, NEG_INF)
                scores[(r, j)] = s
        probs = {}
        for r in rs:
            rows = slice(r * sub, (r + 1) * sub)
            ps, alphas = [], []
            for j in range(2):
                s = scores[(r, j)]
                m_old = m_ref[j, rows]
                m_new = jnp.maximum(m_old, jnp.max(s, axis=-1, keepdims=True))
                alpha = jnp.exp2(m_old - m_new)
                pe = jnp.exp2(s - jnp.concatenate([m_new] * (sub // LANES), axis=1))
                l_ref[j, rows] = alpha * l_ref[j, rows] + jnp.sum(pe, axis=-1, keepdims=True)
                m_ref[j, rows] = m_new
                ps.append(pe.astype(BF16))
                alphas.append(alpha)
            probs[r] = (jnp.concatenate(ps, axis=1), jnp.where(head_a, alphas[0], alphas[1]))
        for r in rs:
            rows = slice(r * sub, (r + 1) * sub)
            pcat, alpha = probs[r]
            acc_ref[rows] = acc_ref[rows] * alpha + _dot(pcat, vs)

    @pl.when(ki < qi)
    def _():
        for c in range(nsub):
            column(c, range(nsub), None)

    @pl.when(ki == qi)
    def _():
        for c in range(nsub):
            column(c, range(c, nsub), c)
        o_ref[0] = (acc_ref[...] / jnp.where(head_a, l_ref[0], l_ref[1])).astype(BF16)


def _mla_call(mq, mk, mv, t=2048, sub=512):
    b, s, _ = mq.shape
    n = s // t
    return pl.pallas_call(
        functools.partial(_mla_kernel, t=t, sub=sub),
        out_shape=jax.ShapeDtypeStruct((b, s, MLA_VW), BF16),
        grid=(b, MLA_HEADS // 2, n, n),
        in_specs=[pl.BlockSpec((1, t, 2 * LANES), lambda bb, p, i, j: (bb, i, p)),
                  pl.BlockSpec((1, t, 2 * LANES), lambda bb, p, i, j: (bb, jnp.minimum(j, i), p)),
                  pl.BlockSpec((1, t, LANES), lambda bb, p, i, j: (bb, jnp.minimum(j, i), p))],
        out_specs=pl.BlockSpec((1, t, LANES), lambda bb, p, i, j: (bb, i, p)),
        scratch_shapes=[pltpu.VMEM((2, t, LANES), F32), pltpu.VMEM((2, t, LANES), F32),
                        pltpu.VMEM((t, LANES), F32)],
        compiler_params=_cparams(("parallel", "parallel", "parallel", "arbitrary")),
        name="mla_attention",
    )(mq, mk, mv)


N_MIX = 8
ROUTER_ROWS = 128


def _mixer_residual(ya_ref, yb_ref, yc_ref, w_ref, x_ref, g_ref, lg_ref, lb_ref, row_groups=None):
    a, b = RET_W, RET_W + DIL_WD
    groups = [slice(None)] if row_groups is None else row_groups
    ys = [_dot(ya_ref[r, :], w_ref[0, 0:a, :]) + _dot(yb_ref[r, :], w_ref[0, a:b, :])
          + _dot(yc_ref[r, :], w_ref[0, b:, :]) for r in groups]
    out = [_layer_norm(ALPHA * x_ref[r, :] + (1.0 + g_ref[0]) * y, lg_ref[...], lb_ref[...])
           for r, y in zip(groups, ys)]
    return out[0] if row_groups is None else out


def _mixer_operands(ya, yb, yc, w_out, layer, x2d, g1, lg, lb, tiles_per_batch, tm):
    t, d = x2d.shape
    tok = lambda a: pl.BlockSpec((tm, a.shape[-1]), lambda i, *_: (i, 0))
    flat = lambda a: a.reshape(t, a.shape[-1])
    arrays = [flat(ya), flat(yb), flat(yc), w_out, x2d, g1, lg, lb]
    specs = [tok(ya), tok(yb), tok(yc),
             pl.BlockSpec((1,) + w_out.shape[1:], lambda i, *_: (layer, 0, 0)),
             tok(x2d), pl.BlockSpec((1, 1, d), lambda i, *_: (i // tiles_per_batch, 0, 0)),
             pl.BlockSpec((1, d), lambda i, *_: (0, 0)), pl.BlockSpec((1, d), lambda i, *_: (0, 0))]
    return arrays, specs


def _swiglu(h, w1_ref, w3_ref, w2_ref, chunk):
    ff = w1_ref.shape[1]
    y = None
    for a in range(0, ff, chunk):
        b = min(a + chunk, ff)
        mid = (_silu(_dot(h, w1_ref[:, a:b])) * _dot(h, w3_ref[:, a:b])).astype(BF16)
        part = _dot(mid, w2_ref[a:b, :])
        y = part if y is None else y + part
    return y


def _load_weights_bf16(w1_hbm, w3_hbm, w2_hbm, w1_ref, w3_ref, w2_ref, stage_cols, stage_rows, sem, chunk):
    ff = w1_ref.shape[1]
    jobs = []
    for a in range(0, ff, chunk):
        b = min(a + chunk, ff)
        for src, dst in ((w1_hbm, w1_ref), (w3_hbm, w3_ref)):
            jobs.append((src.at[:, a:b], lambda s, n=b - a: stage_cols.at[s, :, 0:n], dst.at[:, a:b]))
        jobs.append((w2_hbm.at[a:b, :], lambda s, n=b - a: stage_rows.at[s, 0:n, :], w2_ref.at[a:b, :]))

    def copy(k):
        src, stage, _ = jobs[k]
        return pltpu.make_async_copy(src, stage(k % 2), sem.at[k % 2])

    copy(0).start()
    for k in range(len(jobs)):
        if k + 1 < len(jobs):
            copy(k + 1).start()
        copy(k).wait()
        _, stage, dst = jobs[k]
        dst[...] = stage(k % 2)[...].astype(BF16)


def _ffn_kernel(*refs, layer, chunk):
    mix = refs[:N_MIX]
    (sc_ref, sh_ref, g_ref, w1_hbm, w3_hbm, w2_hbm, lg_ref, lb_ref, o_ref,
     w1_ref, w3_ref, w2_ref, stage_cols, stage_rows, sem) = refs[N_MIX:]

    @pl.when(pl.program_id(0) == 0)
    def _():
        _load_weights_bf16(w1_hbm.at[layer], w3_hbm.at[layer], w2_hbm.at[layer], w1_ref, w3_ref, w2_ref,
                           stage_cols, stage_rows, sem, chunk)

    tm = o_ref.shape[0]
    groups = [slice(k, k + ROUTER_ROWS) for k in range(0, tm, ROUTER_ROWS)]
    xs = _mixer_residual(*mix, row_groups=groups)
    h = jnp.concatenate([(x * (1.0 + sc_ref[0]) + sh_ref[0]).astype(BF16) for x in xs], axis=0)
    y = _swiglu(h, w1_ref, w3_ref, w2_ref, chunk)
    for r, x in zip(groups, xs):
        o_ref[r, :] = _layer_norm(ALPHA * x + (1.0 + g_ref[0]) * y[r, :], lg_ref[...], lb_ref[...])


def _weight_scratch(d, ff, chunk):
    return [pltpu.VMEM((d, ff), BF16), pltpu.VMEM((d, ff), BF16), pltpu.VMEM((ff, d), BF16),
            pltpu.VMEM((2, d, chunk), F32), pltpu.VMEM((2, chunk, d), F32), pltpu.SemaphoreType.DMA((2,))]


def _ffn_call(mixer, sc, sh, g2, w1, w3, w2, layer, lg, lb, tiles_per_batch, tm, chunk):
    mix_arrays, mix_specs = mixer
    t, d = mix_arrays[4].shape
    ff = w1.shape[2]
    per_b = pl.BlockSpec((1, 1, d), lambda i: (i // tiles_per_batch, 0, 0))
    vec = pl.BlockSpec((1, d), lambda i: (0, 0))
    hbm = pl.BlockSpec(memory_space=pl.ANY)
    return pl.pallas_call(
        functools.partial(_ffn_kernel, layer=layer, chunk=chunk),
        out_shape=jax.ShapeDtypeStruct((t, d), F32),
        grid=(t // tm,),
        in_specs=mix_specs + [per_b, per_b, per_b, hbm, hbm, hbm, vec, vec],
        out_specs=pl.BlockSpec((tm, d), lambda i: (i, 0)),
        scratch_shapes=_weight_scratch(d, ff, chunk),
        compiler_params=_cparams(("arbitrary",), FFN_VMEM_LIMIT),
        name="ffn_dense",
    )(*mix_arrays, sc, sh, g2, w1, w3, w2, lg, lb)


ROUTE_G_OFF = 2
ROUTE_RANK_OFF = 4


def _router_kernel(*refs):
    sc_ref, sh_ref, wr_ref, tri_ref, x1_ref, h_ref, r_ref, cnt_ref = refs[N_MIX:]
    tm = x1_ref.shape[0]
    groups = [slice(k, k + ROUTER_ROWS) for k in range(0, tm, ROUTER_ROWS)]
    x1s = _mixer_residual(*refs[:N_MIX], row_groups=groups)
    hs = [x1 * (1.0 + sc_ref[0]) + sh_ref[0] for x1 in x1s]
    for r, x1, h in zip(groups, x1s, hs):
        x1_ref[r, :] = x1
        h_ref[r, :] = h.astype(BF16)
    logits = jnp.concatenate(
        [jnp.dot(h, wr_ref[...], preferred_element_type=F32, precision=lax.Precision.HIGHEST) for h in hs], axis=0)
    lane = _lane_ids(logits.shape)
    lg = jnp.where(lane < N_EXPERTS, logits, NEG_INF)
    m1 = jnp.max(lg, axis=-1, keepdims=True)
    i1 = jnp.min(jnp.where(lg == m1, lane, LANES), axis=-1, keepdims=True)
    lg2 = jnp.where(lane == i1, NEG_INF, lg)
    m2 = jnp.max(lg2, axis=-1, keepdims=True)
    i2 = jnp.min(jnp.where(lg2 == m2, lane, LANES), axis=-1, keepdims=True)
    e2 = jnp.exp(m2 - m1)
    den = 1.0 + e2
    chosen = (lane == i1) | (lane == i2)
    cum = _dot(tri_ref[...], chosen.astype(BF16))
    rank1 = jnp.sum(jnp.where(lane == i1, cum, 0.0), axis=-1, keepdims=True) - 1.0
    rank2 = jnp.sum(jnp.where(lane == i2, cum, 0.0), axis=-1, keepdims=True) - 1.0
    cnt_ref[0] = cum[cum.shape[0] - 1:, :]
    out = jnp.where(lane == 0, i1.astype(F32), 0.0)
    out = jnp.where(lane == 1, i2.astype(F32), out)
    out = jnp.where(lane == ROUTE_G_OFF, 1.0 / den, out)
    out = jnp.where(lane == ROUTE_G_OFF + 1, e2 / den, out)
    out = jnp.where(lane == ROUTE_RANK_OFF, rank1, out)
    out = jnp.where(lane == ROUTE_RANK_OFF + 1, rank2, out)
    r_ref[...] = out


def _router_call(mixer, sc, sh, wr_p, tiles_per_batch, tm):
    mix_arrays, mix_specs = mixer
    t, d = mix_arrays[4].shape
    per_b = pl.BlockSpec((1, 1, d), lambda i: (i // tiles_per_batch, 0, 0))
    tri = (jnp.arange(tm)[:, None] >= jnp.arange(tm)[None, :]).astype(BF16)
    wspec = pl.BlockSpec(wr_p.shape, lambda i: (0, 0))
    tok = pl.BlockSpec((tm, d), lambda i: (i, 0))
    return pl.pallas_call(
        _router_kernel,
        out_shape=[jax.ShapeDtypeStruct((t, d), F32), jax.ShapeDtypeStruct((t, d), BF16),
                   jax.ShapeDtypeStruct((t, LANES), F32), jax.ShapeDtypeStruct((t // tm, 1, LANES), F32)],
        grid=(t // tm,),
        in_specs=mix_specs + [per_b, per_b, wspec, pl.BlockSpec((tm, tm), lambda i: (0, 0))],
        out_specs=[tok, tok, pl.BlockSpec((tm, LANES), lambda i: (i, 0)),
                   pl.BlockSpec((1, 1, LANES), lambda i: (i, 0, 0))],
        compiler_params=_cparams(("parallel",)),
        name="moe_router",
    )(*mix_arrays, sc, sh, wr_p, tri)


def _expert_kernel(te_ref, nu_ref, x_ref, w1_hbm, w3_hbm, w2_hbm, o_ref,
                   w1_ref, w3_ref, w2_ref, stage_cols, stage_rows, sem, *, layer, chunk):
    i = pl.program_id(0)
    used = i < nu_ref[0]
    e = te_ref[i]
    new_expert = (i == 0) | (e != te_ref[jnp.maximum(i - 1, 0)])

    @pl.when(used & new_expert)
    def _():
        _load_weights_bf16(w1_hbm.at[layer, e], w3_hbm.at[layer, e], w2_hbm.at[layer, e], w1_ref, w3_ref, w2_ref,
                           stage_cols, stage_rows, sem, chunk)

    @pl.when(used)
    def _():
        o_ref[...] = _swiglu(x_ref[...], w1_ref, w3_ref, w2_ref, chunk).astype(o_ref.dtype)

    @pl.when(jnp.logical_not(used))
    def _():
        o_ref[...] = jnp.zeros_like(o_ref)


def _expert_call(tile_expert, n_used, xs, w1, w3, w2, layer, tm, chunk):
    p, d = xs.shape
    ff = w1.shape[3]
    hbm = pl.BlockSpec(memory_space=pl.ANY)
    return pl.pallas_call(
        functools.partial(_expert_kernel, layer=layer, chunk=chunk),
        out_shape=jax.ShapeDtypeStruct((p, d), BF16),
        grid_spec=pltpu.PrefetchScalarGridSpec(
            num_scalar_prefetch=2,
            grid=(p // tm,),
            in_specs=[pl.BlockSpec((tm, d), lambda i, te, nu: (i, 0)), hbm, hbm, hbm],
            out_specs=pl.BlockSpec((tm, d), lambda i, te, nu: (i, 0)),
            scratch_shapes=_weight_scratch(d, ff, chunk)),
        compiler_params=_cparams(("arbitrary",), FFN_VMEM_LIMIT),
        name="moe_experts",
    )(tile_expert, n_used, xs, w1, w3, w2)


def _combine_kernel(x_ref, ya_ref, yb_ref, r_ref, g_ref, lg_ref, lb_ref, o_ref):
    r = r_ref[...]
    lane = _lane_ids(r.shape)
    ga = jnp.sum(jnp.where(lane == ROUTE_G_OFF, r, 0.0), axis=-1, keepdims=True)
    gb = jnp.sum(jnp.where(lane == ROUTE_G_OFF + 1, r, 0.0), axis=-1, keepdims=True)
    y = ga * ya_ref[...].astype(F32) + gb * yb_ref[...].astype(F32)
    v = ALPHA * x_ref[...] + (1.0 + g_ref[0]) * y
    o_ref[...] = _layer_norm(v, lg_ref[...], lb_ref[...])


def _combine_call(x2d, ya, yb, route, g2, lg, lb, tiles_per_batch, tm):
    t, d = x2d.shape
    tok = pl.BlockSpec((tm, d), lambda i: (i, 0))
    per_b = pl.BlockSpec((1, 1, d), lambda i: (i // tiles_per_batch, 0, 0))
    vec = pl.BlockSpec((1, d), lambda i: (0, 0))
    return pl.pallas_call(
        _combine_kernel,
        out_shape=jax.ShapeDtypeStruct((t, d), F32),
        grid=(t // tm,),
        in_specs=[tok, tok, tok, pl.BlockSpec((tm, LANES), lambda i: (i, 0)), per_b, vec, vec],
        out_specs=tok,
        compiler_params=_cparams(("parallel",)),
        name="moe_combine_ln",
    )(x2d, ya, yb, route, g2, lg, lb)


def _dispatch_plan(route, counts, tm, tme):
    t = route.shape[0]
    nt = t // tm
    n = counts[:, 0, :N_EXPERTS].astype(jnp.int32)
    gran = (n + ROW_GRANULE - 1) // ROW_GRANULE
    seg = gran * ROW_GRANULE
    local_off = jnp.cumsum(seg, axis=1) - seg
    region = ((jnp.sum(seg, axis=0) + tme - 1) // tme) * tme
    ends = jnp.cumsum(region)
    seg_row = (ends - region)[None, :] + jnp.cumsum(seg, axis=0) - seg
    n_rows = _sorted_rows(t, tm, tme)
    tile_start = jnp.arange(n_rows // tme, dtype=jnp.int32) * tme
    tile_expert = jnp.minimum(jnp.sum((tile_start[:, None] >= ends[None, :]).astype(jnp.int32), axis=1),
                              N_EXPERTS - 1)
    n_used = (ends[-1] // tme).astype(jnp.int32).reshape(1)
    tables = (seg_row.reshape(-1), local_off.reshape(-1), gran.reshape(-1))
    return tables, tile_expert, n_used


def _sorted_rows(t, tm, tme):
    bound = 2 * t + (t // tm) * N_EXPERTS * (ROW_GRANULE - 1) + N_EXPERTS * (tme - 1)
    return (bound + tme - 1) // tme * tme


def _dispatch_kernel(row_ref, off_ref, gran_ref, h_ref, r_ref, xs_in, xs_out, dest_ref, sorted_ref, sem, *, tm):
    del xs_in
    i = pl.program_id(0)
    rt = r_ref[...].T
    e1, e2 = rt[0:1, :], rt[1:2, :]
    rank1, rank2 = rt[ROUTE_RANK_OFF:ROUTE_RANK_OFF + 1, :], rt[ROUTE_RANK_OFF + 1:ROUTE_RANK_OFF + 2, :]
    slot1, slot2, dest1, dest2 = rank1, rank2, rank1, rank2
    for e in range(N_EXPERTS):
        off = off_ref[i * N_EXPERTS + e].astype(F32)
        row = row_ref[i * N_EXPERTS + e].astype(F32)
        slot1 = slot1 + jnp.where(e1 == e, off, 0.0)
        slot2 = slot2 + jnp.where(e2 == e, off, 0.0)
        dest1 = dest1 + jnp.where(e1 == e, row, 0.0)
        dest2 = dest2 + jnp.where(e2 == e, row, 0.0)
    sub = lax.broadcasted_iota(jnp.int32, (dest_ref.shape[1], tm), 0)
    dest_ref[0] = jnp.where(sub == 0, dest1, jnp.where(sub == 1, dest2, 0.0)).astype(jnp.int32)
    rows = lax.broadcasted_iota(jnp.int32, (sorted_ref.shape[0], tm), 0).astype(F32)
    perm = ((rows == slot1) | (rows == slot2)).astype(BF16)
    sorted_ref[...] = _dot(perm, h_ref[...]).astype(BF16)

    def copy(e, g):
        src = pl.multiple_of(off_ref[i * N_EXPERTS + e] + g * ROW_GRANULE, ROW_GRANULE)
        dst = pl.multiple_of(row_ref[i * N_EXPERTS + e] + g * ROW_GRANULE, ROW_GRANULE)
        return pltpu.make_async_copy(sorted_ref.at[pl.ds(src, ROW_GRANULE)], xs_out.at[pl.ds(dst, ROW_GRANULE)], sem)

    for e in range(N_EXPERTS):
        lax.fori_loop(0, gran_ref[i * N_EXPERTS + e], lambda g, c, e=e: (copy(e, g).start(), c)[1], 0)
    for e in range(N_EXPERTS):
        lax.fori_loop(0, gran_ref[i * N_EXPERTS + e], lambda g, c, e=e: (copy(e, g).wait(), c)[1], 0)


def _dispatch_call(tables, h, route, n_rows, tm):
    t, d = h.shape
    buf_rows = 2 * tm + N_EXPERTS * ROW_GRANULE
    hbm = pl.BlockSpec(memory_space=pl.ANY)
    sub = 8
    xs, dest = pl.pallas_call(
        functools.partial(_dispatch_kernel, tm=tm),
        out_shape=[jax.ShapeDtypeStruct((n_rows, d), BF16), jax.ShapeDtypeStruct((t // tm, sub, tm), jnp.int32)],
        grid_spec=pltpu.PrefetchScalarGridSpec(
            num_scalar_prefetch=3,
            grid=(t // tm,),
            in_specs=[pl.BlockSpec((tm, d), lambda i, *_: (i, 0)), pl.BlockSpec((tm, LANES), lambda i, *_: (i, 0)),
                      hbm],
            out_specs=[hbm, pl.BlockSpec((1, sub, tm), lambda i, *_: (i, 0, 0))],
            scratch_shapes=[pltpu.VMEM((buf_rows, d), BF16), pltpu.SemaphoreType.DMA(())]),
        input_output_aliases={5: 0},
        compiler_params=_cparams(("arbitrary",)),
        name="moe_dispatch",
    )(*tables, h, route, jnp.zeros((n_rows, d), BF16))
    return xs, dest[:, 0, :].reshape(t), dest[:, 1, :].reshape(t)


def _moe_layer(mixer, sc, sh, g2, wr_p, w1, w3, w2, layer, lg, lb, tiles_per_batch, tm, tme, tf):
    x2d, h, route, counts = _router_call(mixer, sc, sh, wr_p, tiles_per_batch, tm)
    tables, tile_expert, n_used = _dispatch_plan(route, counts, tm, tme)
    xs, dest1, dest2 = _dispatch_call(tables, h, route, _sorted_rows(x2d.shape[0], tm, tme), tm)
    ys = _expert_call(tile_expert, n_used, xs, w1, w3, w2, layer, tme, tf)
    ya = ys.at[dest1].get(mode="promise_in_bounds")
    yb = ys.at[dest2].get(mode="promise_in_bounds")
    return _combine_call(x2d, ya, yb, route, g2, lg, lb, tiles_per_batch, tm)


def _trig_kernel(a_ref, c_ref, s_ref):
    a = a_ref[...]
    c_ref[...] = jnp.cos(a)
    s_ref[...] = jnp.sin(a)


def _trig_call(ang2d):
    n = ang2d.shape[0]
    tr = math.gcd(n, 1024)
    spec = pl.BlockSpec((tr, LANES), lambda i: (i, 0))
    return pl.pallas_call(
        _trig_kernel,
        out_shape=[jax.ShapeDtypeStruct(ang2d.shape, F32)] * 2,
        grid=(n // tr,),
        in_specs=[spec],
        out_specs=[spec, spec],
        compiler_params=_cparams(("parallel",)),
        name="rope_trig",
    )(ang2d)


def _rope_tables(positions):
    pos = positions.astype(F32)[..., None]
    lane = jnp.arange(LANES)
    inv64 = ROPE_THETA ** (-jnp.arange(0, RET_D, 2, dtype=F32) / RET_D)
    invm = ROPE_THETA ** (-jnp.arange(0, MLA_ROPE, 2, dtype=F32) / MLA_ROPE)
    n64 = inv64.shape[0]
    ang = pos * jnp.concatenate([inv64, invm])
    cos, sin = _trig_call(ang.reshape(-1, LANES))
    cos, sin = cos.reshape(ang.shape), sin.reshape(ang.shape)
    cos64, sin64, cosm, sinm = cos[..., :n64], sin[..., :n64], cos[..., n64:], sin[..., n64:]
    sign64 = jnp.where((lane % 64) < 32, -1.0, 1.0)
    c64 = jnp.tile(cos64, (1, 1, LANES // 32))
    s64 = jnp.tile(sin64, (1, 1, LANES // 32)) * sign64
    in_rope = (lane >= KR_LANE) & (lane < KR_LANE + MLA_ROPE)
    signm = jnp.where(lane < KR_LANE + MLA_ROPE // 2, -1.0, 1.0)
    cm = jnp.where(in_rope, jnp.tile(cosm, (1, 1, LANES // 16)), 1.0)
    sm = jnp.where(in_rope, jnp.tile(sinm, (1, 1, LANES // 16)) * signm, 0.0)
    return c64, s64, cm, sm


def _prep_mixer_weights(w_in, w_uq, w_ukv):
    d = w_in.shape[0]
    kr_cols = jnp.zeros((d, LANES), w_in.dtype).at[:, KR_LANE:KR_LANE + MLA_ROPE].set(
        w_in[:, MLA_OFF + MLA_Q_RANK + MLA_KV_RANK:])
    w_in_p = jnp.concatenate([w_in[:, :MLA_OFF + MLA_Q_RANK + MLA_KV_RANK], kr_cols], axis=1).astype(BF16)
    uq = w_uq.reshape(MLA_Q_RANK, MLA_HEADS, MLA_NOPE + MLA_ROPE)
    wuq_p = jnp.pad(uq, ((0, 0), (0, 0), (0, LANES - MLA_NOPE - MLA_ROPE))).reshape(MLA_Q_RANK, MLA_PAD).astype(BF16)
    ukv = w_ukv.reshape(MLA_KV_RANK, MLA_HEADS, MLA_NOPE + MLA_V)
    wkn_p = jnp.pad(ukv[:, :, :MLA_NOPE], ((0, 0), (0, 0), (0, LANES - MLA_NOPE))).reshape(MLA_KV_RANK, MLA_PAD).astype(BF16)
    wv = ukv[:, :, MLA_NOPE:].reshape(MLA_KV_RANK, MLA_VW).astype(BF16)
    return w_in_p, wuq_p, wkn_p, wv


def kernel(x, c, positions, w_in, ret_gn_g, mla_qn_g, mla_kvn_g, w_uq, w_ukv, w_out, w_ada, b_ada, ln1_g, ln1_b, ln2_g, ln2_b, w1_dense, w3_dense, w2_dense, w_router, w1_moe, w3_moe, w2_moe):
    b, s, d = x.shape
    tabs = _rope_tables(positions)
    ret_tabs = _ret_tables()
    mod = _ada_call(c, w_ada, b_ada)
    tm_tok = 512
    tiles_per_batch = s // tm_tok
    w_out_b = _cast_call(w_out)
    for l in range(w_in.shape[0]):
        sh1, sc1, g1, sh2, sc2, g2 = [mod[l, :, j * d:(j + 1) * d].reshape(b, 1, d) for j in range(6)]
        w_in_p, wuq_p, wkn_p, wv = _prep_mixer_weights(w_in[l], w_uq[l], w_ukv[l])
        (rq, rk, rv, rg), dil_views, (mq, mk, mv) = _inproj_call(
            x, sc1, sh1, w_in_p, tabs, mla_qn_g[l].reshape(1, -1), mla_kvn_g[l].reshape(1, -1), wuq_p, wkn_p, wv)
        ya = _ret_call(rq, rk, rv, rg, ret_gn_g[l].reshape(1, -1), ret_tabs)
        yb = _dil_call(dil_views)
        yc = _mla_call(mq, mk, mv)
        mixer = _mixer_operands(ya, yb, yc, w_out_b, l, x.reshape(b * s, d), g1, ln1_g[l].reshape(1, d),
                                ln1_b[l].reshape(1, d), tiles_per_batch, tm_tok)
        lg, lb = ln2_g[l].reshape(1, d), ln2_b[l].reshape(1, d)
        j = l // 2
        if l % 2 == 0:
            x2d = _ffn_call(mixer, sc2, sh2, g2, w1_dense, w3_dense, w2_dense, j, lg, lb, tiles_per_batch, tm_tok,
                            FF_CHUNK)
        else:
            wr_p = jnp.pad(w_router[j], ((0, 0), (0, LANES - N_EXPERTS)))
            x2d = _moe_layer(mixer, sc2, sh2, g2, wr_p, w1_moe, w3_moe, w2_moe, j, lg, lb, tiles_per_batch, tm_tok,
                             EXPERT_TILE, FF_CHUNK)
        x = x2d.reshape(b, s, d)
    return x
```

```python
import functools
import math

import jax
import jax.numpy as jnp
from jax import lax
from jax.experimental import pallas as pl
from jax.experimental.pallas import tpu as pltpu

DEPTH = 4
RET_HEADS = 4
RET_D = 64
RET_CHUNK = 128
DIL_HEADS = 6
DIL_D = 64
DIL_DILATIONS = (1, 4, 16)
DIL_W = 128
MLA_HEADS = 6
MLA_Q_RANK = 384
MLA_KV_RANK = 256
MLA_NOPE = 64
MLA_ROPE = 32
MLA_V = 64
N_EXPERTS = 8
ROPE_THETA = 10000.0
LN_EPS = 1e-5
RMS_EPS = 1e-6
ALPHA = (2.0 * DEPTH) ** 0.25

LANES = 128
RET_W = RET_HEADS * RET_D
DIL_WD = DIL_HEADS * DIL_D
MLA_PAD = MLA_HEADS * LANES
MLA_VW = MLA_HEADS * MLA_V
RET_IN = 4 * RET_W
DIL_IN = 3 * DIL_WD
MLA_OFF = RET_IN + DIL_IN
D_IN_PAD = MLA_OFF + MLA_Q_RANK + MLA_KV_RANK + LANES
KR_LANE = MLA_NOPE

VMEM_LIMIT = 48 * 1024 * 1024
FFN_VMEM_LIMIT = 56 * 1024 * 1024
FF_CHUNK = 512
EXPERT_TILE = 512
CAST_BLOCK_BYTES = 4 * 1024 * 1024
ROW_GRANULE = 16
BF16 = jnp.bfloat16
F32 = jnp.float32
NEG_INF = float("-inf")
LOG2E = math.log2(math.e)
MLA_Q_SCALE = (MLA_NOPE + MLA_ROPE) ** -0.5 * LOG2E


def _cparams(sem, vmem_limit=VMEM_LIMIT):
    return pltpu.CompilerParams(dimension_semantics=sem, vmem_limit_bytes=vmem_limit)


def _dot(a, b):
    return jnp.dot(a, b, preferred_element_type=F32)


def _dot_nt(a, b):
    return lax.dot_general(a, b, (((1,), (1,)), ((), ())), preferred_element_type=F32)


def _dot_tn(a, b):
    return lax.dot_general(a, b, (((0,), (0,)), ((), ())), preferred_element_type=F32)


def _silu(x):
    return x * (1.0 / (1.0 + jnp.exp(-x)))


def _layer_norm(v, g, b):
    mu = jnp.mean(v, axis=-1, keepdims=True)
    d = v - mu
    var = jnp.mean(d * d, axis=-1, keepdims=True)
    return d * lax.rsqrt(var + LN_EPS) * g + b


def _lane_ids(shape):
    return lax.broadcasted_iota(jnp.int32, shape, len(shape) - 1)


def _rope_group(x, cos, sin_signed, first_half, half):
    fwd = pltpu.roll(x, LANES - half, 1)
    bwd = pltpu.roll(x, half, 1)
    return x * cos + jnp.where(first_half, fwd, bwd) * sin_signed


def _cast_kernel(w_ref, o_ref):
    o_ref[...] = w_ref[...].astype(BF16)


def _cast_call(w):
    cols = w.shape[-1]
    w2d = w.reshape(-1, cols)
    rows = w2d.shape[0]
    pack = 16
    tr = max(pack, CAST_BLOCK_BYTES // (4 * cols) // pack * pack)
    while rows % tr:
        tr -= pack
    spec = pl.BlockSpec((tr, cols), lambda i: (i, 0))
    out = pl.pallas_call(
        _cast_kernel,
        out_shape=jax.ShapeDtypeStruct(w2d.shape, BF16),
        grid=(rows // tr,),
        in_specs=[spec],
        out_specs=spec,
        compiler_params=_cparams(("parallel",)),
        name="cast_bf16",
    )(w2d)
    return out.reshape(w.shape)


def _ada_kernel(c_ref, w_ref, b_ref, o_ref):
    cond = _silu(c_ref[...])
    o_ref[0] = jnp.dot(cond, w_ref[0], preferred_element_type=F32,
                       precision=lax.Precision.HIGHEST) + b_ref[0]


def _ada_call(c, w_ada, b_ada):
    nl, d, n = w_ada.shape
    b = c.shape[0]
    tn = 1536
    return pl.pallas_call(
        _ada_kernel,
        out_shape=jax.ShapeDtypeStruct((nl, b, n), F32),
        grid=(nl, n // tn),
        in_specs=[pl.BlockSpec((b, d), lambda l, j: (0, 0)),
                  pl.BlockSpec((1, d, tn), lambda l, j: (l, 0, j)),
                  pl.BlockSpec((1, 1, tn), lambda l, j: (l, 0, j))],
        out_specs=pl.BlockSpec((1, b, tn), lambda l, j: (l, 0, j)),
        compiler_params=_cparams(("arbitrary", "arbitrary")),
        name="ada_mod",
    )(c, w_ada, b_ada.reshape(nl, 1, n))


def _inproj_kernel(x_ref, sc_ref, sh_ref, w_ref, c64_ref, s64_ref, cm_ref, sm_ref,
                   qg_ref, kvg_ref, wuq_ref, wkn_ref, wv_ref,
                   rq_ref, rk_ref, rv_ref, rg_ref,
                   dq1_ref, dk1_ref, dv1_ref, dq4_ref, dk4_ref, dv4_ref, dq16_ref, dk16_ref, dv16_ref,
                   mq_ref, mk_ref, mv_ref, scr_ref, *, tm):
    h = (x_ref[0] * (1.0 + sc_ref[0]) + sh_ref[0]).astype(BF16)
    c64 = c64_ref[0]
    s64 = s64_ref[0]
    cm = cm_ref[0]
    sm = sm_ref[0]
    lane = _lane_ids((1, LANES))
    first64 = (lane % 64) < 32
    firstm = lane < (KR_LANE + MLA_ROPE // 2)

    def rope64(zc):
        return _rope_group(zc, c64, s64, first64, 32)

    def ropem(zc):
        return _rope_group(zc, cm, sm, firstm, MLA_ROPE // 2)

    zm = _dot(h, w_ref[:, MLA_OFF:D_IN_PAD])
    cq = zm[:, 0:MLA_Q_RANK]
    ckv = zm[:, MLA_Q_RANK:MLA_Q_RANK + MLA_KV_RANK]
    kr = ropem(zm[:, MLA_Q_RANK + MLA_KV_RANK:])
    cqn = (cq * lax.rsqrt(jnp.mean(cq * cq, axis=-1, keepdims=True) + RMS_EPS) * qg_ref[...]).astype(BF16)
    ckvn = (ckv * lax.rsqrt(jnp.mean(ckv * ckv, axis=-1, keepdims=True) + RMS_EPS) * kvg_ref[...]).astype(BF16)
    q = _dot(cqn, wuq_ref[...])
    kn = _dot(ckvn, wkn_ref[...])
    for hh in range(MLA_HEADS):
        sl = slice(hh * LANES, (hh + 1) * LANES)
        mq_ref[0, :, sl] = (ropem(q[:, hh * LANES:(hh + 1) * LANES]) * MLA_Q_SCALE).astype(BF16)
        mk_ref[0, :, sl] = (kn[:, hh * LANES:(hh + 1) * LANES] + kr).astype(BF16)
    mv_ref[0] = _dot(ckvn, wv_ref[...]).astype(BF16)

    za = _dot(h, w_ref[:, 0:RET_IN])
    for j in range(RET_W // LANES):
        sl = slice(j * LANES, (j + 1) * LANES)
        rq_ref[0, :, sl] = rope64(za[:, j * LANES:(j + 1) * LANES]).astype(BF16)
        kc = za[:, RET_W + j * LANES:RET_W + (j + 1) * LANES]
        rk_ref[0, :, sl] = (rope64(kc) * (RET_D ** -0.5)).astype(BF16)
    rv_ref[0] = za[:, 2 * RET_W:3 * RET_W].astype(BF16)
    rg_ref[0] = za[:, 3 * RET_W:4 * RET_W]

    zb = _dot(h, w_ref[:, RET_IN:MLA_OFF])
    q_scale = (DIL_D ** -0.5) * LOG2E
    views = ((dq1_ref, dq4_ref, dq16_ref), (dk1_ref, dk4_ref, dk16_ref), (dv1_ref, dv4_ref, dv16_ref))
    for a, (n1, n4, n16) in enumerate(views):
        for j in range(DIL_WD // LANES):
            zc = zb[:, a * DIL_WD + j * LANES:a * DIL_WD + (j + 1) * LANES]
            if a == 0:
                zc = rope64(zc) * q_scale
            elif a == 1:
                zc = rope64(zc)
            scr_ref[j] = zc
            n1[0, :, j * LANES:(j + 1) * LANES] = zc.astype(BF16)
        for r, ref in ((4, n4), (16, n16)):
            for rho in range(r):
                for j in range(DIL_WD // LANES):
                    ref[0, :, rho * DIL_WD + j * LANES:rho * DIL_WD + (j + 1) * LANES] = (
                        scr_ref[j, pl.ds(rho, tm // r, stride=r), :].astype(BF16))

def _inproj_call(x, sc, sh, w_in_p, tabs, qg, kvg, wuq_p, wkn_p, wv, tm=512):
    b, s, d = x.shape
    c64, s64, cm, sm = tabs
    tok = lambda w: pl.BlockSpec((1, tm, w), lambda i, j: (i, j, 0))
    per_b = pl.BlockSpec((1, 1, d), lambda i, j: (i, 0, 0))
    full = lambda a: pl.BlockSpec(a.shape, lambda i, j: (0,) * a.ndim)
    outs = [(1, RET_W, BF16), (1, RET_W, BF16), (1, RET_W, BF16), (1, RET_W, F32)]
    outs += [(r, DIL_WD, BF16) for r in DIL_DILATIONS for _ in range(3)]
    outs += [(1, MLA_PAD, BF16), (1, MLA_PAD, BF16), (1, MLA_VW, BF16)]
    res = pl.pallas_call(
        functools.partial(_inproj_kernel, tm=tm),
        out_shape=[jax.ShapeDtypeStruct((b, s // r, r * w), dt) for r, w, dt in outs],
        grid=(b, s // tm),
        in_specs=[tok(d), per_b, per_b, full(w_in_p), tok(LANES), tok(LANES), tok(LANES), tok(LANES),
                  full(qg), full(kvg), full(wuq_p), full(wkn_p), full(wv)],
        out_specs=[pl.BlockSpec((1, tm // r, r * w), lambda i, j: (i, j, 0)) for r, w, _ in outs],
        scratch_shapes=[pltpu.VMEM((DIL_WD // LANES, tm, LANES), F32)],
        compiler_params=_cparams(("parallel", "parallel")),
        name="in_proj",
    )(x, sc, sh, w_in_p, c64, s64, cm, sm, qg, kvg, wuq_p, wkn_p, wv)
    ret = res[0:4]
    dil = [res[4 + 3 * i:7 + 3 * i] for i in range(len(DIL_DILATIONS))]
    mla = res[4 + 3 * len(DIL_DILATIONS):]
    return ret, dil, mla


def _ret_kernel(q_ref, k_ref, v_ref, g_ref, gn_ref, intra_ref, xi_ref, zeta_ref, dmat_ref, bmask_ref,
                o_ref, state_ref, *, chunks):
    @pl.when(pl.program_id(1) == 0)
    def _():
        state_ref[...] = jnp.zeros_like(state_ref)

    lane = _lane_ids((1, LANES))
    head_a = lane < RET_D
    c = RET_CHUNK
    npair = RET_W // LANES
    items = [(ci, p) for ci in range(chunks) for p in range(npair)]
    view = lambda ref, ci, p: ref[0, ci * c:(ci + 1) * c, p * LANES:(p + 1) * LANES]

    inner, kv_inc = {}, {}
    for ci, p in items:
        cols = slice(p * LANES, (p + 1) * LANES)
        q, k, v = view(q_ref, ci, p), view(k_ref, ci, p), view(v_ref, ci, p)
        zero = jnp.zeros_like(q)
        s_a = _dot_nt(jnp.where(head_a, q, zero), k) * intra_ref[2 * p]
        s_b = _dot_nt(jnp.where(head_a, zero, q), k) * intra_ref[2 * p + 1]
        s_cat = jnp.concatenate([s_a, s_b], axis=1).astype(BF16)
        v_stack = jnp.concatenate([jnp.where(head_a, v, zero), jnp.where(head_a, zero, v)], axis=0)
        inner[(ci, p)] = _dot(s_cat, v_stack)
        kz = (k.astype(F32) * zeta_ref[:, cols]).astype(BF16)
        kv_inc[(ci, p)] = _dot_tn(kz, v) * bmask_ref[...]

    outs = {}
    for p in range(npair):
        cols = slice(p * LANES, (p + 1) * LANES)
        state = state_ref[p]
        for ci in range(chunks):
            cross = _dot(view(q_ref, ci, p), state.astype(BF16)) * xi_ref[:, cols]
            outs[(ci, p)] = inner[(ci, p)] + cross
            state = state * dmat_ref[p] + kv_inc[(ci, p)]
        state_ref[p] = state

    inv_n = 1.0 / RET_D
    for ci, p in items:
        cols = slice(p * LANES, (p + 1) * LANES)
        o = outs[(ci, p)]
        mu = jnp.where(head_a,
                       jnp.sum(jnp.where(head_a, o, 0.0), axis=-1, keepdims=True),
                       jnp.sum(jnp.where(head_a, 0.0, o), axis=-1, keepdims=True)) * inv_n
        d = o - mu
        dd = d * d
        var = jnp.where(head_a,
                        jnp.sum(jnp.where(head_a, dd, 0.0), axis=-1, keepdims=True),
                        jnp.sum(jnp.where(head_a, 0.0, dd), axis=-1, keepdims=True)) * inv_n
        on = d * lax.rsqrt(var + LN_EPS) * gn_ref[:, cols]
        o_ref[0, ci * c:(ci + 1) * c, cols] = (_silu(view(g_ref, ci, p)) * on).astype(BF16)


def _ret_tables():
    c = RET_CHUNK
    log_g = jnp.log(1.0 - 2.0 ** (-5.0 - jnp.arange(RET_HEADS, dtype=F32)))
    idx = jnp.arange(c, dtype=F32)
    diff = idx[:, None] - idx[None, :]
    intra = jnp.where(diff >= 0, jnp.exp(jnp.maximum(diff, 0.0) * log_g[:, None, None]), 0.0)
    xi = jnp.exp((idx[:, None] + 1.0) * log_g[None, :])
    zeta = jnp.exp((c - 1.0 - idx[:, None]) * log_g[None, :])
    decay = jnp.exp(c * log_g)
    xi_l = jnp.repeat(xi, RET_D, axis=1)
    zeta_l = jnp.repeat(zeta, RET_D, axis=1)
    r = jnp.arange(LANES)
    same = (r[:, None] // RET_D) == (r[None, :] // RET_D)
    bmask = same.astype(F32)
    dec_l = jnp.repeat(decay, RET_D).reshape(RET_W // LANES, 1, LANES)
    dmat = bmask[None] * dec_l
    return intra, xi_l, zeta_l, dmat, bmask


def _ret_call(rq, rk, rv, rg, gn, tables, chunks=8):
    b, s, w = rq.shape
    tm = chunks * RET_CHUNK
    intra, xi_l, zeta_l, dmat, bmask = tables
    tok = pl.BlockSpec((1, tm, w), lambda i, j: (i, j, 0))
    full = lambda a: pl.BlockSpec(a.shape, lambda i, j: (0,) * a.ndim)
    return pl.pallas_call(
        functools.partial(_ret_kernel, chunks=chunks),
        out_shape=jax.ShapeDtypeStruct((b, s, w), BF16),
        grid=(b, s // tm),
        in_specs=[tok, tok, tok, tok, full(gn), full(intra), full(xi_l), full(zeta_l), full(dmat), full(bmask)],
        out_specs=tok,
        scratch_shapes=[pltpu.VMEM((w // LANES, LANES, LANES), F32)],
        compiler_params=_cparams(("parallel", "arbitrary")),
        name="retention",
    )(rq, rk, rv, rg, gn, intra, xi_l, zeta_l, dmat, bmask)


def _dil_band(first_block):
    w = DIL_W
    qi = lax.broadcasted_iota(jnp.int32, (w, 2 * w), 0)
    kj = lax.broadcasted_iota(jnp.int32, (w, 2 * w), 1)
    dist = w + qi - kj
    band = (dist >= 0) & (dist <= w)
    if first_block is False:
        return band
    return band & (jnp.logical_not(first_block) | (kj >= w))


DIL_GROUP = 2


def _dil_attend(items, valid_of):
    head_a = _lane_ids((1, LANES)) < DIL_D
    scores = []
    for idx, (q, kcat, _) in enumerate(items):
        zero = jnp.zeros_like(q)
        valid = valid_of(idx)
        scores.append([jnp.where(valid, _dot_nt(qh, kcat), NEG_INF)
                       for qh in (jnp.where(head_a, q, zero), jnp.where(head_a, zero, q))])
    soft = []
    for scs in scores:
        ms, ls, ps = [], [], []
        for sc in scs:
            m = jnp.max(sc, axis=-1, keepdims=True)
            pe = jnp.exp2(sc - m)
            ms.append(m)
            ls.append(jnp.sum(pe, axis=-1, keepdims=True))
            ps.append(pe.astype(BF16))
        soft.append((ms, ls, jnp.concatenate(ps, axis=1)))
    out = []
    for (_, _, vcat), (ms, ls, pcat) in zip(items, soft):
        zv = jnp.zeros_like(vcat)
        v_stack = jnp.concatenate([jnp.where(head_a, vcat, zv), jnp.where(head_a, zv, vcat)], axis=0)
        out.append((_dot(pcat, v_stack), ms, ls))
    return out


def _dil_normalise(acc, ms, ls):
    head_a = _lane_ids((1, LANES)) < DIL_D
    o = acc * jnp.where(head_a, 1.0 / ls[0], 1.0 / ls[1])
    lse = jnp.where(head_a, ms[0] + jnp.log2(ls[0]), ms[1] + jnp.log2(ls[1]))
    return o, lse


def _dil_partial_kernel(q_ref, kp_ref, kc_ref, vp_ref, vc_ref, o_ref, lse_ref, *, r):
    valid = _dil_band(pl.program_id(1) == 0)
    npair = DIL_WD // LANES
    for g0 in range(0, r, DIL_GROUP):
        rhos = range(g0, min(g0 + DIL_GROUP, r))
        items = []
        for rho in rhos:
            for p in range(npair):
                cols = slice(rho * DIL_WD + p * LANES, rho * DIL_WD + (p + 1) * LANES)
                items.append((q_ref[0, :, cols],
                              jnp.concatenate([kp_ref[0, :, cols], kc_ref[0, :, cols]], axis=0),
                              jnp.concatenate([vp_ref[0, :, cols], vc_ref[0, :, cols]], axis=0)))
        res = _dil_attend(items, lambda idx: valid)
        for gi, rho in enumerate(rhos):
            rows = pl.ds(rho, DIL_W, stride=r)
            for p in range(npair):
                o, lse = _dil_normalise(*res[gi * npair + p])
                o_ref[0, p, rows, :] = o
                lse_ref[0, p, rows, :] = lse


def _dil_final_kernel(q_ref, k_ref, kh_ref, v_ref, vh_ref, o4_ref, lse4_ref, o16_ref, lse16_ref, o_ref, *, nblk):
    w = DIL_W
    npair = DIL_WD // LANES
    band_first = _dil_band(pl.program_id(1) == 0)
    band = _dil_band(False)

    for g0 in range(0, nblk, DIL_GROUP):
        blocks = range(g0, min(g0 + DIL_GROUP, nblk))
        items = []
        for i in blocks:
            rows = slice(i * w, (i + 1) * w)
            for p in range(npair):
                cols = slice(p * LANES, (p + 1) * LANES)
                if i == 0:
                    kcat = jnp.concatenate([kh_ref[0, :, cols], k_ref[0, rows, cols]], axis=0)
                    vcat = jnp.concatenate([vh_ref[0, :, cols], v_ref[0, rows, cols]], axis=0)
                else:
                    kcat = k_ref[0, (i - 1) * w:(i + 1) * w, cols]
                    vcat = v_ref[0, (i - 1) * w:(i + 1) * w, cols]
                items.append((q_ref[0, rows, cols], kcat, vcat))
        res = _dil_attend(items, lambda idx: band_first if blocks[idx // npair] == 0 else band)
        for gi, i in enumerate(blocks):
            rows = slice(i * w, (i + 1) * w)
            for p in range(npair):
                o1, lse1 = _dil_normalise(*res[gi * npair + p])
                lse4 = lse4_ref[0, p, rows, :]
                lse16 = lse16_ref[0, p, rows, :]
                top = jnp.maximum(jnp.maximum(lse1, lse4), lse16)
                w1, w4, w16 = jnp.exp2(lse1 - top), jnp.exp2(lse4 - top), jnp.exp2(lse16 - top)
                num = w1 * o1 + w4 * o4_ref[0, p, rows, :] + w16 * o16_ref[0, p, rows, :]
                o_ref[0, rows, p * LANES:(p + 1) * LANES] = (num / (w1 + w4 + w16)).astype(BF16)


def _dil_call(views, nblk=8):
    (q1, k1, v1) = views[0]
    b, s, w = q1.shape
    npair = w // LANES
    partial = []
    for r, (q, k, v) in zip(DIL_DILATIONS[1:], views[1:]):
        cur = pl.BlockSpec((1, DIL_W, r * w), lambda bb, n: (bb, n, 0))
        prev = pl.BlockSpec((1, DIL_W, r * w), lambda bb, n: (bb, jnp.maximum(n - 1, 0), 0))
        partial += pl.pallas_call(
            functools.partial(_dil_partial_kernel, r=r),
            out_shape=[jax.ShapeDtypeStruct((b, npair, s, LANES), F32)] * 2,
            grid=(b, s // (r * DIL_W)),
            in_specs=[cur, prev, cur, prev, cur],
            out_specs=[pl.BlockSpec((1, npair, r * DIL_W, LANES), lambda bb, n: (bb, 0, n, 0))] * 2,
            compiler_params=_cparams(("parallel", "arbitrary")),
            name=f"dilated_r{r}",
        )(q, k, k, v, v)
    tm = nblk * DIL_W
    tok = lambda width: pl.BlockSpec((1, tm, width), lambda bb, n: (bb, n, 0))
    halo = pl.BlockSpec((1, DIL_W, w), lambda bb, n: (bb, jnp.maximum(n * nblk - 1, 0), 0))
    accs = pl.BlockSpec((1, npair, tm, LANES), lambda bb, n: (bb, 0, n, 0))
    return pl.pallas_call(
        functools.partial(_dil_final_kernel, nblk=nblk),
        out_shape=jax.ShapeDtypeStruct((b, s, w), BF16),
        grid=(b, s // tm),
        in_specs=[tok(w), tok(w), halo, tok(w), halo, accs, accs, accs, accs],
        out_specs=tok(w),
        compiler_params=_cparams(("parallel", "arbitrary")),
        name="dilated_r1_merge",
    )(q1, k1, k1, v1, v1, *partial)


def _mla_kernel(q_ref, k_ref, v_ref, o_ref, m_ref, l_ref, acc_ref, *, t, sub):
    qi = pl.program_id(2)
    ki = pl.program_id(3)
    head_a = _lane_ids((1, LANES)) < MLA_V
    nsub = t // sub

    @pl.when(ki == 0)
    def _():
        m_ref[...] = jnp.full_like(m_ref, NEG_INF)
        l_ref[...] = jnp.zeros_like(l_ref)
        acc_ref[...] = jnp.zeros_like(acc_ref)

    def v_stack(c):
        v = v_ref[0, c * sub:(c + 1) * sub]
        zv = jnp.zeros_like(v)
        return jnp.concatenate([jnp.where(head_a, v, zv), jnp.where(head_a, zv, v)], axis=0)

    def column(c, rs, diag_r):
        vs = v_stack(c)
        scores = {}
        for r in rs:
            rows = slice(r * sub, (r + 1) * sub)
            for j in range(2):
                q = q_ref[0, rows, j * LANES:(j + 1) * LANES]
                k = k_ref[0, c * sub:(c + 1) * sub, j * LANES:(j + 1) * LANES]
                s = _dot_nt(q, k)
                if r == diag_r:
                    row = lax.broadcasted_iota(jnp.int32, (sub, sub), 0)
                    colk = lax.broadcasted_iota(jnp.int32, (sub, sub), 1)
                    s = jnp.where(colk <= row, s, NEG_INF)
                scores[(r, j)] = s
        probs = {}
        for r in rs:
            rows = slice(r * sub, (r + 1) * sub)
            ps, alphas = [], []
            for j in range(2):
                s = scores[(r, j)]
                m_old = m_ref[j, rows]
                m_new = jnp.maximum(m_old, jnp.max(s, axis=-1, keepdims=True))
                alpha = jnp.exp2(m_old - m_new)
                pe = jnp.exp2(s - jnp.concatenate([m_new] * (sub // LANES), axis=1))
                l_ref[j, rows] = alpha * l_ref[j, rows] + jnp.sum(pe, axis=-1, keepdims=True)
                m_ref[j, rows] = m_new
                ps.append(pe.astype(BF16))
                alphas.append(alpha)
            probs[r] = (jnp.concatenate(ps, axis=1), jnp.where(head_a, alphas[0], alphas[1]))
        for r in rs:
            rows = slice(r * sub, (r + 1) * sub)
            pcat, alpha = probs[r]
            acc_ref[rows] = acc_ref[rows] * alpha + _dot(pcat, vs)

    @pl.when(ki < qi)
    def _():
        for c in range(nsub):
            column(c, range(nsub), None)

    @pl.when(ki == qi)
    def _():
        for c in range(nsub):
            column(c, range(c, nsub), c)
        o_ref[0] = (acc_ref[...] / jnp.where(head_a, l_ref[0], l_ref[1])).astype(BF16)


def _mla_call(mq, mk, mv, t=2048, sub=512):
    b, s, _ = mq.shape
    n = s // t
    return pl.pallas_call(
        functools.partial(_mla_kernel, t=t, sub=sub),
        out_shape=jax.ShapeDtypeStruct((b, s, MLA_VW), BF16),
        grid=(b, MLA_HEADS // 2, n, n),
        in_specs=[pl.BlockSpec((1, t, 2 * LANES), lambda bb, p, i, j: (bb, i, p)),
                  pl.BlockSpec((1, t, 2 * LANES), lambda bb, p, i, j: (bb, jnp.minimum(j, i), p)),
                  pl.BlockSpec((1, t, LANES), lambda bb, p, i, j: (bb, jnp.minimum(j, i), p))],
        out_specs=pl.BlockSpec((1, t, LANES), lambda bb, p, i, j: (bb, i, p)),
        scratch_shapes=[pltpu.VMEM((2, t, LANES), F32), pltpu.VMEM((2, t, LANES), F32),
                        pltpu.VMEM((t, LANES), F32)],
        compiler_params=_cparams(("parallel", "parallel", "parallel", "arbitrary")),
        name="mla_attention",
    )(mq, mk, mv)


N_MIX = 8
ROUTER_ROWS = 128


def _mixer_residual(ya_ref, yb_ref, yc_ref, w_ref, x_ref, g_ref, lg_ref, lb_ref, row_groups=None):
    a, b = RET_W, RET_W + DIL_WD
    groups = [slice(None)] if row_groups is None else row_groups
    ys = [_dot(ya_ref[r, :], w_ref[0, 0:a, :]) + _dot(yb_ref[r, :], w_ref[0, a:b, :])
          + _dot(yc_ref[r, :], w_ref[0, b:, :]) for r in groups]
    out = [_layer_norm(ALPHA * x_ref[r, :] + (1.0 + g_ref[0]) * y, lg_ref[...], lb_ref[...])
           for r, y in zip(groups, ys)]
    return out[0] if row_groups is None else out


def _mixer_operands(ya, yb, yc, w_out, layer, x2d, g1, lg, lb, tiles_per_batch, tm):
    t, d = x2d.shape
    tok = lambda a: pl.BlockSpec((tm, a.shape[-1]), lambda i, *_: (i, 0))
    flat = lambda a: a.reshape(t, a.shape[-1])
    arrays = [flat(ya), flat(yb), flat(yc), w_out, x2d, g1, lg, lb]
    specs = [tok(ya), tok(yb), tok(yc),
             pl.BlockSpec((1,) + w_out.shape[1:], lambda i, *_: (layer, 0, 0)),
             tok(x2d), pl.BlockSpec((1, 1, d), lambda i, *_: (i // tiles_per_batch, 0, 0)),
             pl.BlockSpec((1, d), lambda i, *_: (0, 0)), pl.BlockSpec((1, d), lambda i, *_: (0, 0))]
    return arrays, specs


def _swiglu(h, w1_ref, w3_ref, w2_ref, chunk):
    ff = w1_ref.shape[1]
    y = None
    for a in range(0, ff, chunk):
        b = min(a + chunk, ff)
        mid = (_silu(_dot(h, w1_ref[:, a:b])) * _dot(h, w3_ref[:, a:b])).astype(BF16)
        part = _dot(mid, w2_ref[a:b, :])
        y = part if y is None else y + part
    return y


def _load_weights_bf16(w1_hbm, w3_hbm, w2_hbm, w1_ref, w3_ref, w2_ref, stage_cols, stage_rows, sem, chunk):
    ff = w1_ref.shape[1]
    jobs = []
    for a in range(0, ff, chunk):
        b = min(a + chunk, ff)
        for src, dst in ((w1_hbm, w1_ref), (w3_hbm, w3_ref)):
            jobs.append((src.at[:, a:b], lambda s, n=b - a: stage_cols.at[s, :, 0:n], dst.at[:, a:b]))
        jobs.append((w2_hbm.at[a:b, :], lambda s, n=b - a: stage_rows.at[s, 0:n, :], w2_ref.at[a:b, :]))

    def copy(k):
        src, stage, _ = jobs[k]
        return pltpu.make_async_copy(src, stage(k % 2), sem.at[k % 2])

    copy(0).start()
    for k in range(len(jobs)):
        if k + 1 < len(jobs):
            copy(k + 1).start()
        copy(k).wait()
        _, stage, dst = jobs[k]
        dst[...] = stage(k % 2)[...].astype(BF16)


def _ffn_kernel(*refs, layer, chunk):
    mix = refs[:N_MIX]
    (sc_ref, sh_ref, g_ref, w1_hbm, w3_hbm, w2_hbm, lg_ref, lb_ref, o_ref,
     w1_ref, w3_ref, w2_ref, stage_cols, stage_rows, sem) = refs[N_MIX:]

    @pl.when(pl.program_id(0) == 0)
    def _():
        _load_weights_bf16(w1_hbm.at[layer], w3_hbm.at[layer], w2_hbm.at[layer], w1_ref, w3_ref, w2_ref,
                           stage_cols, stage_rows, sem, chunk)

    tm = o_ref.shape[0]
    groups = [slice(k, k + ROUTER_ROWS) for k in range(0, tm, ROUTER_ROWS)]
    xs = _mixer_residual(*mix, row_groups=groups)
    h = jnp.concatenate([(x * (1.0 + sc_ref[0]) + sh_ref[0]).astype(BF16) for x in xs], axis=0)
    y = _swiglu(h, w1_ref, w3_ref, w2_ref, chunk)
    for r, x in zip(groups, xs):
        o_ref[r, :] = _layer_norm(ALPHA * x + (1.0 + g_ref[0]) * y[r, :], lg_ref[...], lb_ref[...])


def _weight_scratch(d, ff, chunk):
    return [pltpu.VMEM((d, ff), BF16), pltpu.VMEM((d, ff), BF16), pltpu.VMEM((ff, d), BF16),
            pltpu.VMEM((2, d, chunk), F32), pltpu.VMEM((2, chunk, d), F32), pltpu.SemaphoreType.DMA((2,))]


def _ffn_call(mixer, sc, sh, g2, w1, w3, w2, layer, lg, lb, tiles_per_batch, tm, chunk):
    mix_arrays, mix_specs = mixer
    t, d = mix_arrays[4].shape
    ff = w1.shape[2]
    per_b = pl.BlockSpec((1, 1, d), lambda i: (i // tiles_per_batch, 0, 0))
    vec = pl.BlockSpec((1, d), lambda i: (0, 0))
    hbm = pl.BlockSpec(memory_space=pl.ANY)
    return pl.pallas_call(
        functools.partial(_ffn_kernel, layer=layer, chunk=chunk),
        out_shape=jax.ShapeDtypeStruct((t, d), F32),
        grid=(t // tm,),
        in_specs=mix_specs + [per_b, per_b, per_b, hbm, hbm, hbm, vec, vec],
        out_specs=pl.BlockSpec((tm, d), lambda i: (i, 0)),
        scratch_shapes=_weight_scratch(d, ff, chunk),
        compiler_params=_cparams(("arbitrary",), FFN_VMEM_LIMIT),
        name="ffn_dense",
    )(*mix_arrays, sc, sh, g2, w1, w3, w2, lg, lb)


ROUTE_G_OFF = 2
ROUTE_RANK_OFF = 4


def _router_kernel(*refs):
    sc_ref, sh_ref, wr_ref, tri_ref, x1_ref, h_ref, r_ref, cnt_ref = refs[N_MIX:]
    tm = x1_ref.shape[0]
    groups = [slice(k, k + ROUTER_ROWS) for k in range(0, tm, ROUTER_ROWS)]
    x1s = _mixer_residual(*refs[:N_MIX], row_groups=groups)
    hs = [x1 * (1.0 + sc_ref[0]) + sh_ref[0] for x1 in x1s]
    for r, x1, h in zip(groups, x1s, hs):
        x1_ref[r, :] = x1
        h_ref[r, :] = h.astype(BF16)
    logits = jnp.concatenate(
        [jnp.dot(h, wr_ref[...], preferred_element_type=F32, precision=lax.Precision.HIGHEST) for h in hs], axis=0)
    lane = _lane_ids(logits.shape)
    lg = jnp.where(lane < N_EXPERTS, logits, NEG_INF)
    m1 = jnp.max(lg, axis=-1, keepdims=True)
    i1 = jnp.min(jnp.where(lg == m1, lane, LANES), axis=-1, keepdims=True)
    lg2 = jnp.where(lane == i1, NEG_INF, lg)
    m2 = jnp.max(lg2, axis=-1, keepdims=True)
    i2 = jnp.min(jnp.where(lg2 == m2, lane, LANES), axis=-1, keepdims=True)
    e2 = jnp.exp(m2 - m1)
    den = 1.0 + e2
    chosen = (lane == i1) | (lane == i2)
    cum = _dot(tri_ref[...], chosen.astype(BF16))
    rank1 = jnp.sum(jnp.where(lane == i1, cum, 0.0), axis=-1, keepdims=True) - 1.0
    rank2 = jnp.sum(jnp.where(lane == i2, cum, 0.0), axis=-1, keepdims=True) - 1.0
    cnt_ref[0] = cum[cum.shape[0] - 1:, :]
    out = jnp.where(lane == 0, i1.astype(F32), 0.0)
    out = jnp.where(lane == 1, i2.astype(F32), out)
    out = jnp.where(lane == ROUTE_G_OFF, 1.0 / den, out)
    out = jnp.where(lane == ROUTE_G_OFF + 1, e2 / den, out)
    out = jnp.where(lane == ROUTE_RANK_OFF, rank1, out)
    out = jnp.where(lane == ROUTE_RANK_OFF + 1, rank2, out)
    r_ref[...] = out


def _router_call(mixer, sc, sh, wr_p, tiles_per_batch, tm):
    mix_arrays, mix_specs = mixer
    t, d = mix_arrays[4].shape
    per_b = pl.BlockSpec((1, 1, d), lambda i: (i // tiles_per_batch, 0, 0))
    tri = (jnp.arange(tm)[:, None] >= jnp.arange(tm)[None, :]).astype(BF16)
    wspec = pl.BlockSpec(wr_p.shape, lambda i: (0, 0))
    tok = pl.BlockSpec((tm, d), lambda i: (i, 0))
    return pl.pallas_call(
        _router_kernel,
        out_shape=[jax.ShapeDtypeStruct((t, d), F32), jax.ShapeDtypeStruct((t, d), BF16),
                   jax.ShapeDtypeStruct((t, LANES), F32), jax.ShapeDtypeStruct((t // tm, 1, LANES), F32)],
        grid=(t // tm,),
        in_specs=mix_specs + [per_b, per_b, wspec, pl.BlockSpec((tm, tm), lambda i: (0, 0))],
        out_specs=[tok, tok, pl.BlockSpec((tm, LANES), lambda i: (i, 0)),
                   pl.BlockSpec((1, 1, LANES), lambda i: (i, 0, 0))],
        compiler_params=_cparams(("parallel",)),
        name="moe_router",
    )(*mix_arrays, sc, sh, wr_p, tri)


def _expert_kernel(te_ref, nu_ref, x_ref, w1_hbm, w3_hbm, w2_hbm, o_ref,
                   w1_ref, w3_ref, w2_ref, stage_cols, stage_rows, sem, *, layer, chunk):
    i = pl.program_id(0)
    used = i < nu_ref[0]
    e = te_ref[i]
    new_expert = (i == 0) | (e != te_ref[jnp.maximum(i - 1, 0)])

    @pl.when(used & new_expert)
    def _():
        _load_weights_bf16(w1_hbm.at[layer, e], w3_hbm.at[layer, e], w2_hbm.at[layer, e], w1_ref, w3_ref, w2_ref,
                           stage_cols, stage_rows, sem, chunk)

    @pl.when(used)
    def _():
        o_ref[...] = _swiglu(x_ref[...], w1_ref, w3_ref, w2_ref, chunk).astype(o_ref.dtype)

    @pl.when(jnp.logical_not(used))
    def _():
        o_ref[...] = jnp.zeros_like(o_ref)


def _expert_call(tile_expert, n_used, xs, w1, w3, w2, layer, tm, chunk):
    p, d = xs.shape
    ff = w1.shape[3]
    hbm = pl.BlockSpec(memory_space=pl.ANY)
    return pl.pallas_call(
        functools.partial(_expert_kernel, layer=layer, chunk=chunk),
        out_shape=jax.ShapeDtypeStruct((p, d), BF16),
        grid_spec=pltpu.PrefetchScalarGridSpec(
            num_scalar_prefetch=2,
            grid=(p // tm,),
            in_specs=[pl.BlockSpec((tm, d), lambda i, te, nu: (i, 0)), hbm, hbm, hbm],
            out_specs=pl.BlockSpec((tm, d), lambda i, te, nu: (i, 0)),
            scratch_shapes=_weight_scratch(d, ff, chunk)),
        compiler_params=_cparams(("arbitrary",), FFN_VMEM_LIMIT),
        name="moe_experts",
    )(tile_expert, n_used, xs, w1, w3, w2)


def _combine_kernel(x_ref, ya_ref, yb_ref, r_ref, g_ref, lg_ref, lb_ref, o_ref):
    r = r_ref[...]
    lane = _lane_ids(r.shape)
    ga = jnp.sum(jnp.where(lane == ROUTE_G_OFF, r, 0.0), axis=-1, keepdims=True)
    gb = jnp.sum(jnp.where(lane == ROUTE_G_OFF + 1, r, 0.0), axis=-1, keepdims=True)
    y = ga * ya_ref[...].astype(F32) + gb * yb_ref[...].astype(F32)
    v = ALPHA * x_ref[...] + (1.0 + g_ref[0]) * y
    o_ref[...] = _layer_norm(v, lg_ref[...], lb_ref[...])


def _combine_call(x2d, ya, yb, route, g2, lg, lb, tiles_per_batch, tm):
    t, d = x2d.shape
    tok = pl.BlockSpec((tm, d), lambda i: (i, 0))
    per_b = pl.BlockSpec((1, 1, d), lambda i: (i // tiles_per_batch, 0, 0))
    vec = pl.BlockSpec((1, d), lambda i: (0, 0))
    return pl.pallas_call(
        _combine_kernel,
        out_shape=jax.ShapeDtypeStruct((t, d), F32),
        grid=(t // tm,),
        in_specs=[tok, tok, tok, pl.BlockSpec((tm, LANES), lambda i: (i, 0)), per_b, vec, vec],
        out_specs=tok,
        compiler_params=_cparams(("parallel",)),
        name="moe_combine_ln",
    )(x2d, ya, yb, route, g2, lg, lb)


def _dispatch_plan(route, counts, tm, tme):
    t = route.shape[0]
    nt = t // tm
    n = counts[:, 0, :N_EXPERTS].astype(jnp.int32)
    gran = (n + ROW_GRANULE - 1) // ROW_GRANULE
    seg = gran * ROW_GRANULE
    local_off = jnp.cumsum(seg, axis=1) - seg
    region = ((jnp.sum(seg, axis=0) + tme - 1) // tme) * tme
    ends = jnp.cumsum(region)
    seg_row = (ends - region)[None, :] + jnp.cumsum(seg, axis=0) - seg
    n_rows = _sorted_rows(t, tm, tme)
    tile_start = jnp.arange(n_rows // tme, dtype=jnp.int32) * tme
    tile_expert = jnp.minimum(jnp.sum((tile_start[:, None] >= ends[None, :]).astype(jnp.int32), axis=1),
                              N_EXPERTS - 1)
    n_used = (ends[-1] // tme).astype(jnp.int32).reshape(1)
    tables = (seg_row.reshape(-1), local_off.reshape(-1), gran.reshape(-1))
    return tables, tile_expert, n_used


def _sorted_rows(t, tm, tme):
    bound = 2 * t + (t // tm) * N_EXPERTS * (ROW_GRANULE - 1) + N_EXPERTS * (tme - 1)
    return (bound + tme - 1) // tme * tme


def _dispatch_kernel(row_ref, off_ref, gran_ref, h_ref, r_ref, xs_in, xs_out, dest_ref, sorted_ref, sem, *, tm):
    del xs_in
    i = pl.program_id(0)
    rt = r_ref[...].T
    e1, e2 = rt[0:1, :], rt[1:2, :]
    rank1, rank2 = rt[ROUTE_RANK_OFF:ROUTE_RANK_OFF + 1, :], rt[ROUTE_RANK_OFF + 1:ROUTE_RANK_OFF + 2, :]
    slot1, slot2, dest1, dest2 = rank1, rank2, rank1, rank2
    for e in range(N_EXPERTS):
        off = off_ref[i * N_EXPERTS + e].astype(F32)
        row = row_ref[i * N_EXPERTS + e].astype(F32)
        slot1 = slot1 + jnp.where(e1 == e, off, 0.0)
        slot2 = slot2 + jnp.where(e2 == e, off, 0.0)
        dest1 = dest1 + jnp.where(e1 == e, row, 0.0)
        dest2 = dest2 + jnp.where(e2 == e, row, 0.0)
    sub = lax.broadcasted_iota(jnp.int32, (dest_ref.shape[1], tm), 0)
    dest_ref[0] = jnp.where(sub == 0, dest1, jnp.where(sub == 1, dest2, 0.0)).astype(jnp.int32)
    rows = lax.broadcasted_iota(jnp.int32, (sorted_ref.shape[0], tm), 0).astype(F32)
    perm = ((rows == slot1) | (rows == slot2)).astype(BF16)
    sorted_ref[...] = _dot(perm, h_ref[...]).astype(BF16)

    def copy(e, g):
        src = pl.multiple_of(off_ref[i * N_EXPERTS + e] + g * ROW_GRANULE, ROW_GRANULE)
        dst = pl.multiple_of(row_ref[i * N_EXPERTS + e] + g * ROW_GRANULE, ROW_GRANULE)
        return pltpu.make_async_copy(sorted_ref.at[pl.ds(src, ROW_GRANULE)], xs_out.at[pl.ds(dst, ROW_GRANULE)], sem)

    for e in range(N_EXPERTS):
        lax.fori_loop(0, gran_ref[i * N_EXPERTS + e], lambda g, c, e=e: (copy(e, g).start(), c)[1], 0)
    for e in range(N_EXPERTS):
        lax.fori_loop(0, gran_ref[i * N_EXPERTS + e], lambda g, c, e=e: (copy(e, g).wait(), c)[1], 0)


def _dispatch_call(tables, h, route, n_rows, tm):
    t, d = h.shape
    buf_rows = 2 * tm + N_EXPERTS * ROW_GRANULE
    hbm = pl.BlockSpec(memory_space=pl.ANY)
    sub = 8
    xs, dest = pl.pallas_call(
        functools.partial(_dispatch_kernel, tm=tm),
        out_shape=[jax.ShapeDtypeStruct((n_rows, d), BF16), jax.ShapeDtypeStruct((t // tm, sub, tm), jnp.int32)],
        grid_spec=pltpu.PrefetchScalarGridSpec(
            num_scalar_prefetch=3,
            grid=(t // tm,),
            in_specs=[pl.BlockSpec((tm, d), lambda i, *_: (i, 0)), pl.BlockSpec((tm, LANES), lambda i, *_: (i, 0)),
                      hbm],
            out_specs=[hbm, pl.BlockSpec((1, sub, tm), lambda i, *_: (i, 0, 0))],
            scratch_shapes=[pltpu.VMEM((buf_rows, d), BF16), pltpu.SemaphoreType.DMA(())]),
        input_output_aliases={5: 0},
        compiler_params=_cparams(("arbitrary",)),
        name="moe_dispatch",
    )(*tables, h, route, jnp.zeros((n_rows, d), BF16))
    return xs, dest[:, 0, :].reshape(t), dest[:, 1, :].reshape(t)


def _moe_layer(mixer, sc, sh, g2, wr_p, w1, w3, w2, layer, lg, lb, tiles_per_batch, tm, tme, tf):
    x2d, h, route, counts = _router_call(mixer, sc, sh, wr_p, tiles_per_batch, tm)
    tables, tile_expert, n_used = _dispatch_plan(route, counts, tm, tme)
    xs, dest1, dest2 = _dispatch_call(tables, h, route, _sorted_rows(x2d.shape[0], tm, tme), tm)
    ys = _expert_call(tile_expert, n_used, xs, w1, w3, w2, layer, tme, tf)
    ya = ys.at[dest1].get(mode="promise_in_bounds")
    yb = ys.at[dest2].get(mode="promise_in_bounds")
    return _combine_call(x2d, ya, yb, route, g2, lg, lb, tiles_per_batch, tm)


def _trig_kernel(a_ref, c_ref, s_ref):
    a = a_ref[...]
    c_ref[...] = jnp.cos(a)
    s_ref[...] = jnp.sin(a)


def _trig_call(ang2d):
    n = ang2d.shape[0]
    tr = math.gcd(n, 1024)
    spec = pl.BlockSpec((tr, LANES), lambda i: (i, 0))
    return pl.pallas_call(
        _trig_kernel,
        out_shape=[jax.ShapeDtypeStruct(ang2d.shape, F32)] * 2,
        grid=(n // tr,),
        in_specs=[spec],
        out_specs=[spec, spec],
        compiler_params=_cparams(("parallel",)),
        name="rope_trig",
    )(ang2d)


def _rope_tables(positions):
    pos = positions.astype(F32)[..., None]
    lane = jnp.arange(LANES)
    inv64 = ROPE_THETA ** (-jnp.arange(0, RET_D, 2, dtype=F32) / RET_D)
    invm = ROPE_THETA ** (-jnp.arange(0, MLA_ROPE, 2, dtype=F32) / MLA_ROPE)
    n64 = inv64.shape[0]
    ang = pos * jnp.concatenate([inv64, invm])
    cos, sin = _trig_call(ang.reshape(-1, LANES))
    cos, sin = cos.reshape(ang.shape), sin.reshape(ang.shape)
    cos64, sin64, cosm, sinm = cos[..., :n64], sin[..., :n64], cos[..., n64:], sin[..., n64:]
    sign64 = jnp.where((lane % 64) < 32, -1.0, 1.0)
    c64 = jnp.tile(cos64, (1, 1, LANES // 32))
    s64 = jnp.tile(sin64, (1, 1, LANES // 32)) * sign64
    in_rope = (lane >= KR_LANE) & (lane < KR_LANE + MLA_ROPE)
    signm = jnp.where(lane < KR_LANE + MLA_ROPE // 2, -1.0, 1.0)
    cm = jnp.where(in_rope, jnp.tile(cosm, (1, 1, LANES // 16)), 1.0)
    sm = jnp.where(in_rope, jnp.tile(sinm, (1, 1, LANES // 16)) * signm, 0.0)
    return c64, s64, cm, sm


def _prep_mixer_weights(w_in, w_uq, w_ukv):
    d = w_in.shape[0]
    kr_cols = jnp.zeros((d, LANES), w_in.dtype).at[:, KR_LANE:KR_LANE + MLA_ROPE].set(
        w_in[:, MLA_OFF + MLA_Q_RANK + MLA_KV_RANK:])
    w_in_p = jnp.concatenate([w_in[:, :MLA_OFF + MLA_Q_RANK + MLA_KV_RANK], kr_cols], axis=1).astype(BF16)
    uq = w_uq.reshape(MLA_Q_RANK, MLA_HEADS, MLA_NOPE + MLA_ROPE)
    wuq_p = jnp.pad(uq, ((0, 0), (0, 0), (0, LANES - MLA_NOPE - MLA_ROPE))).reshape(MLA_Q_RANK, MLA_PAD).astype(BF16)
    ukv = w_ukv.reshape(MLA_KV_RANK, MLA_HEADS, MLA_NOPE + MLA_V)
    wkn_p = jnp.pad(ukv[:, :, :MLA_NOPE], ((0, 0), (0, 0), (0, LANES - MLA_NOPE))).reshape(MLA_KV_RANK, MLA_PAD).astype(BF16)
    wv = ukv[:, :, MLA_NOPE:].reshape(MLA_KV_RANK, MLA_VW).astype(BF16)
    return w_in_p, wuq_p, wkn_p, wv


def kernel(x, c, positions, w_in, ret_gn_g, mla_qn_g, mla_kvn_g, w_uq, w_ukv, w_out, w_ada, b_ada, ln1_g, ln1_b, ln2_g, ln2_b, w1_dense, w3_dense, w2_dense, w_router, w1_moe, w3_moe, w2_moe):
    b, s, d = x.shape
    tabs = _rope_tables(positions)
    ret_tabs = _ret_tables()
    mod = _ada_call(c, w_ada, b_ada)
    tm_tok = 512
    tiles_per_batch = s // tm_tok
    w_out_b = _cast_call(w_out)
    for l in range(w_in.shape[0]):
        sh1, sc1, g1, sh2, sc2, g2 = [mod[l, :, j * d:(j + 1) * d].reshape(b, 1, d) for j in range(6)]
        w_in_p, wuq_p, wkn_p, wv = _prep_mixer_weights(w_in[l], w_uq[l], w_ukv[l])
        (rq, rk, rv, rg), dil_views, (mq, mk, mv) = _inproj_call(
            x, sc1, sh1, w_in_p, tabs, mla_qn_g[l].reshape(1, -1), mla_kvn_g[l].reshape(1, -1), wuq_p, wkn_p, wv)
        ya = _ret_call(rq, rk, rv, rg, ret_gn_g[l].reshape(1, -1), ret_tabs)
        yb = _dil_call(dil_views)
        yc = _mla_call(mq, mk, mv)
        mixer = _mixer_operands(ya, yb, yc, w_out_b, l, x.reshape(b * s, d), g1, ln1_g[l].reshape(1, d),
                                ln1_b[l].reshape(1, d), tiles_per_batch, tm_tok)
        lg, lb = ln2_g[l].reshape(1, d), ln2_b[l].reshape(1, d)
        j = l // 2
        if l % 2 == 0:
            x2d = _ffn_call(mixer, sc2, sh2, g2, w1_dense, w3_dense, w2_dense, j, lg, lb, tiles_per_batch, tm_tok,
                            FF_CHUNK)
        else:
            wr_p = jnp.pad(w_router[j], ((0, 0), (0, LANES - N_EXPERTS)))
            x2d = _moe_layer(mixer, sc2, sh2, g2, wr_p, w1_moe, w3_moe, w2_moe, j, lg, lb, tiles_per_batch, tm_tok,
                             EXPERT_TILE, FF_CHUNK)
        x = x2d.reshape(b, s, d)
    return x
```
